```python
import jax, jax.numpy as jnp
from jax import lax
import numpy as np

D_MODEL = 1024
BATCH = 16
SEQ = 256
DEPTH = 4
DEC_BATCH = 4
DEC_SEQ = 4096
PAST_LEN = 256

GRID_W = 64
N_HEADS_A = 4
DK_A = 64
DV_A = 128
QK_W = N_HEADS_A * DK_A
V_W = N_HEADS_A * DV_A
GLA_LOWRANK = 16
GATE_NORMALIZER = 16.0
GLA_CHUNK = 64
SC_W = 512
SC_KERNEL = 3
FN_GROUPS = 4
FN_GW = 128
FN_W = FN_GROUPS * FN_GW
N_BRANCH = 3
D_FF = ((8 * D_MODEL // 3 + 255) // 256) * 256
N_MOD = 6
EPS = 1e-6
IN_SIZES = (QK_W, QK_W, V_W, V_W, GLA_LOWRANK, GLA_LOWRANK, SC_W, SC_W, SC_W, FN_W, N_BRANCH * D_MODEL)
P_IN = sum(IN_SIZES)

kernel_name = 'hybrid_gla_conv_fourier_dit_step'


def rms_norm(x, g):
    xf = x.astype(jnp.float32)
    y = xf * lax.rsqrt(jnp.mean(xf * xf, axis=-1, keepdims=True) + EPS)
    return (y * g.astype(jnp.float32)).astype(x.dtype)


def gla_scan(q, k, v, log_a, s0):
    bsz, nh, L, _ = q.shape
    n = L // GLA_CHUNK

    def blk(t):
        return t.reshape(bsz, nh, n, GLA_CHUNK, t.shape[-1])

    q, k, v, log_a = blk(q), blk(k), blk(v), blk(log_a)
    b = jnp.cumsum(log_a, axis=3)
    b_last = b[:, :, :, -1:, :]
    q_in = q * jnp.exp(b)
    k_in = k * jnp.exp(-b)
    k_out = k * jnp.exp(b_last - b)
    mask = jnp.tril(jnp.ones((GLA_CHUNK, GLA_CHUNK), dtype=bool))
    att = jnp.where(mask, jnp.einsum('bhntd,bhnsd->bhnts', q_in, k_in), 0.0)
    o_intra = jnp.einsum('bhnts,bhnse->bhnte', att, v)
    u = jnp.einsum('bhnsd,bhnse->bhnde', k_out, v)
    g = jnp.exp(b_last[:, :, :, 0, :])

    def step(s, xs):
        g_n, u_n = xs
        return g_n[..., None] * s + u_n, s

    s_final, s_start = lax.scan(step, s0, (jnp.moveaxis(g, 2, 0), jnp.moveaxis(u, 2, 0)))
    s_start = jnp.moveaxis(s_start, 0, 2)
    o_inter = jnp.einsum('bhntd,bhnde->bhnte', q_in, s_start)
    return (o_intra + o_inter).reshape(bsz, nh, L, v.shape[-1]), s_final


def gla_bidir(q, k, v, la_f, la_b, s0_f, s0_b):
    o_f, s_f = gla_scan(q, k, v, la_f, s0_f)
    flip = lambda t: jnp.flip(t, axis=2)
    o_b, s_b = gla_scan(flip(q), flip(k), flip(v), flip(la_b), s0_b)
    return o_f + flip(o_b), s_f, s_b


def dwconv3(u, w, axis):
    n = u.shape[axis]
    pad = [(0, 0)] * u.ndim
    pad[axis] = (1, 1)
    up = jnp.pad(u, pad)
    return sum(w[i] * lax.slice_in_dim(up, i, i + n, axis=axis) for i in range(SC_KERNEL))


def token_mixing(h, s0_f, s0_b, rows, p):
    bsz, L, _ = h.shape
    f32 = jnp.float32
    pts = np.cumsum(IN_SIZES)[:-1].tolist()
    q, k, v, og, gkf, gkb, sb, sc, sx, fx, mg = jnp.split(h @ p['w_in'], pts, axis=-1)

    def heads(t):
        return t.reshape(bsz, L, N_HEADS_A, -1).transpose(0, 2, 1, 3).astype(f32)

    la_f = jax.nn.log_sigmoid((gkf @ p['w_gk_f'] + p['b_gk_f']).astype(f32)) / GATE_NORMALIZER
    la_b = jax.nn.log_sigmoid((gkb @ p['w_gk_b'] + p['b_gk_b']).astype(f32)) / GATE_NORMALIZER
    o, s_f, s_b = gla_bidir(heads(q) * DK_A ** -0.5, heads(k), heads(v), heads(la_f), heads(la_b),
                            s0_f.astype(f32), s0_b.astype(f32))
    o = o.transpose(0, 2, 1, 3)
    o = o * lax.rsqrt(jnp.mean(o * o, axis=-1, keepdims=True) + EPS) * p['gla_norm'].astype(f32)
    y_a = (o.reshape(bsz, L, V_W).astype(h.dtype) * jax.nn.silu(og)) @ p['w_a_out']

    u = sc * sx
    if rows is None:
        u = dwconv3(u, p['conv_w'], axis=1)
    else:
        u = dwconv3(u.reshape(bsz, rows, GRID_W, SC_W), p['conv_w'], axis=2).reshape(bsz, L, SC_W)
    y_b = (sb * u) @ p['w_b_out']

    fr = jnp.fft.fft2(fx.reshape(bsz, L, FN_GROUPS, FN_GW).astype(f32), axes=(1, 3), norm='ortho').real
    y_c = fr.reshape(bsz, L, FN_W).astype(h.dtype) @ p['w_c_out']

    g = jax.nn.sigmoid(mg.reshape(bsz, L, N_BRANCH, D_MODEL))
    y = g[:, :, 0] * y_a + g[:, :, 1] * y_b + g[:, :, 2] * y_c
    return y @ p['w_o'], s_f, s_b


def swiglu(h, w_up, w_down):
    gate, up = jnp.split(h @ w_up, 2, axis=-1)
    return (jax.nn.silu(gate) * up) @ w_down


def trunk_layer(x, cond, s0_f, s0_b, rows, p):
    m = jax.nn.silu(cond) @ p['w_ada'] + p['b_ada']
    sh1, sc1, g1, sh2, sc2, g2 = jnp.split(m[:, None, :], N_MOD, axis=-1)
    h = rms_norm(x, p['norm1']) * (1 + sc1) + sh1
    mix, s_f, s_b = token_mixing(h, s0_f, s0_b, rows, p)
    x = x + g1 * mix
    h = rms_norm(x, p['norm2']) * (1 + sc2) + sh2
    x = x + g2 * swiglu(h, p['w_up'], p['w_down'])
    return x, s_f, s_b


def setup_inputs(seed: int = 0) -> dict:
    key = jax.random.key(seed)
    ks = jax.random.split(key, 23)
    f = jnp.float32
    nrm = jax.random.normal

    def w(k, shape, fan_in):
        return nrm(k, shape, f) * fan_in ** -0.5

    def gain(k, shape):
        return 1.0 + 0.02 * nrm(k, shape, f)

    return {
        'x_prompt': nrm(ks[0], (BATCH, SEQ, D_MODEL), f),
        'x_sample': nrm(ks[1], (DEC_BATCH, DEC_SEQ, D_MODEL), f),
        'state_gla': nrm(ks[2], (DEC_BATCH, DEPTH, 2, N_HEADS_A, DK_A, DV_A), f),
        'c': nrm(ks[3], (DEC_BATCH, D_MODEL), f),
        'c_ctx': nrm(ks[4], (D_MODEL,), f),
        'w_ada': w(ks[5], (DEPTH, D_MODEL, N_MOD * D_MODEL), D_MODEL),
        'b_ada': 0.02 * nrm(ks[6], (DEPTH, N_MOD * D_MODEL), f),
        'norm1': gain(ks[7], (DEPTH, D_MODEL)),
        'norm2': gain(ks[8], (DEPTH, D_MODEL)),
        'w_in': w(ks[9], (DEPTH, D_MODEL, P_IN), D_MODEL),
        'w_gk_f': w(ks[10], (DEPTH, GLA_LOWRANK, QK_W), GLA_LOWRANK),
        'b_gk_f': 0.02 * nrm(ks[11], (DEPTH, QK_W), f),
        'w_gk_b': w(ks[12], (DEPTH, GLA_LOWRANK, QK_W), GLA_LOWRANK),
        'b_gk_b': 0.02 * nrm(ks[13], (DEPTH, QK_W), f),
        'gla_norm': gain(ks[14], (DEPTH, DV_A)),
        'w_a_out': w(ks[15], (DEPTH, V_W, D_MODEL), V_W),
        'conv_w': w(ks[16], (DEPTH, SC_KERNEL, SC_W), SC_KERNEL),
        'w_b_out': w(ks[17], (DEPTH, SC_W, D_MODEL), SC_W),
        'w_c_out': w(ks[18], (DEPTH, FN_W, D_MODEL), FN_W),
        'w_o': w(ks[19], (DEPTH, D_MODEL, D_MODEL), D_MODEL),
        'w_up': w(ks[20], (DEPTH, D_MODEL, 2 * D_FF), D_MODEL),
        'w_down': w(ks[21], (DEPTH, D_FF, D_MODEL), D_FF),
        'norm_f': gain(ks[22], (D_MODEL,)),
    }


def reference(x_prompt, x_sample, state_gla, c, c_ctx, w_ada, b_ada, norm1, norm2, w_in,
              w_gk_f, b_gk_f, w_gk_b, b_gk_b, gla_norm, w_a_out, conv_w, w_b_out, w_c_out,
              w_o, w_up, w_down, norm_f):
    xp, xs = x_prompt, x_sample
    rows = x_sample.shape[1] // GRID_W
    zero_state = jnp.zeros((x_prompt.shape[0], N_HEADS_A, DK_A, DV_A), jnp.float32)
    cond_ctx = c_ctx[None, :]
    ctx_states = []
    for l in range(DEPTH):
        p = {'w_ada': w_ada[l], 'b_ada': b_ada[l], 'norm1': norm1[l], 'norm2': norm2[l],
             'w_in': w_in[l], 'w_gk_f': w_gk_f[l], 'b_gk_f': b_gk_f[l], 'w_gk_b': w_gk_b[l],
             'b_gk_b': b_gk_b[l], 'gla_norm': gla_norm[l], 'w_a_out': w_a_out[l],
             'conv_w': conv_w[l], 'w_b_out': w_b_out[l], 'w_c_out': w_c_out[l], 'w_o': w_o[l],
             'w_up': w_up[l], 'w_down': w_down[l]}
        xp, s_f, s_b = trunk_layer(xp, cond_ctx, zero_state, zero_state, None, p)
        ctx_states.append(jnp.stack([s_f, s_b], axis=1))
        xs, _, _ = trunk_layer(xs, c, state_gla[:, l, 0], state_gla[:, l, 1], rows, p)
    new_state_gla = jnp.stack(ctx_states, axis=1).astype(x_prompt.dtype)
    y_prompt = rms_norm(xp, norm_f)
    y_sample = rms_norm(xs, norm_f)
    return (y_prompt, y_sample, new_state_gla)
```

```python
import functools
import math

import numpy as np
import jax
import jax.numpy as jnp
from jax import lax
from jax.experimental import pallas as pl
from jax.experimental.pallas import tpu as pltpu

F32 = jnp.float32
BF16 = jnp.bfloat16

GRID_W = 64
N_HEADS = 4
DK = 64
DV = 128
V_W = N_HEADS * DV
FB_W = N_HEADS * 2 * DK
LOWRANK = 16
GATE_NORMALIZER = 16.0
CHUNK = 64
SUB = 256
CPS = SUB // CHUNK
SC_W = 512
FN_GROUPS = 4
FN_GW = 128
FN_W = FN_GROUPS * FN_GW
N_MOD = 6
EPS = 1e-6
GK_PAD = 128

TM_GLA = 1024
TM_MIX = 512
TM_OUT = 512
TM_FFN = 1024
FF_CHUNK = 256
CT_NB = 8
CT_CB = 4096
VMEM_LIMIT = 56 * 1024 * 1024


def _dot(a, b):
    return jnp.dot(a, b, preferred_element_type=F32)


def _dot_nt(a, b):
    return lax.dot_general(a, b, (((1,), (1,)), ((), ())), preferred_element_type=F32)


def _dot_tn(a, b):
    return lax.dot_general(a, b, (((0,), (0,)), ((), ())), preferred_element_type=F32)


def _sigmoid(x):
    return 1.0 / (1.0 + jnp.exp(-x))


def _silu(x):
    return x * _sigmoid(x)


def _rms(x):
    return x * lax.rsqrt(jnp.mean(x * x, axis=-1, keepdims=True) + EPS)


def _params(sem):
    return pltpu.CompilerParams(dimension_semantics=sem, vmem_limit_bytes=VMEM_LIMIT)


def _const_spec(shape):
    nd = len(shape)
    return pl.BlockSpec(shape, lambda *_: (0,) * nd)


def _ada_kernel(c_ref, w_ref, b_ref, o_ref):
    s = _silu(c_ref[...]).astype(BF16)
    o_ref[...] = _dot(s, w_ref[...].astype(BF16)) + b_ref[...]


def _ada(cond, w_ada, b_ada):
    depth, d, n = w_ada.shape
    rows = cond.shape[0]
    nb = 1536
    return pl.pallas_call(
        _ada_kernel,
        grid=(depth, n // nb),
        in_specs=[
            pl.BlockSpec((rows, d), lambda l, j: (0, 0)),
            pl.BlockSpec((None, d, nb), lambda l, j: (l, 0, j)),
            pl.BlockSpec((None, 1, nb), lambda l, j: (l, 0, j)),
        ],
        out_specs=pl.BlockSpec((None, rows, nb), lambda l, j: (l, 0, j)),
        out_shape=jax.ShapeDtypeStruct((depth, rows, n), F32),
        compiler_params=_params(("arbitrary", "arbitrary")),
        name="ada",
    )(cond, w_ada, b_ada.reshape(depth, 1, n))


def _prenorm_kernel(x_ref, mod_ref, n_ref, h_ref):
    m = mod_ref[...]
    h = _rms(x_ref[...]) * n_ref[...] * (1.0 + m[1:2]) + m[0:1]
    h_ref[...] = h.astype(BF16)


def _gla_in_kernel(h_ref, wqk_ref, wv_ref, wog_ref, wgk_ref, wgk2_ref, bgk_ref,
                   og_ref, qin_ref, oin_ref, u_ref, g_ref,
                   q_s, k_s, v_s, la_s):
    h = h_ref[...]
    qk = _dot(h, wqk_ref[...])
    q_s[...] = qk[:, :FB_W] * (DK ** -0.5)
    k_s[...] = qk[:, FB_W:]
    v_s[...] = _dot(h, wv_ref[...]).astype(BF16)
    og_ref[...] = _dot(h, wog_ref[...]).astype(BF16)
    gk = _dot(h, wgk_ref[...]).astype(BF16)
    lp = _dot(gk, wgk2_ref[...]) + bgk_ref[...]
    la_s[...] = (jnp.minimum(lp, 0.0) - jnp.log(1.0 + jnp.exp(-jnp.abs(lp)))) * (1.0 / GATE_NORMALIZER)

    row = lax.broadcasted_iota(jnp.int32, (SUB, SUB), 0)
    col = lax.broadcasted_iota(jnp.int32, (SUB, SUB), 1)
    same = (row & -CHUNK) == (col & -CHUNK)
    lower = same & (col <= row)
    upper = same & (col >= row)
    tri = jnp.where(lower, 1.0, 0.0).astype(BF16)
    is_f = (lax.broadcasted_iota(jnp.int32, (SUB, FB_W), 1) & (2 * DK - 1)) < DK
    rchunk = lax.broadcasted_iota(jnp.int32, (SUB, 2 * DK), 0) & -CHUNK

    def sub_tile(s, carry):
        r0 = pl.multiple_of(s * SUB, SUB)
        rows = pl.ds(r0, SUB)
        la = la_s[rows, :]
        hi = la.astype(BF16)
        r1 = la - hi.astype(F32)
        mid = r1.astype(BF16)
        lo = (r1 - mid.astype(F32)).astype(BF16)
        pre = _dot(tri, hi) + _dot(tri, mid) + _dot(tri, lo)
        tot_rows = [pre[c * CHUNK + CHUNK - 1:c * CHUNK + CHUNK, :] for c in range(CPS)]
        tot = jnp.concatenate([jnp.broadcast_to(t, (CHUNK, FB_W)) for t in tot_rows], axis=0)
        b = jnp.where(is_f, pre, tot - pre + la)
        q = q_s[rows, :]
        k = k_s[rows, :]
        qin = (q * jnp.exp(b)).astype(BF16)
        kin = (k * jnp.exp(-b)).astype(BF16)
        kout = (k * jnp.exp(tot - b)).astype(BF16)
        qin_ref[rows, :] = qin
        zero = jnp.zeros_like(qin)
        qf = jnp.where(is_f, qin, zero)
        qb = jnp.where(is_f, zero, qin)
        v = v_s[rows, :]
        for hh in range(N_HEADS):
            fb = slice(hh * 2 * DK, (hh + 1) * 2 * DK)
            vs = slice(hh * DV, (hh + 1) * DV)
            af = _dot_nt(qf[:, fb], kin[:, fb])
            ab = _dot_nt(qb[:, fb], kin[:, fb])
            att = (jnp.where(lower, af, 0.0) + jnp.where(upper, ab, 0.0)).astype(BF16)
            oin_ref[rows, vs] = _dot(att, v[:, vs])
            ko = kout[:, fb]
            kbd = jnp.concatenate([jnp.where(rchunk == c * CHUNK, ko, jnp.zeros_like(ko)) for c in range(CPS)],
                                  axis=1)
            u_ref[s, hh] = _dot_tn(v[:, vs], kbd)
            g_ref[s, hh] = jnp.concatenate([jnp.exp(t[:, fb]) for t in tot_rows], axis=1)
        return carry

    lax.fori_loop(0, h_ref.shape[0] // SUB, sub_tile, 0)


def _mix_in_kernel(h_ref, wc_ref, wf_ref, wm_ref, cw_ref, cs_ref, wb_ref,
                   ab_ref, g0_ref, g2_ref, yp_ref, *, n_ctx_tiles, period_ctx, period_lat):
    i = pl.program_id(0)
    h = h_ref[...]
    s = _dot(h, wc_ref[...])
    sb, sc, sx = s[:, :SC_W], s[:, SC_W:2 * SC_W], s[:, 2 * SC_W:]
    u = sc * sx
    tm = h_ref.shape[0]
    period = jnp.where(i < n_ctx_tiles, period_ctx, period_lat)
    pos = lax.broadcasted_iota(jnp.int32, (tm, 1), 0) & (period - 1)
    up = jnp.where(pos == 0, 0.0, pltpu.roll(u, 1, axis=0))
    un = jnp.where(pos == period - 1, 0.0, pltpu.roll(u, tm - 1, axis=0))
    cw = cw_ref[...]
    conv = cw[0:1] * up + cw[1:2] * u + cw[2:3] * un
    yb = _dot((sb * conv).astype(BF16), wb_ref[...])

    fx = _dot(h, wf_ref[...]).astype(BF16)
    for g in range(FN_GROUPS):
        gs = slice(g * FN_GW, (g + 1) * FN_GW)
        r = _dot(fx[:, gs], cs_ref[...])
        ab_ref[0, :, gs] = r[:, :FN_GW].astype(BF16)
        ab_ref[1, :, gs] = r[:, FN_GW:].astype(BF16)

    d = g0_ref.shape[-1]
    g0_ref[...] = _sigmoid(_dot(h, wm_ref[:, 0:d])).astype(BF16)
    yp_ref[...] = (_sigmoid(_dot(h, wm_ref[:, d:2 * d])) * yb).astype(BF16)
    g2_ref[...] = _sigmoid(_dot(h, wm_ref[:, 2 * d:3 * d])).astype(BF16)


def _scan_kernel(u_ref, g_ref, s0_ref, ss_ref, sfin_ref, *, n_sub):
    row_f = lax.broadcasted_iota(jnp.int32, (2 * DK, DV), 0) < DK
    lane_f = lax.broadcasted_iota(jnp.int32, (DV, 2 * DK), 1) < DK
    s0 = s0_ref[...]

    def fwd(j, st):
        for c in range(CPS):
            cs = slice(c * 2 * DK, (c + 1) * 2 * DK)
            ss_ref[j, :, c * DV:(c + 1) * DV] = st.T.astype(BF16)
            st = g_ref[j, :, cs] * st + u_ref[j, :, cs]
        return st

    sf = lax.fori_loop(0, n_sub, fwd, s0)

    def bwd(jj, st):
        j = n_sub - 1 - jj
        for c in range(CPS - 1, -1, -1):
            cs = slice(c * 2 * DK, (c + 1) * 2 * DK)
            vs = slice(c * DV, (c + 1) * DV)
            ss_ref[j, :, vs] = jnp.where(row_f, ss_ref[j, :, vs], st.T.astype(BF16))
            st = g_ref[j, :, cs] * st + u_ref[j, :, cs]
        return st

    sb = lax.fori_loop(0, n_sub, bwd, s0)
    sfin_ref[...] = jnp.where(lane_f, sf, sb)


def _scan(u, g, s0, n_seq, n_sub, sub_block0):
    return pl.pallas_call(
        functools.partial(_scan_kernel, n_sub=n_sub),
        grid=(n_seq, N_HEADS),
        in_specs=[
            pl.BlockSpec((n_sub, None, DV, CPS * 2 * DK), lambda b, hh: (sub_block0 + b, hh, 0, 0)),
            pl.BlockSpec((n_sub, None, 1, CPS * 2 * DK), lambda b, hh: (sub_block0 + b, hh, 0, 0)),
            pl.BlockSpec((None, None, DV, 2 * DK), lambda b, hh: (b, hh, 0, 0)),
        ],
        out_specs=[
            pl.BlockSpec((n_sub, None, 2 * DK, CPS * DV), lambda b, hh: (b, hh, 0, 0)),
            pl.BlockSpec((None, None, DV, 2 * DK), lambda b, hh: (b, hh, 0, 0)),
        ],
        out_shape=[
            jax.ShapeDtypeStruct((n_seq * n_sub, N_HEADS, 2 * DK, CPS * DV), BF16),
            jax.ShapeDtypeStruct((n_seq, N_HEADS, DV, 2 * DK), F32),
        ],
        compiler_params=_params(("arbitrary", "arbitrary")),
        name="scan",
    )(u, g, s0)


def _dft_direct_kernel(ab_ref, c_ref, ns_ref, o_ref):
    o_ref[...] = (_dot(c_ref[...], ab_ref[0]) + _dot(ns_ref[...], ab_ref[1])).astype(BF16)


def _ct1_kernel(ab_ref, m_ref, tc_ref, ts_ref, y_ref, *, l1):
    m = m_ref[...]
    for jn in range(CT_NB):
        cols = slice(jn * FN_W, (jn + 1) * FN_W)
        rhs = jnp.concatenate([ab_ref[0, :, cols], ab_ref[1, :, cols]], axis=0)
        res = _dot(m, rhs)
        yr, yi = res[:l1], res[l1:]
        c = jnp.concatenate([tc_ref[jn]] * (FN_W // 128), axis=1)
        s = jnp.concatenate([ts_ref[jn]] * (FN_W // 128), axis=1)
        y_ref[0, jn] = (yr * c + yi * s).astype(BF16)
        y_ref[1, jn] = (yi * c - yr * s).astype(BF16)


def _ct2_kernel(y_ref, m_ref, o_ref):
    rhs = jnp.concatenate([y_ref[0], y_ref[1]], axis=0)
    o_ref[...] = _dot(m_ref[...], rhs).astype(BF16)


def _cos_sin(n_out, n_in, period, scale):
    idx = (np.arange(n_out)[:, None] * np.arange(n_in)[None, :]) % period
    ang = 2.0 * np.pi * idx / period
    return np.cos(ang) * scale, np.sin(ang) * scale


def _mix_out_kernel(x_ref, oin_ref, qin_ref, ssc_ref, ssl_ref, og_ref, frc_ref, frl_ref,
                    g0_ref, g2_ref, yp_ref, mod_ref, gn_ref, n2_ref, wa_ref, wc_ref, wo_ref,
                    x1_ref, h2_ref, o_s, *, n_ctx_tiles):
    is_ctx = pl.program_id(0) < n_ctx_tiles
    for s in range(x_ref.shape[0] // SUB):
        for hh in range(N_HEADS):
            st = jnp.where(is_ctx, ssc_ref[s, hh], ssl_ref[s, hh])
            fb = slice(hh * 2 * DK, (hh + 1) * 2 * DK)
            vs = slice(hh * DV, (hh + 1) * DV)
            for c in range(CPS):
                rows = slice(s * SUB + c * CHUNK, s * SUB + (c + 1) * CHUNK)
                o_s[rows, vs] = oin_ref[rows, vs] + _dot(qin_ref[rows, fb], st[:, c * DV:(c + 1) * DV])
    gn = gn_ref[...]
    parts = []
    for hh in range(N_HEADS):
        vs = slice(hh * DV, (hh + 1) * DV)
        parts.append((_rms(o_s[:, vs]) * gn * _silu(og_ref[:, vs].astype(F32))).astype(BF16))
    ya = _dot(jnp.concatenate(parts, axis=1), wa_ref[...])
    fr = jnp.where(is_ctx, frc_ref[...], frl_ref[...])
    yc = _dot(fr, wc_ref[...])
    y = g0_ref[...].astype(F32) * ya + yp_ref[...].astype(F32) + g2_ref[...].astype(F32) * yc
    mix = _dot(y.astype(BF16), wo_ref[...])
    m = mod_ref[...]
    x1 = x_ref[...] + m[2:3] * mix
    x1_ref[...] = x1
    h2_ref[...] = (_rms(x1) * n2_ref[...] * (1.0 + m[4:5]) + m[3:4]).astype(BF16)


def _ffn_kernel(h2_ref, x1_ref, wg_ref, wu_ref, wd_ref, mod_ref, modn_ref, nn_ref, *rest, final):
    if final:
        x2_ref, acc_s = rest
    else:
        x2_ref, hn_ref, acc_s = rest
    j = pl.program_id(1)
    h2 = h2_ref[...]
    act = (_silu(_dot(h2, wg_ref[...])) * _dot(h2, wu_ref[...])).astype(BF16)
    part = _dot(act, wd_ref[...])

    @pl.when(j == 0)
    def _():
        acc_s[...] = part

    @pl.when(j > 0)
    def _():
        acc_s[...] += part

    @pl.when(j == pl.num_programs(1) - 1)
    def _():
        x2 = x1_ref[...] + mod_ref[5:6, :] * acc_s[...]
        if final:
            x2_ref[...] = _rms(x2) * nn_ref[...]
        else:
            x2_ref[...] = x2
            hn_ref[...] = (_rms(x2) * nn_ref[...] * (1.0 + modn_ref[1:2, :]) + modn_ref[0:1, :]).astype(BF16)


def kernel(x_prompt, x_sample, state_gla, c, c_ctx, w_ada, b_ada, norm1, norm2, w_in, w_gk_f, b_gk_f,
           w_gk_b, b_gk_b, gla_norm, w_a_out, conv_w, w_b_out, w_c_out, w_o, w_up, w_down, norm_f):
    b_ctx, seq, d = x_prompt.shape
    b_lat, dec_seq, _ = x_sample.shape
    depth = w_ada.shape[0]
    d_ff = w_down.shape[1]
    nc, nl = b_ctx * seq, b_lat * dec_seq
    nt = nc + nl
    l1, l2 = dec_seq // GRID_W, GRID_W
    assert seq == SUB and d_ff % FF_CHUNK == 0
    assert (nc // l2) % l1 == 0 and l2 % CT_NB == 0 and (l1 * FN_W) % CT_CB == 0
    n_sub_tot, n_sub_lat = nt // SUB, dec_seq // SUB
    assert (nc // SUB) % n_sub_lat == 0
    n_cond = -(-(1 + b_lat) // 8) * 8

    class Tiling:
        def __init__(self, tm):
            assert nc % tm == 0 and dec_seq % tm == 0
            self.tm, self.n, self.n_ctx, self.per_lat = tm, nt // tm, nc // tm, dec_seq // tm

        def cond(self, i):
            return jnp.where(i < self.n_ctx, 0, 1 + (i - self.n_ctx) // self.per_lat)

        def ctx_blk(self, i):
            return jnp.minimum(i, self.n_ctx - 1)

        def lat_blk(self, i):
            return jnp.maximum(i - self.n_ctx, 0)

        def tok(self, w):
            return pl.BlockSpec((self.tm, w), lambda i, *_: (i, 0))

        def tok_ctx(self, w):
            return pl.BlockSpec((self.tm, w), lambda i, *_: (self.ctx_blk(i), 0))

        def tok_lat(self, w):
            return pl.BlockSpec((self.tm, w), lambda i, *_: (self.lat_blk(i), 0))

        def mod(self):
            return pl.BlockSpec((None, N_MOD, d), lambda i, *_: (self.cond(i), 0, 0))

    t_gla, t_mix, t_out, t_ffn = Tiling(TM_GLA), Tiling(TM_MIX), Tiling(TM_OUT), Tiling(TM_FFN)

    cond = jnp.concatenate([c_ctx[None, :], c, jnp.zeros((n_cond - 1 - b_lat, d), F32)], axis=0)
    mods = _ada(cond, w_ada, b_ada).reshape(depth, n_cond, N_MOD, d)

    cc, sc_ = _cos_sin(FN_GW, FN_GW, FN_GW, FN_GW ** -0.5)
    cs_tab = jnp.asarray(np.concatenate([cc, sc_], axis=1), F32).astype(BF16)
    cl, sl = _cos_sin(seq, seq, seq, seq ** -0.5)
    c_ctx_tab = jnp.asarray(cl, F32).astype(BF16)
    ns_ctx_tab = jnp.asarray(-sl, F32).astype(BF16)
    c1, s1 = _cos_sin(l1, l1, l1, l1 ** -0.5)
    m1_tab = jnp.asarray(np.block([[c1, -s1], [-s1, -c1]]), F32).astype(BF16)
    tcn, tsn = _cos_sin(l2, l1, l1 * l2, 1.0)
    tc_tab = jnp.asarray(np.repeat(tcn[:, :, None], 128, axis=2), F32)
    ts_tab = jnp.asarray(np.repeat(tsn[:, :, None], 128, axis=2), F32)
    c2, s2 = _cos_sin(l2, l2, l2, l2 ** -0.5)
    m3_tab = jnp.asarray(np.concatenate([c2, s2], axis=1), F32).astype(BF16)

    x = jnp.concatenate([x_prompt.reshape(nc, d), x_sample.reshape(nl, d)], axis=0)

    h = pl.pallas_call(
        _prenorm_kernel,
        grid=(t_ffn.n,),
        in_specs=[t_ffn.tok(d), t_ffn.mod(), _const_spec((1, d))],
        out_specs=t_ffn.tok(d),
        out_shape=jax.ShapeDtypeStruct((nt, d), BF16),
        compiler_params=_params(("arbitrary",)),
        name="prenorm",
    )(x, mods[0], norm1[0][None, :])

    offs = np.cumsum([0, 256, 256, 512, 512, LOWRANK, LOWRANK, SC_W, SC_W, SC_W, FN_W, 3 * d])
    new_states = []
    y_out = None
    for l in range(depth):
        w = w_in[l]
        seg = lambda a: w[:, offs[a]:offs[a + 1]]

        def dup_heads(m):
            m4 = m.reshape(d, N_HEADS, 1, DK)
            return jnp.broadcast_to(m4, (d, N_HEADS, 2, DK)).reshape(d, FB_W)

        wqk = jnp.concatenate([dup_heads(seg(0)), dup_heads(seg(1))], axis=1).astype(BF16)
        wv = seg(2).astype(BF16)
        wog = seg(3).astype(BF16)
        wgk = jnp.concatenate([seg(4), seg(5), jnp.zeros((d, GK_PAD - 2 * LOWRANK), F32)], axis=1).astype(BF16)
        zf = jnp.zeros((LOWRANK, N_HEADS, DK), F32)
        top = jnp.stack([w_gk_f[l].reshape(LOWRANK, N_HEADS, DK), zf], axis=2).reshape(LOWRANK, FB_W)
        bot = jnp.stack([zf, w_gk_b[l].reshape(LOWRANK, N_HEADS, DK)], axis=2).reshape(LOWRANK, FB_W)
        wgk2 = jnp.concatenate([top, bot, jnp.zeros((GK_PAD - 2 * LOWRANK, FB_W), F32)], axis=0).astype(BF16)
        bgk = jnp.stack([b_gk_f[l].reshape(N_HEADS, DK), b_gk_b[l].reshape(N_HEADS, DK)], axis=1).reshape(1, FB_W)

        tok = t_gla.tok
        og, qin, oin, u, g = pl.pallas_call(
            _gla_in_kernel,
            grid=(t_gla.n,),
            in_specs=[tok(d), _const_spec((d, 2 * FB_W)), _const_spec((d, V_W)), _const_spec((d, V_W)),
                      _const_spec((d, GK_PAD)), _const_spec((GK_PAD, FB_W)), _const_spec((1, FB_W))],
            out_specs=[tok(V_W), tok(FB_W), tok(V_W),
                       pl.BlockSpec((TM_GLA // SUB, N_HEADS, DV, CPS * 2 * DK), lambda i: (i, 0, 0, 0)),
                       pl.BlockSpec((TM_GLA // SUB, N_HEADS, 1, CPS * 2 * DK), lambda i: (i, 0, 0, 0))],
            out_shape=[jax.ShapeDtypeStruct((nt, V_W), BF16), jax.ShapeDtypeStruct((nt, FB_W), BF16),
                       jax.ShapeDtypeStruct((nt, V_W), F32),
                       jax.ShapeDtypeStruct((n_sub_tot, N_HEADS, DV, CPS * 2 * DK), F32),
                       jax.ShapeDtypeStruct((n_sub_tot, N_HEADS, 1, CPS * 2 * DK), F32)],
            scratch_shapes=[pltpu.VMEM((TM_GLA, FB_W), F32), pltpu.VMEM((TM_GLA, FB_W), F32),
                            pltpu.VMEM((TM_GLA, V_W), BF16), pltpu.VMEM((TM_GLA, FB_W), F32)],
            compiler_params=_params(("arbitrary",)),
            name="gla_in",
        )(h, wqk, wv, wog, wgk, wgk2, bgk)

        wc = w[:, offs[6]:offs[9]].astype(BF16)
        wf = seg(9).astype(BF16)
        wm = seg(10).astype(BF16)
        tok = t_mix.tok
        ab, g0, g2, yp = pl.pallas_call(
            functools.partial(_mix_in_kernel, n_ctx_tiles=t_mix.n_ctx, period_ctx=seq, period_lat=GRID_W),
            grid=(t_mix.n,),
            in_specs=[tok(d), _const_spec((d, 3 * SC_W)), _const_spec((d, FN_W)), _const_spec((d, 3 * d)),
                      _const_spec((3, SC_W)), _const_spec((FN_GW, 2 * FN_GW)), _const_spec((SC_W, d))],
            out_specs=[pl.BlockSpec((2, TM_MIX, FN_W), lambda i: (0, i, 0)), tok(d), tok(d), tok(d)],
            out_shape=[jax.ShapeDtypeStruct((2, nt, FN_W), BF16)] + [jax.ShapeDtypeStruct((nt, d), BF16)] * 3,
            compiler_params=_params(("arbitrary",)),
            name="mix_in",
        )(h, wc, wf, wm, conv_w[l], cs_tab, w_b_out[l].astype(BF16))

        s0_ctx = jnp.zeros((b_ctx, N_HEADS, DV, 2 * DK), F32)
        s0_lat = state_gla[:, l].transpose(0, 2, 4, 1, 3).reshape(b_lat, N_HEADS, DV, 2 * DK)
        ss_ctx, sfin = _scan(u, g, s0_ctx, b_ctx, seq // SUB, 0)
        ss_lat, _ = _scan(u, g, s0_lat, b_lat, n_sub_lat, (nc // SUB) // n_sub_lat)
        new_states.append(sfin.reshape(b_ctx, N_HEADS, DV, 2, DK).transpose(0, 3, 1, 4, 2))

        fr_ctx = pl.pallas_call(
            _dft_direct_kernel,
            grid=(b_ctx,),
            in_specs=[pl.BlockSpec((2, seq, FN_W), lambda s: (0, s, 0)),
                      _const_spec((seq, seq)), _const_spec((seq, seq))],
            out_specs=pl.BlockSpec((seq, FN_W), lambda s: (s, 0)),
            out_shape=jax.ShapeDtypeStruct((nc, FN_W), BF16),
            compiler_params=_params(("arbitrary",)),
            name="dft_ctx",
        )(ab, c_ctx_tab, ns_ctx_tab)
        lat_row0 = (nc // l2) // l1
        yct = pl.pallas_call(
            functools.partial(_ct1_kernel, l1=l1),
            grid=(b_lat, l2 // CT_NB),
            in_specs=[pl.BlockSpec((2, l1, CT_NB * FN_W), lambda b, j: (0, lat_row0 + b, j)),
                      _const_spec((2 * l1, 2 * l1)),
                      pl.BlockSpec((CT_NB, l1, 128), lambda b, j: (j, 0, 0)),
                      pl.BlockSpec((CT_NB, l1, 128), lambda b, j: (j, 0, 0))],
            out_specs=pl.BlockSpec((None, 2, CT_NB, l1, FN_W), lambda b, j: (b, 0, j, 0, 0)),
            out_shape=jax.ShapeDtypeStruct((b_lat, 2, l2, l1, FN_W), BF16),
            compiler_params=_params(("arbitrary", "arbitrary")),
            name="dft_lat1",
        )(ab.reshape(2, nt // l2, l2 * FN_W), m1_tab, tc_tab, ts_tab)
        fr_lat = pl.pallas_call(
            _ct2_kernel,
            grid=(b_lat, l1 * FN_W // CT_CB),
            in_specs=[pl.BlockSpec((None, 2, l2, CT_CB), lambda b, j: (b, 0, 0, j)),
                      _const_spec((l2, 2 * l2))],
            out_specs=pl.BlockSpec((None, l2, CT_CB), lambda b, j: (b, 0, j)),
            out_shape=jax.ShapeDtypeStruct((b_lat, l2, l1 * FN_W), BF16),
            compiler_params=_params(("arbitrary", "arbitrary")),
            name="dft_lat2",
        )(yct.reshape(b_lat, 2, l2, l1 * FN_W), m3_tab).reshape(nl, FN_W)

        tok = t_out.tok
        ss_spec = lambda blk: pl.BlockSpec((TM_OUT // SUB, N_HEADS, 2 * DK, CPS * DV),
                                           lambda i: (blk(i), 0, 0, 0))
        x1, h2 = pl.pallas_call(
            functools.partial(_mix_out_kernel, n_ctx_tiles=t_out.n_ctx),
            grid=(t_out.n,),
            in_specs=[tok(d), tok(V_W), tok(FB_W), ss_spec(t_out.ctx_blk), ss_spec(t_out.lat_blk), tok(V_W),
                      t_out.tok_ctx(FN_W), t_out.tok_lat(FN_W), tok(d), tok(d), tok(d), t_out.mod(),
                      _const_spec((1, DV)), _const_spec((1, d)),
                      _const_spec((V_W, d)), _const_spec((FN_W, d)), _const_spec((d, d))],
            out_specs=[tok(d), tok(d)],
            out_shape=[jax.ShapeDtypeStruct((nt, d), F32), jax.ShapeDtypeStruct((nt, d), BF16)],
            scratch_shapes=[pltpu.VMEM((TM_OUT, V_W), F32)],
            compiler_params=_params(("arbitrary",)),
            name="mix_out",
        )(x, oin, qin, ss_ctx, ss_lat, og, fr_ctx, fr_lat, g0, g2, yp, mods[l],
          gla_norm[l][None, :], norm2[l][None, :],
          w_a_out[l].astype(BF16), w_c_out[l].astype(BF16), w_o[l].astype(BF16))

        final = l == depth - 1
        wup = w_up[l].astype(BF16)
        n_fc = d_ff // FF_CHUNK
        tok = t_ffn.tok
        nxt = depth - 1 if final else l + 1
        outs = pl.pallas_call(
            functools.partial(_ffn_kernel, final=final),
            grid=(t_ffn.n, n_fc),
            in_specs=[tok(d), tok(d),
                      pl.BlockSpec((d, FF_CHUNK), lambda i, j: (0, j)),
                      pl.BlockSpec((d, FF_CHUNK), lambda i, j: (0, n_fc + j)),
                      pl.BlockSpec((FF_CHUNK, d), lambda i, j: (j, 0)),
                      t_ffn.mod(), t_ffn.mod(), pl.BlockSpec((1, d), lambda i, j: (0, 0))],
            out_specs=[tok(d)] if final else [tok(d), tok(d)],
            out_shape=[jax.ShapeDtypeStruct((nt, d), F32)] + ([] if final else [jax.ShapeDtypeStruct((nt, d), BF16)]),
            scratch_shapes=[pltpu.VMEM((TM_FFN, d), F32)],
            compiler_params=_params(("arbitrary", "arbitrary")),
            name="ffn",
        )(h2, x1, wup, wup, w_down[l].astype(BF16), mods[l], mods[nxt],
          (norm_f if final else norm1[nxt])[None, :])
        if final:
            y_out = outs[0]
        else:
            x, h = outs

    y_prompt = y_out[:nc].reshape(b_ctx, seq, d)
    y_sample = y_out[nc:].reshape(b_lat, dec_seq, d)
    new_state_gla = jnp.stack(new_states, axis=1).astype(x_prompt.dtype)
    return (y_prompt, y_sample, new_state_gla)
```

```python
import functools
import math

import numpy as np
import jax
import jax.numpy as jnp
from jax import lax
from jax.experimental import pallas as pl
from jax.experimental.pallas import tpu as pltpu

F32 = jnp.float32
BF16 = jnp.bfloat16

GRID_W = 64
N_HEADS = 4
DK = 64
DV = 128
V_W = N_HEADS * DV
FB_W = N_HEADS * 2 * DK
LOWRANK = 16
GATE_NORMALIZER = 16.0
CHUNK = 64
SUB = 256
CPS = SUB // CHUNK
SC_W = 512
FN_GROUPS = 4
FN_GW = 128
FN_W = FN_GROUPS * FN_GW
N_MOD = 6
EPS = 1e-6
GK_PAD = 128

TM_GLA = 1024
TM_MIX = 512
TM_OUT = 512
TM_FFN = 512
FF_CHUNK = 512
CT_NB = 16
CT_KB = 16
VMEM_LIMIT = 56 * 1024 * 1024


def _dot(a, b):
    return jnp.dot(a, b, preferred_element_type=F32)


def _dot_nt(a, b):
    return lax.dot_general(a, b, (((1,), (1,)), ((), ())), preferred_element_type=F32)


def _dot_tn(a, b):
    return lax.dot_general(a, b, (((0,), (0,)), ((), ())), preferred_element_type=F32)


def _sigmoid(x):
    return 1.0 / (1.0 + jnp.exp(-x))


def _silu(x):
    return x * _sigmoid(x)


def _rms(x):
    return x * lax.rsqrt(jnp.mean(x * x, axis=-1, keepdims=True) + EPS)


def _params(sem):
    return pltpu.CompilerParams(dimension_semantics=sem, vmem_limit_bytes=VMEM_LIMIT)


def _const_spec(shape):
    nd = len(shape)
    return pl.BlockSpec(shape, lambda *_: (0,) * nd, pipeline_mode=pl.Buffered(1))


def _ada_kernel(c_ref, w_ref, b_ref, o_ref):
    s = _silu(c_ref[...]).astype(BF16)
    o_ref[...] = _dot(s, w_ref[...].astype(BF16)) + b_ref[...]


def _ada(cond, w_ada, b_ada):
    depth, d, n = w_ada.shape
    rows = cond.shape[0]
    nb = 1536
    return pl.pallas_call(
        _ada_kernel,
        grid=(depth, n // nb),
        in_specs=[
            pl.BlockSpec((rows, d), lambda l, j: (0, 0)),
            pl.BlockSpec((None, d, nb), lambda l, j: (l, 0, j)),
            pl.BlockSpec((None, 1, nb), lambda l, j: (l, 0, j)),
        ],
        out_specs=pl.BlockSpec((None, rows, nb), lambda l, j: (l, 0, j)),
        out_shape=jax.ShapeDtypeStruct((depth, rows, n), F32),
        compiler_params=_params(("arbitrary", "arbitrary")),
        name="ada",
    )(cond, w_ada, b_ada.reshape(depth, 1, n))


def _prenorm_kernel(xc_ref, xl_ref, mod_ref, n_ref, h_ref, *, n_ctx_tiles):
    x = jnp.where(pl.program_id(0) < n_ctx_tiles, xc_ref[...], xl_ref[...])
    m = mod_ref[...]
    h = _rms(x) * n_ref[...] * (1.0 + m[1:2]) + m[0:1]
    h_ref[...] = h.astype(BF16)


def _gla_in_kernel(h_ref, wqk_ref, wv_ref, wog_ref, wgk_ref, wgk2_ref, bgk_ref,
                   og_ref, qin_ref, oin_ref, u_ref, g_ref,
                   q_s, k_s, v_s, la_s):
    h = h_ref[...]
    qk = _dot(h, wqk_ref[...])
    q_s[...] = qk[:, :FB_W] * (DK ** -0.5)
    k_s[...] = qk[:, FB_W:]
    v_s[...] = _dot(h, wv_ref[...]).astype(BF16)
    og_ref[...] = _dot(h, wog_ref[...]).astype(BF16)
    gk = _dot(h, wgk_ref[...]).astype(BF16)
    lp = _dot(gk, wgk2_ref[...]) + bgk_ref[...]
    la_s[...] = (jnp.minimum(lp, 0.0) - jnp.log(1.0 + jnp.exp(-jnp.abs(lp)))) * (1.0 / GATE_NORMALIZER)

    row = lax.broadcasted_iota(jnp.int32, (SUB, SUB), 0)
    col = lax.broadcasted_iota(jnp.int32, (SUB, SUB), 1)
    same = (row & -CHUNK) == (col & -CHUNK)
    lower = same & (col <= row)
    upper = same & (col >= row)
    tri = jnp.where(lower, 1.0, 0.0).astype(BF16)
    is_f = (lax.broadcasted_iota(jnp.int32, (SUB, FB_W), 1) & (2 * DK - 1)) < DK
    rchunk = lax.broadcasted_iota(jnp.int32, (SUB, 2 * DK), 0) & -CHUNK

    def sub_tile(s, carry):
        r0 = pl.multiple_of(s * SUB, SUB)
        rows = pl.ds(r0, SUB)
        la = la_s[rows, :]
        hi = la.astype(BF16)
        r1 = la - hi.astype(F32)
        mid = r1.astype(BF16)
        lo = (r1 - mid.astype(F32)).astype(BF16)
        pre = _dot(tri, hi) + _dot(tri, mid) + _dot(tri, lo)
        tot_rows = [pre[c * CHUNK + CHUNK - 1:c * CHUNK + CHUNK, :] for c in range(CPS)]
        tot = jnp.concatenate([jnp.broadcast_to(t, (CHUNK, FB_W)) for t in tot_rows], axis=0)
        b = jnp.where(is_f, pre, tot - pre + la)
        q = q_s[rows, :]
        k = k_s[rows, :]
        qin = (q * jnp.exp(b)).astype(BF16)
        kin = (k * jnp.exp(-b)).astype(BF16)
        kout = (k * jnp.exp(tot - b)).astype(BF16)
        qin_ref[rows, :] = qin
        zero = jnp.zeros_like(qin)
        qf = jnp.where(is_f, qin, zero)
        qb = jnp.where(is_f, zero, qin)
        v = v_s[rows, :]
        for hh in range(N_HEADS):
            fb = slice(hh * 2 * DK, (hh + 1) * 2 * DK)
            vs = slice(hh * DV, (hh + 1) * DV)
            af = _dot_nt(qf[:, fb], kin[:, fb])
            ab = _dot_nt(qb[:, fb], kin[:, fb])
            att = (jnp.where(lower, af, 0.0) + jnp.where(upper, ab, 0.0)).astype(BF16)
            oin_ref[rows, vs] = _dot(att, v[:, vs])
            ko = kout[:, fb]
            kbd = jnp.concatenate([jnp.where(rchunk == c * CHUNK, ko, jnp.zeros_like(ko)) for c in range(CPS)],
                                  axis=1)
            u_ref[s, hh] = _dot_tn(v[:, vs], kbd)
            g_ref[s, hh] = jnp.concatenate([jnp.exp(t[:, fb]) for t in tot_rows], axis=1)
        return carry

    lax.fori_loop(0, h_ref.shape[0] // SUB, sub_tile, 0)


def _mix_in_kernel(h_ref, wc_ref, wf_ref, wm_ref, cw_ref, wb_ref,
                   fx_ref, g0_ref, g2_ref, yp_ref, *, n_ctx_tiles, period_ctx, period_lat):
    i = pl.program_id(0)
    h = h_ref[...]
    s = _dot(h, wc_ref[...])
    sb, sc, sx = s[:, :SC_W], s[:, SC_W:2 * SC_W], s[:, 2 * SC_W:]
    u = sc * sx
    tm = h_ref.shape[0]
    period = jnp.where(i < n_ctx_tiles, period_ctx, period_lat)
    pos = lax.broadcasted_iota(jnp.int32, (tm, 1), 0) & (period - 1)
    up = jnp.where(pos == 0, 0.0, pltpu.roll(u, 1, axis=0))
    un = jnp.where(pos == period - 1, 0.0, pltpu.roll(u, tm - 1, axis=0))
    cw = cw_ref[...]
    conv = cw[0:1] * up + cw[1:2] * u + cw[2:3] * un
    yb = _dot((sb * conv).astype(BF16), wb_ref[...])

    _put_rows(fx_ref, 0, _dot(h, wf_ref[...]))

    d = g0_ref.shape[-1]
    g0_ref[...] = _sigmoid(_dot(h, wm_ref[:, 0:d])).astype(BF16)
    yp_ref[...] = (_sigmoid(_dot(h, wm_ref[:, d:2 * d])) * yb).astype(BF16)
    g2_ref[...] = _sigmoid(_dot(h, wm_ref[:, 2 * d:3 * d])).astype(BF16)


def _scan_kernel(u_ref, g_ref, s0_ref, ss_ref, sfin_ref, *, n_sub):
    row_f = lax.broadcasted_iota(jnp.int32, (2 * DK, DV), 0) < DK
    lane_f = lax.broadcasted_iota(jnp.int32, (DV, 2 * DK), 1) < DK
    s0 = s0_ref[...]

    def fwd(j, st):
        for c in range(CPS):
            cs = slice(c * 2 * DK, (c + 1) * 2 * DK)
            ss_ref[j, :, c * DV:(c + 1) * DV] = st.T.astype(BF16)
            st = g_ref[j, :, cs] * st + u_ref[j, :, cs]
        return st

    sf = lax.fori_loop(0, n_sub, fwd, s0)

    def bwd(jj, st):
        j = n_sub - 1 - jj
        for c in range(CPS - 1, -1, -1):
            cs = slice(c * 2 * DK, (c + 1) * 2 * DK)
            vs = slice(c * DV, (c + 1) * DV)
            ss_ref[j, :, vs] = jnp.where(row_f, ss_ref[j, :, vs], st.T.astype(BF16))
            st = g_ref[j, :, cs] * st + u_ref[j, :, cs]
        return st

    sb = lax.fori_loop(0, n_sub, bwd, s0)
    sfin_ref[...] = jnp.where(lane_f, sf, sb)


def _scan(u, g, s0, n_seq, n_sub, sub_block0):
    return pl.pallas_call(
        functools.partial(_scan_kernel, n_sub=n_sub),
        grid=(n_seq, N_HEADS),
        in_specs=[
            pl.BlockSpec((n_sub, None, DV, CPS * 2 * DK), lambda b, hh: (sub_block0 + b, hh, 0, 0)),
            pl.BlockSpec((n_sub, None, 1, CPS * 2 * DK), lambda b, hh: (sub_block0 + b, hh, 0, 0)),
            pl.BlockSpec((None, None, DV, 2 * DK), lambda b, hh: (b, hh, 0, 0)),
        ],
        out_specs=[
            pl.BlockSpec((n_sub, None, 2 * DK, CPS * DV), lambda b, hh: (b, hh, 0, 0)),
            pl.BlockSpec((None, None, DV, 2 * DK), lambda b, hh: (b, hh, 0, 0)),
        ],
        out_shape=[
            jax.ShapeDtypeStruct((n_seq * n_sub, N_HEADS, 2 * DK, CPS * DV), BF16),
            jax.ShapeDtypeStruct((n_seq, N_HEADS, DV, 2 * DK), F32),
        ],
        compiler_params=_params(("arbitrary", "arbitrary")),
        name="scan",
    )(u, g, s0)


def _gather_rows(ref, start, size, stride):
    return jnp.concatenate([ref[g, pl.ds(start, size, stride=stride), :] for g in range(FN_GROUPS)], axis=1)


def _put_rows(ref, r0, val):
    for g in range(FN_GROUPS):
        ref[g, r0:r0 + val.shape[0], :] = val[:, g * FN_GW:(g + 1) * FN_GW]


def _dft_direct_kernel(x_ref, m_ref, o_ref):
    seq = x_ref.shape[1]
    x = jnp.concatenate([x_ref[g] for g in range(FN_GROUPS)], axis=1)
    res = _dot(m_ref[...], x.astype(BF16))
    o_ref[0] = res[:seq].astype(BF16)
    o_ref[1] = res[seq:].astype(BF16)


def _pitch(n):
    return n + 4


def _ct1_kernel(x_ref, m_ref, tc_ref, ts_ref, y_ref, xs_s, tr_s, ti_s, *, l1, l2):
    j = pl.program_id(1)
    px, pt = _pitch(l2), _pitch(l1)

    @pl.when(j == 0)
    def _():
        for n1 in range(l1):
            for g in range(FN_GROUPS):
                xs_s[g, n1 * px:n1 * px + l2, :] = x_ref[g, n1 * l2:(n1 + 1) * l2, :]

    m = m_ref[...]
    for jn in range(CT_NB):
        xs = _gather_rows(xs_s, j * CT_NB + jn, l1, px)
        res = _dot(m, xs.astype(BF16))
        yr, yi = res[:l1], res[l1:]
        c = jnp.concatenate([tc_ref[jn]] * FN_GROUPS, axis=1)
        s = jnp.concatenate([ts_ref[jn]] * FN_GROUPS, axis=1)
        _put_rows(tr_s, jn * pt, yr * c + yi * s)
        _put_rows(ti_s, jn * pt, yi * c - yr * s)
    for k1 in range(l1):
        y_ref[0, k1] = _gather_rows(tr_s, k1, CT_NB, pt).astype(BF16)
        y_ref[1, k1] = _gather_rows(ti_s, k1, CT_NB, pt).astype(BF16)


def _ct2_kernel(y_ref, m_ref, o_ref, tr_s, ti_s, *, l2):
    m = m_ref[...]
    pt = _pitch(l2)
    for kk in range(CT_KB):
        rhs = jnp.concatenate([y_ref[0, kk], y_ref[1, kk]], axis=0)
        res = _dot(m, rhs)
        _put_rows(tr_s, kk * pt, res[:l2])
        _put_rows(ti_s, kk * pt, res[l2:])
    for k2 in range(l2):
        o_ref[0, k2] = _gather_rows(tr_s, k2, CT_KB, pt).astype(BF16)
        o_ref[1, k2] = _gather_rows(ti_s, k2, CT_KB, pt).astype(BF16)


def _cos_sin(n_out, n_in, period, scale):
    idx = (np.arange(n_out)[:, None] * np.arange(n_in)[None, :]) % period
    ang = 2.0 * np.pi * idx / period
    return np.cos(ang) * scale, np.sin(ang) * scale


def _mix_out_kernel(*refs, n_ctx_tiles, split_x):
    if split_x:
        xc_ref, xl_ref, *refs = refs
    else:
        x_ref, *refs = refs
    (oin_ref, qin_ref, ssc_ref, ssl_ref, og_ref, frc_ref, frl_ref, g0_ref, g2_ref, yp_ref, mod_ref,
     gn_ref, n2_ref, cs_ref, wa_ref, wc_ref, wo_ref, x1_ref, h2_ref, o_s) = refs
    is_ctx = pl.program_id(0) < n_ctx_tiles
    x = jnp.where(is_ctx, xc_ref[...], xl_ref[...]) if split_x else x_ref[...]
    for s in range(oin_ref.shape[0] // SUB):
        for hh in range(N_HEADS):
            st = jnp.where(is_ctx, ssc_ref[s, hh], ssl_ref[s, hh])
            fb = slice(hh * 2 * DK, (hh + 1) * 2 * DK)
            vs = slice(hh * DV, (hh + 1) * DV)
            for c in range(CPS):
                rows = slice(s * SUB + c * CHUNK, s * SUB + (c + 1) * CHUNK)
                o_s[rows, vs] = oin_ref[rows, vs] + _dot(qin_ref[rows, fb], st[:, c * DV:(c + 1) * DV])
    gn = gn_ref[...]
    parts = []
    for hh in range(N_HEADS):
        vs = slice(hh * DV, (hh + 1) * DV)
        parts.append((_rms(o_s[:, vs]) * gn * _silu(og_ref[:, vs].astype(F32))).astype(BF16))
    ya = _dot(jnp.concatenate(parts, axis=1), wa_ref[...])
    spec = jnp.where(is_ctx, frc_ref[...], frl_ref[...])
    parts = []
    for g in range(FN_GROUPS):
        gs = slice(g * FN_GW, (g + 1) * FN_GW)
        ri = jnp.concatenate([spec[0][:, gs], spec[1][:, gs]], axis=1)
        parts.append(_dot(ri, cs_ref[...]).astype(BF16))
    yc = _dot(jnp.concatenate(parts, axis=1), wc_ref[...])
    y = g0_ref[...].astype(F32) * ya + yp_ref[...].astype(F32) + g2_ref[...].astype(F32) * yc
    mix = _dot(y.astype(BF16), wo_ref[...])
    m = mod_ref[...]
    x1 = x + m[2:3] * mix
    x1_ref[...] = x1
    h2_ref[...] = (_rms(x1) * n2_ref[...] * (1.0 + m[4:5]) + m[3:4]).astype(BF16)


def _ffn_kernel(h2_ref, x1_ref, wup_ref, wd_ref, mod_ref, modn_ref, nn_ref, *outs, final, n_ctx_tiles):
    h2 = h2_ref[...]
    d_ff = wd_ref.shape[0]
    y = None
    for c0 in range(0, d_ff, FF_CHUNK):
        c1 = min(c0 + FF_CHUNK, d_ff)
        gate = _dot(h2, wup_ref[:, c0:c1])
        up = _dot(h2, wup_ref[:, d_ff + c0:d_ff + c1])
        part = _dot((_silu(gate) * up).astype(BF16), wd_ref[c0:c1, :])
        y = part if y is None else y + part
    x2 = x1_ref[...] + mod_ref[5:6, :] * y
    if final:
        yc_ref, yl_ref = outs
        out = _rms(x2) * nn_ref[...]
        is_ctx = pl.program_id(0) < n_ctx_tiles

        @pl.when(is_ctx)
        def _():
            yc_ref[...] = out

        @pl.when(jnp.logical_not(is_ctx))
        def _():
            yl_ref[...] = out
    else:
        x2_ref, hn_ref = outs
        x2_ref[...] = x2
        hn_ref[...] = (_rms(x2) * nn_ref[...] * (1.0 + modn_ref[1:2, :]) + modn_ref[0:1, :]).astype(BF16)


def kernel(x_prompt, x_sample, state_gla, c, c_ctx, w_ada, b_ada, norm1, norm2, w_in, w_gk_f, b_gk_f,
           w_gk_b, b_gk_b, gla_norm, w_a_out, conv_w, w_b_out, w_c_out, w_o, w_up, w_down, norm_f):
    b_ctx, seq, d = x_prompt.shape
    b_lat, dec_seq, _ = x_sample.shape
    depth = w_ada.shape[0]
    d_ff = w_down.shape[1]
    nc, nl = b_ctx * seq, b_lat * dec_seq
    nt = nc + nl
    l1, l2 = dec_seq // GRID_W, GRID_W
    assert seq == SUB and d_ff % 128 == 0
    assert nc % dec_seq == 0 and l2 % CT_NB == 0 and l1 % CT_KB == 0
    n_sub_tot, n_sub_lat = nt // SUB, dec_seq // SUB
    assert (nc // SUB) % n_sub_lat == 0
    n_cond = -(-(1 + b_lat) // 8) * 8

    class Tiling:
        def __init__(self, tm):
            assert nc % tm == 0 and dec_seq % tm == 0
            self.tm, self.n, self.n_ctx, self.per_lat = tm, nt // tm, nc // tm, dec_seq // tm

        def cond(self, i):
            return jnp.where(i < self.n_ctx, 0, 1 + (i - self.n_ctx) // self.per_lat)

        def ctx_blk(self, i):
            return jnp.minimum(i, self.n_ctx - 1)

        def lat_blk(self, i):
            return jnp.maximum(i - self.n_ctx, 0)

        def tok(self, w):
            return pl.BlockSpec((self.tm, w), lambda i, *_: (i, 0))

        def tok_ctx(self, w):
            return pl.BlockSpec((self.tm, w), lambda i, *_: (self.ctx_blk(i), 0))

        def tok_lat(self, w):
            return pl.BlockSpec((self.tm, w), lambda i, *_: (self.lat_blk(i), 0))

        def mod(self):
            return pl.BlockSpec((None, N_MOD, d), lambda i, *_: (self.cond(i), 0, 0))

    t_gla, t_mix, t_out, t_ffn = Tiling(TM_GLA), Tiling(TM_MIX), Tiling(TM_OUT), Tiling(TM_FFN)

    cond = jnp.concatenate([c_ctx[None, :], c, jnp.zeros((n_cond - 1 - b_lat, d), F32)], axis=0)
    mods = _ada(cond, w_ada, b_ada).reshape(depth, n_cond, N_MOD, d)

    cc, sc_ = _cos_sin(FN_GW, FN_GW, FN_GW, FN_GW ** -0.5)
    cs_tab = jnp.asarray(np.concatenate([cc, sc_], axis=0), F32).astype(BF16)
    cl, sl = _cos_sin(seq, seq, seq, seq ** -0.5)
    m_ctx_tab = jnp.asarray(np.concatenate([cl, -sl], axis=0), F32).astype(BF16)
    c1, s1 = _cos_sin(l1, l1, l1, l1 ** -0.5)
    m1_tab = jnp.asarray(np.concatenate([c1, -s1], axis=0), F32).astype(BF16)
    tcn, tsn = _cos_sin(l2, l1, l1 * l2, 1.0)
    tc_tab = jnp.asarray(np.repeat(tcn[:, :, None], 128, axis=2), F32)
    ts_tab = jnp.asarray(np.repeat(tsn[:, :, None], 128, axis=2), F32)
    c2, s2 = _cos_sin(l2, l2, l2, l2 ** -0.5)
    m2_tab = jnp.asarray(np.block([[c2, s2], [-s2, c2]]), F32).astype(BF16)

    xc, xl = x_prompt.reshape(nc, d), x_sample.reshape(nl, d)
    x = None

    h = pl.pallas_call(
        functools.partial(_prenorm_kernel, n_ctx_tiles=t_ffn.n_ctx),
        grid=(t_ffn.n,),
        in_specs=[t_ffn.tok_ctx(d), t_ffn.tok_lat(d), t_ffn.mod(), _const_spec((1, d))],
        out_specs=t_ffn.tok(d),
        out_shape=jax.ShapeDtypeStruct((nt, d), BF16),
        compiler_params=_params(("arbitrary",)),
        name="prenorm",
    )(xc, xl, mods[0], norm1[0][None, :])

    offs = np.cumsum([0, 256, 256, 512, 512, LOWRANK, LOWRANK, SC_W, SC_W, SC_W, FN_W, 3 * d])
    new_states = []
    y_out = None
    for l in range(depth):
        w = w_in[l]
        seg = lambda a: w[:, offs[a]:offs[a + 1]]

        def dup_heads(m):
            m4 = m.reshape(d, N_HEADS, 1, DK)
            return jnp.broadcast_to(m4, (d, N_HEADS, 2, DK)).reshape(d, FB_W)

        wqk = jnp.concatenate([dup_heads(seg(0)), dup_heads(seg(1))], axis=1).astype(BF16)
        wv = seg(2).astype(BF16)
        wog = seg(3).astype(BF16)
        wgk = jnp.concatenate([seg(4), seg(5), jnp.zeros((d, GK_PAD - 2 * LOWRANK), F32)], axis=1).astype(BF16)
        zf = jnp.zeros((LOWRANK, N_HEADS, DK), F32)
        top = jnp.stack([w_gk_f[l].reshape(LOWRANK, N_HEADS, DK), zf], axis=2).reshape(LOWRANK, FB_W)
        bot = jnp.stack([zf, w_gk_b[l].reshape(LOWRANK, N_HEADS, DK)], axis=2).reshape(LOWRANK, FB_W)
        wgk2 = jnp.concatenate([top, bot, jnp.zeros((GK_PAD - 2 * LOWRANK, FB_W), F32)], axis=0).astype(BF16)
        bgk = jnp.stack([b_gk_f[l].reshape(N_HEADS, DK), b_gk_b[l].reshape(N_HEADS, DK)], axis=1).reshape(1, FB_W)

        tok = t_gla.tok
        og, qin, oin, u, g = pl.pallas_call(
            _gla_in_kernel,
            grid=(t_gla.n,),
            in_specs=[tok(d), _const_spec((d, 2 * FB_W)), _const_spec((d, V_W)), _const_spec((d, V_W)),
                      _const_spec((d, GK_PAD)), _const_spec((GK_PAD, FB_W)), _const_spec((1, FB_W))],
            out_specs=[tok(V_W), tok(FB_W), tok(V_W),
                       pl.BlockSpec((TM_GLA // SUB, N_HEADS, DV, CPS * 2 * DK), lambda i: (i, 0, 0, 0)),
                       pl.BlockSpec((TM_GLA // SUB, N_HEADS, 1, CPS * 2 * DK), lambda i: (i, 0, 0, 0))],
            out_shape=[jax.ShapeDtypeStruct((nt, V_W), BF16), jax.ShapeDtypeStruct((nt, FB_W), BF16),
                       jax.ShapeDtypeStruct((nt, V_W), F32),
                       jax.ShapeDtypeStruct((n_sub_tot, N_HEADS, DV, CPS * 2 * DK), F32),
                       jax.ShapeDtypeStruct((n_sub_tot, N_HEADS, 1, CPS * 2 * DK), F32)],
            scratch_shapes=[pltpu.VMEM((TM_GLA, FB_W), F32), pltpu.VMEM((TM_GLA, FB_W), F32),
                            pltpu.VMEM((TM_GLA, V_W), BF16), pltpu.VMEM((TM_GLA, FB_W), F32)],
            compiler_params=_params(("arbitrary",)),
            name="gla_in",
        )(h, wqk, wv, wog, wgk, wgk2, bgk)

        wc = w[:, offs[6]:offs[9]].astype(BF16)
        wf = seg(9).astype(BF16)
        wm = seg(10).astype(BF16)
        tok = t_mix.tok
        fx, g0, g2, yp = pl.pallas_call(
            functools.partial(_mix_in_kernel, n_ctx_tiles=t_mix.n_ctx, period_ctx=seq, period_lat=GRID_W),
            grid=(t_mix.n,),
            in_specs=[tok(d), _const_spec((d, 3 * SC_W)), _const_spec((d, FN_W)), _const_spec((d, 3 * d)),
                      _const_spec((3, SC_W)), _const_spec((SC_W, d))],
            out_specs=[pl.BlockSpec((FN_GROUPS, TM_MIX, FN_GW), lambda i: (0, i, 0)), tok(d), tok(d), tok(d)],
            out_shape=[jax.ShapeDtypeStruct((FN_GROUPS, nt, FN_GW), F32)] + [jax.ShapeDtypeStruct((nt, d), BF16)] * 3,
            compiler_params=_params(("arbitrary",)),
            name="mix_in",
        )(h, wc, wf, wm, conv_w[l], w_b_out[l].astype(BF16))

        s0_ctx = jnp.zeros((b_ctx, N_HEADS, DV, 2 * DK), F32)
        s0_lat = state_gla[:, l].transpose(0, 2, 4, 1, 3).reshape(b_lat, N_HEADS, DV, 2 * DK)
        ss_ctx, sfin = _scan(u, g, s0_ctx, b_ctx, seq // SUB, 0)
        ss_lat, _ = _scan(u, g, s0_lat, b_lat, n_sub_lat, (nc // SUB) // n_sub_lat)
        new_states.append(sfin.reshape(b_ctx, N_HEADS, DV, 2, DK).transpose(0, 3, 1, 4, 2))

        sp_ctx = pl.pallas_call(
            _dft_direct_kernel,
            grid=(b_ctx,),
            in_specs=[pl.BlockSpec((FN_GROUPS, seq, FN_GW), lambda s: (0, s, 0)), _const_spec((2 * seq, seq))],
            out_specs=pl.BlockSpec((2, seq, FN_W), lambda s: (0, s, 0)),
            out_shape=jax.ShapeDtypeStruct((2, nc, FN_W), BF16),
            compiler_params=_params(("arbitrary",)),
            name="dft_ctx",
        )(fx, m_ctx_tab)
        lat_blk0 = nc // dec_seq
        yct = pl.pallas_call(
            functools.partial(_ct1_kernel, l1=l1, l2=l2),
            grid=(b_lat, l2 // CT_NB),
            in_specs=[pl.BlockSpec((FN_GROUPS, dec_seq, FN_GW), lambda b, j: (0, lat_blk0 + b, 0)),
                      _const_spec((2 * l1, l1)),
                      pl.BlockSpec((CT_NB, l1, 128), lambda b, j: (j, 0, 0)),
                      pl.BlockSpec((CT_NB, l1, 128), lambda b, j: (j, 0, 0))],
            out_specs=pl.BlockSpec((None, 2, l1, CT_NB, FN_W), lambda b, j: (b, 0, 0, j, 0)),
            out_shape=jax.ShapeDtypeStruct((b_lat, 2, l1, l2, FN_W), BF16),
            scratch_shapes=[pltpu.VMEM((FN_GROUPS, l1 * _pitch(l2), FN_GW), F32)]
            + [pltpu.VMEM((FN_GROUPS, CT_NB * _pitch(l1), FN_GW), F32)] * 2,
            compiler_params=_params(("arbitrary", "arbitrary")),
            name="dft_lat1",
        )(fx, m1_tab, tc_tab, ts_tab)
        sp_lat = pl.pallas_call(
            functools.partial(_ct2_kernel, l2=l2),
            grid=(b_lat, l1 // CT_KB),
            in_specs=[pl.BlockSpec((None, 2, CT_KB, l2, FN_W), lambda b, j: (b, 0, j, 0, 0)),
                      _const_spec((2 * l2, 2 * l2))],
            out_specs=pl.BlockSpec((2, None, l2, CT_KB, FN_W), lambda b, j: (0, b, 0, j, 0)),
            out_shape=jax.ShapeDtypeStruct((2, b_lat, l2, l1, FN_W), BF16),
            scratch_shapes=[pltpu.VMEM((FN_GROUPS, CT_KB * _pitch(l2), FN_GW), F32)] * 2,
            compiler_params=_params(("arbitrary", "arbitrary")),
            name="dft_lat2",
        )(yct, m2_tab).reshape(2, nl, FN_W)

        tok = t_out.tok
        split_x = x is None
        ss_spec = lambda blk: pl.BlockSpec((TM_OUT // SUB, N_HEADS, 2 * DK, CPS * DV),
                                           lambda i: (blk(i), 0, 0, 0))
        sp_spec = lambda blk: pl.BlockSpec((2, TM_OUT, FN_W), lambda i: (0, blk(i), 0))
        x1, h2 = pl.pallas_call(
            functools.partial(_mix_out_kernel, n_ctx_tiles=t_out.n_ctx, split_x=split_x),
            grid=(t_out.n,),
            in_specs=([t_out.tok_ctx(d), t_out.tok_lat(d)] if split_x else [tok(d)]) + [
                tok(V_W), tok(FB_W), ss_spec(t_out.ctx_blk), ss_spec(t_out.lat_blk), tok(V_W),
                sp_spec(t_out.ctx_blk), sp_spec(t_out.lat_blk), tok(d), tok(d), tok(d), t_out.mod(),
                _const_spec((1, DV)), _const_spec((1, d)), _const_spec((2 * FN_GW, FN_GW)),
                _const_spec((V_W, d)), _const_spec((FN_W, d)), _const_spec((d, d))],
            out_specs=[tok(d), tok(d)],
            out_shape=[jax.ShapeDtypeStruct((nt, d), F32), jax.ShapeDtypeStruct((nt, d), BF16)],
            scratch_shapes=[pltpu.VMEM((TM_OUT, V_W), F32)],
            compiler_params=_params(("arbitrary",)),
            name="mix_out",
        )(*((xc, xl) if split_x else (x,)), oin, qin, ss_ctx, ss_lat, og, sp_ctx, sp_lat, g0, g2, yp, mods[l],
          gla_norm[l][None, :], norm2[l][None, :], cs_tab,
          w_a_out[l].astype(BF16), w_c_out[l].astype(BF16), w_o[l].astype(BF16))

        final = l == depth - 1
        tok = t_ffn.tok
        nxt = depth - 1 if final else l + 1
        outs = pl.pallas_call(
            functools.partial(_ffn_kernel, final=final, n_ctx_tiles=t_ffn.n_ctx),
            grid=(t_ffn.n,),
            in_specs=[tok(d), tok(d), _const_spec((d, 2 * d_ff)), _const_spec((d_ff, d)),
                      t_ffn.mod(), t_ffn.mod(), _const_spec((1, d))],
            out_specs=[t_ffn.tok_ctx(d), t_ffn.tok_lat(d)] if final else [tok(d), tok(d)],
            out_shape=([jax.ShapeDtypeStruct((nc, d), F32), jax.ShapeDtypeStruct((nl, d), F32)] if final else
                       [jax.ShapeDtypeStruct((nt, d), F32), jax.ShapeDtypeStruct((nt, d), BF16)]),
            compiler_params=_params(("arbitrary",)),
            name="ffn",
        )(h2, x1, w_up[l].astype(BF16), w_down[l].astype(BF16), mods[l], mods[nxt],
          (norm_f if final else norm1[nxt])[None, :])
        if final:
            y_out = outs
        else:
            x, h = outs

    y_prompt = y_out[0].reshape(b_ctx, seq, d)
    y_sample = y_out[1].reshape(b_lat, dec_seq, d)
    new_state_gla = jnp.stack(new_states, axis=1).astype(x_prompt.dtype)
    return (y_prompt, y_sample, new_state_gla)
```

```python
import functools
import math

import numpy as np
import jax
import jax.numpy as jnp
from jax import lax
from jax.experimental import pallas as pl
from jax.experimental.pallas import tpu as pltpu

F32 = jnp.float32
BF16 = jnp.bfloat16

GRID_W = 64
N_HEADS = 4
DK = 64
DV = 128
V_W = N_HEADS * DV
FB_W = N_HEADS * 2 * DK
LOWRANK = 16
GATE_NORMALIZER = 16.0
CHUNK = 64
SUB = 256
CPS = SUB // CHUNK
SC_W = 512
FN_GROUPS = 4
FN_GW = 128
FN_W = FN_GROUPS * FN_GW
N_MOD = 6
EPS = 1e-6
GK_PAD = 128

TM_GLA = 1024
TM_MIX = 512
TM_OUT = 512
TM_FFN = 512
FF_CHUNK = 512
CT_NB = 16
CT_KB = 16
VMEM_LIMIT = 56 * 1024 * 1024


def _dot(a, b):
    return jnp.dot(a, b, preferred_element_type=F32)


def _dot_nt(a, b):
    return lax.dot_general(a, b, (((1,), (1,)), ((), ())), preferred_element_type=F32)


def _dot_tn(a, b):
    return lax.dot_general(a, b, (((0,), (0,)), ((), ())), preferred_element_type=F32)


def _sigmoid(x):
    return 1.0 / (1.0 + jnp.exp(-x))


def _silu(x):
    return x * _sigmoid(x)


def _rms(x):
    return x * lax.rsqrt(jnp.mean(x * x, axis=-1, keepdims=True) + EPS)


def _params(sem):
    return pltpu.CompilerParams(dimension_semantics=sem, vmem_limit_bytes=VMEM_LIMIT)


def _const_spec(shape):
    nd = len(shape)
    return pl.BlockSpec(shape, lambda *_: (0,) * nd, pipeline_mode=pl.Buffered(1))


def _ada_kernel(c_ref, w_ref, b_ref, o_ref):
    s = _silu(c_ref[...]).astype(BF16)
    o_ref[...] = _dot(s, w_ref[...].astype(BF16)) + b_ref[...]


def _ada(cond, w_ada, b_ada):
    depth, d, n = w_ada.shape
    rows = cond.shape[0]
    nb = 1536
    return pl.pallas_call(
        _ada_kernel,
        grid=(depth, n // nb),
        in_specs=[
            pl.BlockSpec((rows, d), lambda l, j: (0, 0)),
            pl.BlockSpec((None, d, nb), lambda l, j: (l, 0, j)),
            pl.BlockSpec((None, 1, nb), lambda l, j: (l, 0, j)),
        ],
        out_specs=pl.BlockSpec((None, rows, nb), lambda l, j: (l, 0, j)),
        out_shape=jax.ShapeDtypeStruct((depth, rows, n), F32),
        compiler_params=_params(("arbitrary", "arbitrary")),
        name="ada",
    )(cond, w_ada, b_ada.reshape(depth, 1, n))


def _prenorm_kernel(xc_ref, xl_ref, mod_ref, n_ref, h_ref, *, n_ctx_tiles):
    x = jnp.where(pl.program_id(0) < n_ctx_tiles, xc_ref[...], xl_ref[...])
    m = mod_ref[...]
    h = _rms(x) * n_ref[...] * (1.0 + m[1:2]) + m[0:1]
    h_ref[...] = h.astype(BF16)


def _gla_in_kernel(h_ref, wqk_ref, wv_ref, wog_ref, wgk_ref, wgk2_ref, bgk_ref,
                   og_ref, qin_ref, oin_ref, u_ref, g_ref,
                   q_s, k_s, v_s, la_s):
    h = h_ref[...]
    qk = _dot(h, wqk_ref[...])
    lo_half = lax.broadcasted_iota(jnp.int32, (1, 2 * DK), 1) < DK

    def both_directions(x):
        tiles = []
        for p in range(N_HEADS // 2):
            t = x[:, p * 2 * DK:(p + 1) * 2 * DK]
            r = pltpu.roll(t, DK, axis=1)
            tiles += [jnp.where(lo_half, t, r), jnp.where(lo_half, r, t)]
        return jnp.concatenate(tiles, axis=1)

    q_s[...] = both_directions(qk[:, :N_HEADS * DK]) * (DK ** -0.5)
    k_s[...] = both_directions(qk[:, N_HEADS * DK:])
    v_s[...] = _dot(h, wv_ref[...]).astype(BF16)
    og_ref[...] = _dot(h, wog_ref[...]).astype(BF16)
    gk = _dot(h, wgk_ref[...]).astype(BF16)
    lp = _dot(gk, wgk2_ref[...]) + bgk_ref[...]
    la_s[...] = (jnp.minimum(lp, 0.0) - jnp.log(1.0 + jnp.exp(-jnp.abs(lp)))) * (1.0 / GATE_NORMALIZER)

    row = lax.broadcasted_iota(jnp.int32, (SUB, SUB), 0)
    col = lax.broadcasted_iota(jnp.int32, (SUB, SUB), 1)
    same = (row & -CHUNK) == (col & -CHUNK)
    lower = same & (col <= row)
    upper = same & (col >= row)
    tri = jnp.where(lower, 1.0, 0.0).astype(BF16)
    is_f = (lax.broadcasted_iota(jnp.int32, (SUB, FB_W), 1) & (2 * DK - 1)) < DK
    rchunk = lax.broadcasted_iota(jnp.int32, (SUB, 2 * DK), 0) & -CHUNK

    def sub_tile(s, carry):
        r0 = pl.multiple_of(s * SUB, SUB)
        rows = pl.ds(r0, SUB)
        la = la_s[rows, :]
        hi = la.astype(BF16)
        r1 = la - hi.astype(F32)
        mid = r1.astype(BF16)
        lo = (r1 - mid.astype(F32)).astype(BF16)
        pre = _dot(tri, hi) + _dot(tri, mid) + _dot(tri, lo)
        tot_rows = [pre[c * CHUNK + CHUNK - 1:c * CHUNK + CHUNK, :] for c in range(CPS)]
        tot = jnp.concatenate([jnp.broadcast_to(t, (CHUNK, FB_W)) for t in tot_rows], axis=0)
        b = jnp.where(is_f, pre, tot - pre + la)
        q = q_s[rows, :]
        k = k_s[rows, :]
        qin = (q * jnp.exp(b)).astype(BF16)
        kin = (k * jnp.exp(-b)).astype(BF16)
        kout = (k * jnp.exp(tot - b)).astype(BF16)
        qin_ref[rows, :] = qin
        zero = jnp.zeros_like(qin)
        qf = jnp.where(is_f, qin, zero)
        qb = jnp.where(is_f, zero, qin)
        v = v_s[rows, :]
        for hh in range(N_HEADS):
            fb = slice(hh * 2 * DK, (hh + 1) * 2 * DK)
            vs = slice(hh * DV, (hh + 1) * DV)
            a2 = _dot_nt(jnp.concatenate([qf[:, fb], qb[:, fb]], axis=0), kin[:, fb])
            att = (jnp.where(lower, a2[:SUB], 0.0) + jnp.where(upper, a2[SUB:], 0.0)).astype(BF16)
            oin_ref[rows, vs] = _dot(att, v[:, vs]).astype(BF16)
            ko = kout[:, fb]
            kbd = jnp.concatenate([jnp.where(rchunk == c * CHUNK, ko, jnp.zeros_like(ko)) for c in range(CPS)],
                                  axis=1)
            u_ref[s, hh] = _dot_tn(v[:, vs], kbd).astype(BF16)
            g_ref[s, hh] = jnp.concatenate([jnp.exp(t[:, fb]) for t in tot_rows], axis=1)
        return carry

    lax.fori_loop(0, h_ref.shape[0] // SUB, sub_tile, 0, unroll=2)


def _mix_in_kernel(h_ref, wc_ref, wf_ref, wm_ref, cw_ref, wb_ref,
                   fx_ref, g0_ref, g2_ref, yp_ref, *, n_ctx_tiles, period_ctx, period_lat):
    i = pl.program_id(0)
    h = h_ref[...]
    s = _dot(h, wc_ref[...])
    sb, sc, sx = s[:, :SC_W], s[:, SC_W:2 * SC_W], s[:, 2 * SC_W:]
    u = sc * sx
    tm = h_ref.shape[0]
    period = jnp.where(i < n_ctx_tiles, period_ctx, period_lat)
    pos = lax.broadcasted_iota(jnp.int32, (tm, 1), 0) & (period - 1)
    up = jnp.where(pos == 0, 0.0, pltpu.roll(u, 1, axis=0))
    un = jnp.where(pos == period - 1, 0.0, pltpu.roll(u, tm - 1, axis=0))
    cw = cw_ref[...]
    conv = cw[0:1] * up + cw[1:2] * u + cw[2:3] * un
    yb = _dot((sb * conv).astype(BF16), wb_ref[...])

    _put_rows(fx_ref, 0, _dot(h, wf_ref[...]))

    d = g0_ref.shape[-1]
    g0_ref[...] = _sigmoid(_dot(h, wm_ref[:, 0:d])).astype(BF16)
    yp_ref[...] = (_sigmoid(_dot(h, wm_ref[:, d:2 * d])) * yb).astype(BF16)
    g2_ref[...] = _sigmoid(_dot(h, wm_ref[:, 2 * d:3 * d])).astype(BF16)


def _scan_kernel(u_ref, g_ref, s0_ref, ss_ref, sfin_ref, *, n_sub):
    lane_f = lax.broadcasted_iota(jnp.int32, (DV, 2 * DK), 1) < DK
    for q in range(s0_ref.shape[0]):
        for hh in range(s0_ref.shape[1]):
            s0 = s0_ref[q, hh]

            def fwd(j, st):
                for c in range(CPS):
                    cs = slice(c * 2 * DK, (c + 1) * 2 * DK)
                    ss_ref[j, hh, :, cs] = st.astype(BF16)
                    st = g_ref[j, hh, :, cs] * st + u_ref[j, hh, :, cs].astype(F32)
                return st

            def bwd(j, st):
                for c in range(CPS - 1, -1, -1):
                    cs = slice(c * 2 * DK, (c + 1) * 2 * DK)
                    ss_ref[j, hh, :, cs] = jnp.where(lane_f, ss_ref[j, hh, :, cs], st.astype(BF16))
                    st = g_ref[j, hh, :, cs] * st + u_ref[j, hh, :, cs].astype(F32)
                return st

            lo = q * n_sub
            sf = lax.fori_loop(0, n_sub, lambda i, st: fwd(lo + i, st), s0)
            sb = lax.fori_loop(0, n_sub, lambda i, st: bwd(lo + n_sub - 1 - i, st), s0)
            sfin_ref[q, hh] = jnp.where(lane_f, sf, sb)


def _scan(u, g, s0, n_seq, n_sub, sub0, seq_blk, head_blk):
    rows = seq_blk * n_sub
    assert n_seq % seq_blk == 0 and N_HEADS % head_blk == 0 and sub0 % rows == 0
    wide = CPS * 2 * DK
    return pl.pallas_call(
        functools.partial(_scan_kernel, n_sub=n_sub),
        grid=(n_seq // seq_blk, N_HEADS // head_blk),
        in_specs=[
            pl.BlockSpec((rows, head_blk, DV, wide), lambda b, hh: (sub0 // rows + b, hh, 0, 0)),
            pl.BlockSpec((rows, head_blk, 1, wide), lambda b, hh: (sub0 // rows + b, hh, 0, 0)),
            pl.BlockSpec((seq_blk, head_blk, DV, 2 * DK), lambda b, hh: (b, hh, 0, 0)),
        ],
        out_specs=[
            pl.BlockSpec((rows, head_blk, DV, wide), lambda b, hh: (b, hh, 0, 0)),
            pl.BlockSpec((seq_blk, head_blk, DV, 2 * DK), lambda b, hh: (b, hh, 0, 0)),
        ],
        out_shape=[
            jax.ShapeDtypeStruct((n_seq * n_sub, N_HEADS, DV, wide), BF16),
            jax.ShapeDtypeStruct((n_seq, N_HEADS, DV, 2 * DK), F32),
        ],
        compiler_params=_params(("arbitrary", "arbitrary")),
        name="scan",
    )(u, g, s0)


def _gather_rows(ref, start, size, stride):
    return jnp.concatenate([ref[g, pl.ds(start, size, stride=stride), :] for g in range(FN_GROUPS)], axis=1)


def _put_rows(ref, r0, val):
    for g in range(FN_GROUPS):
        ref[g, r0:r0 + val.shape[0], :] = val[:, g * FN_GW:(g + 1) * FN_GW]


def _channel_dft_real(re_g, im_g, cs):
    return _dot(jnp.concatenate([re_g.astype(BF16), im_g.astype(BF16)], axis=1), cs)


def _dft_direct_kernel(x_ref, m_ref, cs_ref, o_ref):
    seq = x_ref.shape[1]
    x = jnp.concatenate([x_ref[g] for g in range(FN_GROUPS)], axis=1)
    res = _dot(m_ref[...], x.astype(BF16))
    for g in range(FN_GROUPS):
        gs = slice(g * FN_GW, (g + 1) * FN_GW)
        o_ref[:, gs] = _channel_dft_real(res[:seq, gs], res[seq:, gs], cs_ref[...]).astype(BF16)


def _pitch(n):
    return n + 4


def _ct1_kernel(x_ref, m_ref, tc_ref, ts_ref, y_ref, xs_s, tr_s, ti_s, *, l1, l2):
    j = pl.program_id(1)
    px, pt = _pitch(l2), _pitch(l1)

    @pl.when(j == 0)
    def _():
        for n1 in range(l1):
            for g in range(FN_GROUPS):
                xs_s[g, n1 * px:n1 * px + l2, :] = x_ref[g, n1 * l2:(n1 + 1) * l2, :]

    m = m_ref[...]
    for jn in range(CT_NB):
        xs = _gather_rows(xs_s, j * CT_NB + jn, l1, px)
        res = _dot(m, xs.astype(BF16))
        yr, yi = res[:l1], res[l1:]
        c = jnp.concatenate([tc_ref[jn]] * FN_GROUPS, axis=1)
        s = jnp.concatenate([ts_ref[jn]] * FN_GROUPS, axis=1)
        _put_rows(tr_s, jn * pt, yr * c + yi * s)
        _put_rows(ti_s, jn * pt, yi * c - yr * s)
    for k1 in range(l1):
        y_ref[0, k1] = _gather_rows(tr_s, k1, CT_NB, pt).astype(BF16)
        y_ref[1, k1] = _gather_rows(ti_s, k1, CT_NB, pt).astype(BF16)


def _ct2_kernel(y_ref, m_ref, cs_ref, o_ref, tr_s, ti_s, *, l2):
    m = m_ref[...]
    pt = _pitch(l2)

    @pl.when((pl.program_id(0) == 0) & (pl.program_id(1) == 0))
    def _():
        tr_s[...] = jnp.zeros_like(tr_s)
        ti_s[...] = jnp.zeros_like(ti_s)

    for kk in range(CT_KB):
        rhs = jnp.concatenate([y_ref[0, kk], y_ref[1, kk]], axis=0)
        res = _dot(m, rhs)
        _put_rows(tr_s, kk * pt, res[:l2])
        _put_rows(ti_s, kk * pt, res[l2:])
    for g in range(FN_GROUPS):
        tr_s[g] = _channel_dft_real(tr_s[g], ti_s[g], cs_ref[...])
    for k2 in range(l2):
        o_ref[k2] = _gather_rows(tr_s, k2, CT_KB, pt).astype(BF16)


def _cos_sin(n_out, n_in, period, scale):
    idx = (np.arange(n_out)[:, None] * np.arange(n_in)[None, :]) % period
    ang = 2.0 * np.pi * idx / period
    return np.cos(ang) * scale, np.sin(ang) * scale


def _mix_out_kernel(*refs, n_ctx_tiles, split_x):
    if split_x:
        xc_ref, xl_ref, *refs = refs
    else:
        x_ref, *refs = refs
    (oin_ref, qin_ref, ssc_ref, ssl_ref, og_ref, frc_ref, frl_ref, g0_ref, g2_ref, yp_ref, mod_ref,
     gn_ref, n2_ref, wa_ref, wc_ref, wo_ref, x1_ref, h2_ref, o_s) = refs
    is_ctx = pl.program_id(0) < n_ctx_tiles
    x = jnp.where(is_ctx, xc_ref[...], xl_ref[...]) if split_x else x_ref[...]
    for s in range(oin_ref.shape[0] // SUB):
        for hh in range(N_HEADS):
            st = jnp.where(is_ctx, ssc_ref[s, hh], ssl_ref[s, hh])
            fb = slice(hh * 2 * DK, (hh + 1) * 2 * DK)
            vs = slice(hh * DV, (hh + 1) * DV)
            for c in range(CPS):
                rows = slice(s * SUB + c * CHUNK, s * SUB + (c + 1) * CHUNK)
                o_s[rows, vs] = oin_ref[rows, vs].astype(F32) + _dot_nt(qin_ref[rows, fb], st[:, c * 2 * DK:(c + 1) * 2 * DK])
    gn = gn_ref[...]
    parts = []
    for hh in range(N_HEADS):
        vs = slice(hh * DV, (hh + 1) * DV)
        parts.append((_rms(o_s[:, vs]) * gn * _silu(og_ref[:, vs].astype(F32))).astype(BF16))
    ya = _dot(jnp.concatenate(parts, axis=1), wa_ref[...])
    yc = _dot(jnp.where(is_ctx, frc_ref[...], frl_ref[...]), wc_ref[...])
    y = g0_ref[...].astype(F32) * ya + yp_ref[...].astype(F32) + g2_ref[...].astype(F32) * yc
    mix = _dot(y.astype(BF16), wo_ref[...])
    m = mod_ref[...]
    x1 = x + m[2:3] * mix
    x1_ref[...] = x1
    h2_ref[...] = (_rms(x1) * n2_ref[...] * (1.0 + m[4:5]) + m[3:4]).astype(BF16)


def _ffn_kernel(h2_ref, x1_ref, wup_ref, wd_ref, mod_ref, modn_ref, nn_ref, *outs, final, n_ctx_tiles):
    h2 = h2_ref[...]
    d_ff = wd_ref.shape[0]
    y = None
    for c0 in range(0, d_ff, FF_CHUNK):
        c1 = min(c0 + FF_CHUNK, d_ff)
        gate = _dot(h2, wup_ref[:, c0:c1])
        up = _dot(h2, wup_ref[:, d_ff + c0:d_ff + c1])
        part = _dot((_silu(gate) * up).astype(BF16), wd_ref[c0:c1, :])
        y = part if y is None else y + part
    x2 = x1_ref[...] + mod_ref[5:6, :] * y
    if final:
        yc_ref, yl_ref = outs
        out = _rms(x2) * nn_ref[...]
        is_ctx = pl.program_id(0) < n_ctx_tiles

        @pl.when(is_ctx)
        def _():
            yc_ref[...] = out

        @pl.when(jnp.logical_not(is_ctx))
        def _():
            yl_ref[...] = out
    else:
        x2_ref, hn_ref = outs
        x2_ref[...] = x2
        hn_ref[...] = (_rms(x2) * nn_ref[...] * (1.0 + modn_ref[1:2, :]) + modn_ref[0:1, :]).astype(BF16)


def kernel(x_prompt, x_sample, state_gla, c, c_ctx, w_ada, b_ada, norm1, norm2, w_in, w_gk_f, b_gk_f,
           w_gk_b, b_gk_b, gla_norm, w_a_out, conv_w, w_b_out, w_c_out, w_o, w_up, w_down, norm_f):
    b_ctx, seq, d = x_prompt.shape
    b_lat, dec_seq, _ = x_sample.shape
    depth = w_ada.shape[0]
    d_ff = w_down.shape[1]
    nc, nl = b_ctx * seq, b_lat * dec_seq
    nt = nc + nl
    l1, l2 = dec_seq // GRID_W, GRID_W
    assert seq == SUB and d_ff % 128 == 0
    assert nc % dec_seq == 0 and l2 % CT_NB == 0 and l1 % CT_KB == 0
    n_sub_tot, n_sub_lat = nt // SUB, dec_seq // SUB
    assert (nc // SUB) % n_sub_lat == 0
    n_cond = -(-(1 + b_lat) // 8) * 8

    class Tiling:
        def __init__(self, tm):
            assert nc % tm == 0 and dec_seq % tm == 0
            self.tm, self.n, self.n_ctx, self.per_lat = tm, nt // tm, nc // tm, dec_seq // tm

        def cond(self, i):
            return jnp.where(i < self.n_ctx, 0, 1 + (i - self.n_ctx) // self.per_lat)

        def ctx_blk(self, i):
            return jnp.minimum(i, self.n_ctx - 1)

        def lat_blk(self, i):
            return jnp.maximum(i - self.n_ctx, 0)

        def tok(self, w):
            return pl.BlockSpec((self.tm, w), lambda i, *_: (i, 0))

        def tok_ctx(self, w):
            return pl.BlockSpec((self.tm, w), lambda i, *_: (self.ctx_blk(i), 0))

        def tok_lat(self, w):
            return pl.BlockSpec((self.tm, w), lambda i, *_: (self.lat_blk(i), 0))

        def mod(self):
            return pl.BlockSpec((None, N_MOD, d), lambda i, *_: (self.cond(i), 0, 0))

    t_gla, t_mix, t_out, t_ffn = Tiling(TM_GLA), Tiling(TM_MIX), Tiling(TM_OUT), Tiling(TM_FFN)

    cond = jnp.concatenate([c_ctx[None, :], c, jnp.zeros((n_cond - 1 - b_lat, d), F32)], axis=0)
    mods = _ada(cond, w_ada, b_ada).reshape(depth, n_cond, N_MOD, d)

    cc, sc_ = _cos_sin(FN_GW, FN_GW, FN_GW, FN_GW ** -0.5)
    cs_tab = jnp.asarray(np.concatenate([cc, sc_], axis=0), F32).astype(BF16)
    cl, sl = _cos_sin(seq, seq, seq, seq ** -0.5)
    m_ctx_tab = jnp.asarray(np.concatenate([cl, -sl], axis=0), F32).astype(BF16)
    c1, s1 = _cos_sin(l1, l1, l1, l1 ** -0.5)
    m1_tab = jnp.asarray(np.concatenate([c1, -s1], axis=0), F32).astype(BF16)
    tcn, tsn = _cos_sin(l2, l1, l1 * l2, 1.0)
    tc_tab = jnp.asarray(np.repeat(tcn[:, :, None], 128, axis=2), F32)
    ts_tab = jnp.asarray(np.repeat(tsn[:, :, None], 128, axis=2), F32)
    c2, s2 = _cos_sin(l2, l2, l2, l2 ** -0.5)
    m2_tab = jnp.asarray(np.block([[c2, s2], [-s2, c2]]), F32).astype(BF16)

    xc, xl = x_prompt.reshape(nc, d), x_sample.reshape(nl, d)
    x = None

    h = pl.pallas_call(
        functools.partial(_prenorm_kernel, n_ctx_tiles=t_ffn.n_ctx),
        grid=(t_ffn.n,),
        in_specs=[t_ffn.tok_ctx(d), t_ffn.tok_lat(d), t_ffn.mod(), _const_spec((1, d))],
        out_specs=t_ffn.tok(d),
        out_shape=jax.ShapeDtypeStruct((nt, d), BF16),
        compiler_params=_params(("arbitrary",)),
        name="prenorm",
    )(xc, xl, mods[0], norm1[0][None, :])

    offs = np.cumsum([0, 256, 256, 512, 512, LOWRANK, LOWRANK, SC_W, SC_W, SC_W, FN_W, 3 * d])
    new_states = []
    y_out = None
    for l in range(depth):
        w = w_in[l]
        seg = lambda a: w[:, offs[a]:offs[a + 1]]

        wqk = w[:, offs[0]:offs[2]].astype(BF16)
        wv = seg(2).astype(BF16)
        wog = seg(3).astype(BF16)
        wgk = jnp.concatenate([seg(4), seg(5), jnp.zeros((d, GK_PAD - 2 * LOWRANK), F32)], axis=1).astype(BF16)
        zf = jnp.zeros((LOWRANK, N_HEADS, DK), F32)
        top = jnp.stack([w_gk_f[l].reshape(LOWRANK, N_HEADS, DK), zf], axis=2).reshape(LOWRANK, FB_W)
        bot = jnp.stack([zf, w_gk_b[l].reshape(LOWRANK, N_HEADS, DK)], axis=2).reshape(LOWRANK, FB_W)
        wgk2 = jnp.concatenate([top, bot, jnp.zeros((GK_PAD - 2 * LOWRANK, FB_W), F32)], axis=0).astype(BF16)
        bgk = jnp.stack([b_gk_f[l].reshape(N_HEADS, DK), b_gk_b[l].reshape(N_HEADS, DK)], axis=1).reshape(1, FB_W)

        tok = t_gla.tok
        og, qin, oin, u, g = pl.pallas_call(
            _gla_in_kernel,
            grid=(t_gla.n,),
            in_specs=[tok(d), _const_spec((d, 2 * N_HEADS * DK)), _const_spec((d, V_W)), _const_spec((d, V_W)),
                      _const_spec((d, GK_PAD)), _const_spec((GK_PAD, FB_W)), _const_spec((1, FB_W))],
            out_specs=[tok(V_W), tok(FB_W), tok(V_W),
                       pl.BlockSpec((TM_GLA // SUB, N_HEADS, DV, CPS * 2 * DK), lambda i: (i, 0, 0, 0)),
                       pl.BlockSpec((TM_GLA // SUB, N_HEADS, 1, CPS * 2 * DK), lambda i: (i, 0, 0, 0))],
            out_shape=[jax.ShapeDtypeStruct((nt, V_W), BF16), jax.ShapeDtypeStruct((nt, FB_W), BF16),
                       jax.ShapeDtypeStruct((nt, V_W), BF16),
                       jax.ShapeDtypeStruct((n_sub_tot, N_HEADS, DV, CPS * 2 * DK), BF16),
                       jax.ShapeDtypeStruct((n_sub_tot, N_HEADS, 1, CPS * 2 * DK), F32)],
            scratch_shapes=[pltpu.VMEM((TM_GLA, FB_W), F32), pltpu.VMEM((TM_GLA, FB_W), F32),
                            pltpu.VMEM((TM_GLA, V_W), BF16), pltpu.VMEM((TM_GLA, FB_W), F32)],
            compiler_params=_params(("arbitrary",)),
            name="gla_in",
        )(h, wqk, wv, wog, wgk, wgk2, bgk)

        wc = w[:, offs[6]:offs[9]].astype(BF16)
        wf = seg(9).astype(BF16)
        wm = seg(10).astype(BF16)
        tok = t_mix.tok
        fx, g0, g2, yp = pl.pallas_call(
            functools.partial(_mix_in_kernel, n_ctx_tiles=t_mix.n_ctx, period_ctx=seq, period_lat=GRID_W),
            grid=(t_mix.n,),
            in_specs=[tok(d), _const_spec((d, 3 * SC_W)), _const_spec((d, FN_W)), _const_spec((d, 3 * d)),
                      _const_spec((3, SC_W)), _const_spec((SC_W, d))],
            out_specs=[pl.BlockSpec((FN_GROUPS, TM_MIX, FN_GW), lambda i: (0, i, 0)), tok(d), tok(d), tok(d)],
            out_shape=[jax.ShapeDtypeStruct((FN_GROUPS, nt, FN_GW), F32)] + [jax.ShapeDtypeStruct((nt, d), BF16)] * 3,
            compiler_params=_params(("arbitrary",)),
            name="mix_in",
        )(h, wc, wf, wm, conv_w[l], w_b_out[l].astype(BF16))

        s0_ctx = jnp.zeros((b_ctx, N_HEADS, DV, 2 * DK), F32)
        s0_lat = state_gla[:, l].transpose(0, 2, 4, 1, 3).reshape(b_lat, N_HEADS, DV, 2 * DK)
        ss_ctx, sfin = _scan(u, g, s0_ctx, b_ctx, seq // SUB, 0, math.gcd(b_ctx, 4), N_HEADS)
        ss_lat, _ = _scan(u, g, s0_lat, b_lat, n_sub_lat, nc // SUB, 1, 1)
        new_states.append(sfin.reshape(b_ctx, N_HEADS, DV, 2, DK).transpose(0, 3, 1, 4, 2))

        cs_spec = _const_spec((2 * FN_GW, FN_GW))
        fr_ctx = pl.pallas_call(
            _dft_direct_kernel,
            grid=(b_ctx,),
            in_specs=[pl.BlockSpec((FN_GROUPS, seq, FN_GW), lambda s: (0, s, 0)), _const_spec((2 * seq, seq)),
                      cs_spec],
            out_specs=pl.BlockSpec((seq, FN_W), lambda s: (s, 0)),
            out_shape=jax.ShapeDtypeStruct((nc, FN_W), BF16),
            compiler_params=_params(("arbitrary",)),
            name="dft_ctx",
        )(fx, m_ctx_tab, cs_tab)
        lat_blk0 = nc // dec_seq
        yct = pl.pallas_call(
            functools.partial(_ct1_kernel, l1=l1, l2=l2),
            grid=(b_lat, l2 // CT_NB),
            in_specs=[pl.BlockSpec((FN_GROUPS, dec_seq, FN_GW), lambda b, j: (0, lat_blk0 + b, 0)),
                      _const_spec((2 * l1, l1)),
                      pl.BlockSpec((CT_NB, l1, 128), lambda b, j: (j, 0, 0)),
                      pl.BlockSpec((CT_NB, l1, 128), lambda b, j: (j, 0, 0))],
            out_specs=pl.BlockSpec((None, 2, l1, CT_NB, FN_W), lambda b, j: (b, 0, 0, j, 0)),
            out_shape=jax.ShapeDtypeStruct((b_lat, 2, l1, l2, FN_W), BF16),
            scratch_shapes=[pltpu.VMEM((FN_GROUPS, l1 * _pitch(l2), FN_GW), F32)]
            + [pltpu.VMEM((FN_GROUPS, CT_NB * _pitch(l1), FN_GW), F32)] * 2,
            compiler_params=_params(("arbitrary", "arbitrary")),
            name="dft_lat1",
        )(fx, m1_tab, tc_tab, ts_tab)
        fr_lat = pl.pallas_call(
            functools.partial(_ct2_kernel, l2=l2),
            grid=(b_lat, l1 // CT_KB),
            in_specs=[pl.BlockSpec((None, 2, CT_KB, l2, FN_W), lambda b, j: (b, 0, j, 0, 0)),
                      _const_spec((2 * l2, 2 * l2)), cs_spec],
            out_specs=pl.BlockSpec((None, l2, CT_KB, FN_W), lambda b, j: (b, 0, j, 0)),
            out_shape=jax.ShapeDtypeStruct((b_lat, l2, l1, FN_W), BF16),
            scratch_shapes=[pltpu.VMEM((FN_GROUPS, CT_KB * _pitch(l2), FN_GW), F32)] * 2,
            compiler_params=_params(("arbitrary", "arbitrary")),
            name="dft_lat2",
        )(yct, m2_tab, cs_tab).reshape(nl, FN_W)

        tok = t_out.tok
        split_x = x is None
        ss_spec = lambda blk: pl.BlockSpec((TM_OUT // SUB, N_HEADS, DV, CPS * 2 * DK),
                                           lambda i: (blk(i), 0, 0, 0))
        x1, h2 = pl.pallas_call(
            functools.partial(_mix_out_kernel, n_ctx_tiles=t_out.n_ctx, split_x=split_x),
            grid=(t_out.n,),
            in_specs=([t_out.tok_ctx(d), t_out.tok_lat(d)] if split_x else [tok(d)]) + [
                tok(V_W), tok(FB_W), ss_spec(t_out.ctx_blk), ss_spec(t_out.lat_blk), tok(V_W),
                t_out.tok_ctx(FN_W), t_out.tok_lat(FN_W), tok(d), tok(d), tok(d), t_out.mod(),
                _const_spec((1, DV)), _const_spec((1, d)),
                _const_spec((V_W, d)), _const_spec((FN_W, d)), _const_spec((d, d))],
            out_specs=[tok(d), tok(d)],
            out_shape=[jax.ShapeDtypeStruct((nt, d), F32), jax.ShapeDtypeStruct((nt, d), BF16)],
            scratch_shapes=[pltpu.VMEM((TM_OUT, V_W), F32)],
            compiler_params=_params(("arbitrary",)),
            name="mix_out",
        )(*((xc, xl) if split_x else (x,)), oin, qin, ss_ctx, ss_lat, og, fr_ctx, fr_lat, g0, g2, yp, mods[l],
          gla_norm[l][None, :], norm2[l][None, :],
          w_a_out[l].astype(BF16), w_c_out[l].astype(BF16), w_o[l].astype(BF16))

        final = l == depth - 1
        tok = t_ffn.tok
        nxt = depth - 1 if final else l + 1
        outs = pl.pallas_call(
            functools.partial(_ffn_kernel, final=final, n_ctx_tiles=t_ffn.n_ctx),
            grid=(t_ffn.n,),
            in_specs=[tok(d), tok(d), _const_spec((d, 2 * d_ff)), _const_spec((d_ff, d)),
                      t_ffn.mod(), t_ffn.mod(), _const_spec((1, d))],
            out_specs=[t_ffn.tok_ctx(d), t_ffn.tok_lat(d)] if final else [tok(d), tok(d)],
            out_shape=([jax.ShapeDtypeStruct((nc, d), F32), jax.ShapeDtypeStruct((nl, d), F32)] if final else
                       [jax.ShapeDtypeStruct((nt, d), F32), jax.ShapeDtypeStruct((nt, d), BF16)]),
            compiler_params=_params(("arbitrary",)),
            name="ffn",
        )(h2, x1, w_up[l].astype(BF16), w_down[l].astype(BF16), mods[l], mods[nxt],
          (norm_f if final else norm1[nxt])[None, :])
        if final:
            y_out = outs
        else:
            x, h = outs

    y_prompt = y_out[0].reshape(b_ctx, seq, d)
    y_sample = y_out[1].reshape(b_lat, dec_seq, d)
    new_state_gla = jnp.stack(new_states, axis=1).astype(x_prompt.dtype)
    return (y_prompt, y_sample, new_state_gla)
```

```python
import functools
import math

import numpy as np
import jax
import jax.numpy as jnp
from jax import lax
from jax.experimental import pallas as pl
from jax.experimental.pallas import tpu as pltpu

F32 = jnp.float32
BF16 = jnp.bfloat16

GRID_W = 64
N_HEADS = 4
DK = 64
DV = 128
V_W = N_HEADS * DV
FB_W = N_HEADS * 2 * DK
LOWRANK = 16
GATE_NORMALIZER = 16.0
CHUNK = 64
SUB = 256
CPS = SUB // CHUNK
SC_W = 512
FN_GROUPS = 4
FN_GW = 128
FN_W = FN_GROUPS * FN_GW
N_MOD = 6
EPS = 1e-6
GK_PAD = 128

TM_GLA = 1024
TM_MIX = 512
TM_OUT = 512
TM_FFN = 512
FF_CHUNK = 512
CT_NB = 16
CT_KB = 16
VMEM_LIMIT = 56 * 1024 * 1024


def _dot(a, b):
    return jnp.dot(a, b, preferred_element_type=F32)


def _dot_nt(a, b):
    return lax.dot_general(a, b, (((1,), (1,)), ((), ())), preferred_element_type=F32)


def _dot_tn(a, b):
    return lax.dot_general(a, b, (((0,), (0,)), ((), ())), preferred_element_type=F32)


def _sigmoid(x):
    return 1.0 / (1.0 + jnp.exp(-x))


def _silu(x):
    return x * _sigmoid(x)


def _rms(x):
    return x * lax.rsqrt(jnp.mean(x * x, axis=-1, keepdims=True) + EPS)


def _params(sem):
    return pltpu.CompilerParams(dimension_semantics=sem, vmem_limit_bytes=VMEM_LIMIT)


def _const_spec(shape):
    nd = len(shape)
    return pl.BlockSpec(shape, lambda *_: (0,) * nd, pipeline_mode=pl.Buffered(1))


def _layer_spec(layer, shape):
    nd = len(shape)
    return pl.BlockSpec((None,) + tuple(shape), lambda *_: (layer,) + (0,) * nd, pipeline_mode=pl.Buffered(1))


def _ada_kernel(c_ref, w_ref, b_ref, o_ref):
    s = _silu(c_ref[...]).astype(BF16)
    o_ref[...] = _dot(s, w_ref[...].astype(BF16)) + b_ref[...]


def _ada(cond, w_ada, b_ada):
    depth, d, n = w_ada.shape
    rows = cond.shape[0]
    nb = 1536
    return pl.pallas_call(
        _ada_kernel,
        grid=(depth, n // nb),
        in_specs=[
            pl.BlockSpec((rows, d), lambda l, j: (0, 0)),
            pl.BlockSpec((None, d, nb), lambda l, j: (l, 0, j)),
            pl.BlockSpec((None, 1, nb), lambda l, j: (l, 0, j)),
        ],
        out_specs=pl.BlockSpec((None, rows, nb), lambda l, j: (l, 0, j)),
        out_shape=jax.ShapeDtypeStruct((depth, rows, n), F32),
        compiler_params=_params(("arbitrary", "arbitrary")),
        name="ada",
    )(cond, w_ada, b_ada.reshape(depth, 1, n))


def _prenorm_kernel(xc_ref, xl_ref, mod_ref, n_ref, h_ref, *, n_ctx_tiles):
    x = jnp.where(pl.program_id(0) < n_ctx_tiles, xc_ref[...], xl_ref[...])
    m = mod_ref[...]
    h = _rms(x) * n_ref[...] * (1.0 + m[1:2]) + m[0:1]
    h_ref[...] = h.astype(BF16)


def _gla_in_kernel(h_ref, w_ref, wgk2_ref, bgk_ref,
                   og_ref, qin_ref, oin_ref, u_ref, g_ref,
                   q_s, k_s, v_s, la_s):
    qk_w = N_HEADS * DK
    z = _dot(h_ref[...], w_ref[...])
    lo_half = lax.broadcasted_iota(jnp.int32, (1, 2 * DK), 1) < DK

    def both_directions(x):
        tiles = []
        for p in range(N_HEADS // 2):
            t = x[:, p * 2 * DK:(p + 1) * 2 * DK]
            r = pltpu.roll(t, DK, axis=1)
            tiles += [jnp.where(lo_half, t, r), jnp.where(lo_half, r, t)]
        return jnp.concatenate(tiles, axis=1)

    q_s[...] = both_directions(z[:, :qk_w]) * (DK ** -0.5)
    k_s[...] = both_directions(z[:, qk_w:2 * qk_w])
    v_s[...] = z[:, 2 * qk_w:2 * qk_w + V_W].astype(BF16)
    og_ref[...] = z[:, 2 * qk_w + V_W:2 * qk_w + 2 * V_W].astype(BF16)
    gk = z[:, 2 * qk_w + 2 * V_W:].astype(BF16)
    lp = _dot(gk, wgk2_ref[...]) + bgk_ref[...]
    la_s[...] = (jnp.minimum(lp, 0.0) - jnp.log(1.0 + jnp.exp(-jnp.abs(lp)))) * (1.0 / GATE_NORMALIZER)

    row = lax.broadcasted_iota(jnp.int32, (SUB, SUB), 0)
    col = lax.broadcasted_iota(jnp.int32, (SUB, SUB), 1)
    same = (row & -CHUNK) == (col & -CHUNK)
    lower = same & (col <= row)
    upper = same & (col >= row)
    tri = jnp.where(lower, 1.0, 0.0).astype(BF16)
    is_f = (lax.broadcasted_iota(jnp.int32, (SUB, FB_W), 1) & (2 * DK - 1)) < DK
    rchunk = lax.broadcasted_iota(jnp.int32, (SUB, 2 * DK), 0) & -CHUNK

    def sub_tile(s, carry):
        r0 = pl.multiple_of(s * SUB, SUB)
        rows = pl.ds(r0, SUB)
        la = la_s[rows, :]
        hi = la.astype(BF16)
        r1 = la - hi.astype(F32)
        mid = r1.astype(BF16)
        lo = (r1 - mid.astype(F32)).astype(BF16)
        pre = _dot(tri, hi) + _dot(tri, mid) + _dot(tri, lo)
        tot_rows = [pre[c * CHUNK + CHUNK - 1:c * CHUNK + CHUNK, :] for c in range(CPS)]
        tot = jnp.concatenate([jnp.broadcast_to(t, (CHUNK, FB_W)) for t in tot_rows], axis=0)
        b = jnp.where(is_f, pre, tot - pre + la)
        q = q_s[rows, :]
        k = k_s[rows, :]
        qin = (q * jnp.exp(b)).astype(BF16)
        kin = (k * jnp.exp(-b)).astype(BF16)
        kout = (k * jnp.exp(tot - b)).astype(BF16)
        qin_ref[rows, :] = qin
        zero = jnp.zeros_like(qin)
        qf = jnp.where(is_f, qin, zero)
        qb = jnp.where(is_f, zero, qin)
        v = v_s[rows, :]
        for hh in range(N_HEADS):
            fb = slice(hh * 2 * DK, (hh + 1) * 2 * DK)
            vs = slice(hh * DV, (hh + 1) * DV)
            a2 = _dot_nt(jnp.concatenate([qf[:, fb], qb[:, fb]], axis=0), kin[:, fb])
            att = (jnp.where(lower, a2[:SUB], 0.0) + jnp.where(upper, a2[SUB:], 0.0)).astype(BF16)
            oin_ref[rows, vs] = _dot(att, v[:, vs]).astype(BF16)
            ko = kout[:, fb]
            kbd = jnp.concatenate([jnp.where(rchunk == c * CHUNK, ko, jnp.zeros_like(ko)) for c in range(CPS)],
                                  axis=1)
            u_ref[s, hh] = _dot_tn(v[:, vs], kbd).astype(BF16)
            g_ref[s, hh] = jnp.concatenate([jnp.exp(t[:, fb]) for t in tot_rows], axis=1)
        return carry

    lax.fori_loop(0, h_ref.shape[0] // SUB, sub_tile, 0, unroll=2)


def _mix_in_kernel(h_ref, w_ref, cw_ref, wb_ref,
                   fx_ref, g0_ref, g2_ref, yp_ref, *, n_ctx_tiles, period_ctx, period_lat):
    i = pl.program_id(0)
    d = g0_ref.shape[-1]
    z = _dot(h_ref[...], w_ref[...])
    sb, sc, sx = z[:, :SC_W], z[:, SC_W:2 * SC_W], z[:, 2 * SC_W:3 * SC_W]
    m0 = 3 * SC_W + FN_W
    u = sc * sx
    tm = h_ref.shape[0]
    period = jnp.where(i < n_ctx_tiles, period_ctx, period_lat)
    pos = lax.broadcasted_iota(jnp.int32, (tm, 1), 0) & (period - 1)
    up = jnp.where(pos == 0, 0.0, pltpu.roll(u, 1, axis=0))
    un = jnp.where(pos == period - 1, 0.0, pltpu.roll(u, tm - 1, axis=0))
    cw = cw_ref[...]
    conv = cw[0:1] * up + cw[1:2] * u + cw[2:3] * un
    yb = _dot((sb * conv).astype(BF16), wb_ref[...])

    _put_rows(fx_ref, 0, z[:, 3 * SC_W:m0])
    g0_ref[...] = _sigmoid(z[:, m0:m0 + d]).astype(BF16)
    yp_ref[...] = (_sigmoid(z[:, m0 + d:m0 + 2 * d]) * yb).astype(BF16)
    g2_ref[...] = _sigmoid(z[:, m0 + 2 * d:m0 + 3 * d]).astype(BF16)


def _scan_kernel(u_ref, g_ref, s0_ref, ss_ref, sfin_ref, *, n_sub):
    lane_f = lax.broadcasted_iota(jnp.int32, (DV, 2 * DK), 1) < DK
    for q in range(s0_ref.shape[0]):
        for hh in range(s0_ref.shape[1]):
            s0 = s0_ref[q, hh]

            def fwd(j, st):
                for c in range(CPS):
                    cs = slice(c * 2 * DK, (c + 1) * 2 * DK)
                    ss_ref[j, hh, :, cs] = st.astype(BF16)
                    st = g_ref[j, hh, :, cs] * st + u_ref[j, hh, :, cs].astype(F32)
                return st

            def bwd(j, st):
                for c in range(CPS - 1, -1, -1):
                    cs = slice(c * 2 * DK, (c + 1) * 2 * DK)
                    ss_ref[j, hh, :, cs] = jnp.where(lane_f, ss_ref[j, hh, :, cs], st.astype(BF16))
                    st = g_ref[j, hh, :, cs] * st + u_ref[j, hh, :, cs].astype(F32)
                return st

            lo = q * n_sub
            sf = lax.fori_loop(0, n_sub, lambda i, st: fwd(lo + i, st), s0)
            sb = lax.fori_loop(0, n_sub, lambda i, st: bwd(lo + n_sub - 1 - i, st), s0)
            sfin_ref[q, hh] = jnp.where(lane_f, sf, sb)


def _scan(u, g, s0, layer, n_seq, n_sub, sub0, seq_blk, head_blk):
    rows = seq_blk * n_sub
    assert n_seq % seq_blk == 0 and N_HEADS % head_blk == 0 and sub0 % rows == 0
    wide = CPS * 2 * DK
    return pl.pallas_call(
        functools.partial(_scan_kernel, n_sub=n_sub),
        grid=(n_seq // seq_blk, N_HEADS // head_blk),
        in_specs=[
            pl.BlockSpec((rows, head_blk, DV, wide), lambda b, hh: (sub0 // rows + b, hh, 0, 0)),
            pl.BlockSpec((rows, head_blk, 1, wide), lambda b, hh: (sub0 // rows + b, hh, 0, 0)),
            pl.BlockSpec((None, seq_blk, head_blk, DV, 2 * DK), lambda b, hh: (layer, b, hh, 0, 0)),
        ],
        out_specs=[
            pl.BlockSpec((rows, head_blk, DV, wide), lambda b, hh: (b, hh, 0, 0)),
            pl.BlockSpec((seq_blk, head_blk, DV, 2 * DK), lambda b, hh: (b, hh, 0, 0)),
        ],
        out_shape=[
            jax.ShapeDtypeStruct((n_seq * n_sub, N_HEADS, DV, wide), BF16),
            jax.ShapeDtypeStruct((n_seq, N_HEADS, DV, 2 * DK), F32),
        ],
        compiler_params=_params(("arbitrary", "arbitrary")),
        name="scan",
    )(u, g, s0)


def _gather_rows(ref, start, size, stride):
    return jnp.concatenate([ref[g, pl.ds(start, size, stride=stride), :] for g in range(FN_GROUPS)], axis=1)


def _put_rows(ref, r0, val):
    for g in range(FN_GROUPS):
        ref[g, r0:r0 + val.shape[0], :] = val[:, g * FN_GW:(g + 1) * FN_GW]


def _channel_dft_real(re_g, im_g, cs):
    return _dot(jnp.concatenate([re_g.astype(BF16), im_g.astype(BF16)], axis=1), cs)


def _dft_direct_kernel(x_ref, m_ref, cs_ref, o_ref):
    seq = x_ref.shape[1]
    x = jnp.concatenate([x_ref[g] for g in range(FN_GROUPS)], axis=1)
    res = _dot(m_ref[...], x.astype(BF16))
    for g in range(FN_GROUPS):
        gs = slice(g * FN_GW, (g + 1) * FN_GW)
        o_ref[:, gs] = _channel_dft_real(res[:seq, gs], res[seq:, gs], cs_ref[...]).astype(BF16)


def _pitch(n):
    return n + 4


def _ct1_kernel(x_ref, m_ref, tc_ref, ts_ref, y_ref, xs_s, tr_s, ti_s, *, l1, l2):
    j = pl.program_id(1)
    px, pt = _pitch(l2), _pitch(l1)

    @pl.when(j == 0)
    def _():
        for n1 in range(l1):
            for g in range(FN_GROUPS):
                xs_s[g, n1 * px:n1 * px + l2, :] = x_ref[g, n1 * l2:(n1 + 1) * l2, :]

    m = m_ref[...]
    for jn in range(CT_NB):
        xs = _gather_rows(xs_s, j * CT_NB + jn, l1, px)
        res = _dot(m, xs.astype(BF16))
        yr, yi = res[:l1], res[l1:]
        c = jnp.concatenate([tc_ref[jn]] * FN_GROUPS, axis=1)
        s = jnp.concatenate([ts_ref[jn]] * FN_GROUPS, axis=1)
        _put_rows(tr_s, jn * pt, yr * c + yi * s)
        _put_rows(ti_s, jn * pt, yi * c - yr * s)
    for k1 in range(l1):
        y_ref[0, k1] = _gather_rows(tr_s, k1, CT_NB, pt).astype(BF16)
        y_ref[1, k1] = _gather_rows(ti_s, k1, CT_NB, pt).astype(BF16)


def _ct2_kernel(y_ref, m_ref, cs_ref, o_ref, tr_s, ti_s, *, l2):
    m = m_ref[...]
    pt = _pitch(l2)

    @pl.when((pl.program_id(0) == 0) & (pl.program_id(1) == 0))
    def _():
        tr_s[...] = jnp.zeros_like(tr_s)
        ti_s[...] = jnp.zeros_like(ti_s)

    for kk in range(CT_KB):
        rhs = jnp.concatenate([y_ref[0, kk], y_ref[1, kk]], axis=0)
        res = _dot(m, rhs)
        _put_rows(tr_s, kk * pt, res[:l2])
        _put_rows(ti_s, kk * pt, res[l2:])
    for g in range(FN_GROUPS):
        tr_s[g] = _channel_dft_real(tr_s[g], ti_s[g], cs_ref[...])
    for k2 in range(l2):
        o_ref[k2] = _gather_rows(tr_s, k2, CT_KB, pt).astype(BF16)


def _cos_sin(n_out, n_in, period, scale):
    idx = (np.arange(n_out)[:, None] * np.arange(n_in)[None, :]) % period
    ang = 2.0 * np.pi * idx / period
    return np.cos(ang) * scale, np.sin(ang) * scale


def _mix_out_kernel(*refs, n_ctx_tiles, split_x):
    if split_x:
        xc_ref, xl_ref, *refs = refs
    else:
        x_ref, *refs = refs
    (oin_ref, qin_ref, ssc_ref, ssl_ref, og_ref, frc_ref, frl_ref, g0_ref, g2_ref, yp_ref, mod_ref,
     gn_ref, n2_ref, wa_ref, wc_ref, wo_ref, x1_ref, h2_ref, o_s) = refs
    is_ctx = pl.program_id(0) < n_ctx_tiles
    x = jnp.where(is_ctx, xc_ref[...], xl_ref[...]) if split_x else x_ref[...]
    for s in range(oin_ref.shape[0] // SUB):
        for hh in range(N_HEADS):
            st = jnp.where(is_ctx, ssc_ref[s, hh], ssl_ref[s, hh])
            fb = slice(hh * 2 * DK, (hh + 1) * 2 * DK)
            vs = slice(hh * DV, (hh + 1) * DV)
            for c in range(CPS):
                rows = slice(s * SUB + c * CHUNK, s * SUB + (c + 1) * CHUNK)
                o_s[rows, vs] = oin_ref[rows, vs].astype(F32) + _dot_nt(qin_ref[rows, fb], st[:, c * 2 * DK:(c + 1) * 2 * DK])
    gn = gn_ref[...]
    parts = []
    for hh in range(N_HEADS):
        vs = slice(hh * DV, (hh + 1) * DV)
        parts.append((_rms(o_s[:, vs]) * gn * _silu(og_ref[:, vs].astype(F32))).astype(BF16))
    ya = _dot(jnp.concatenate(parts, axis=1), wa_ref[...])
    yc = _dot(jnp.where(is_ctx, frc_ref[...], frl_ref[...]), wc_ref[...])
    y = g0_ref[...].astype(F32) * ya + yp_ref[...].astype(F32) + g2_ref[...].astype(F32) * yc
    mix = _dot(y.astype(BF16), wo_ref[...])
    m = mod_ref[...]
    x1 = x + m[2:3] * mix
    x1_ref[...] = x1
    h2_ref[...] = (_rms(x1) * n2_ref[...] * (1.0 + m[4:5]) + m[3:4]).astype(BF16)


def _ffn_kernel(h2_ref, x1_ref, wup_ref, wd_ref, mod_ref, modn_ref, nn_ref, *outs, final, n_ctx_tiles):
    h2 = h2_ref[...]
    d_ff = wd_ref.shape[0]
    y = None
    for c0 in range(0, d_ff, FF_CHUNK):
        c1 = min(c0 + FF_CHUNK, d_ff)
        gate = _dot(h2, wup_ref[:, c0:c1])
        up = _dot(h2, wup_ref[:, d_ff + c0:d_ff + c1])
        part = _dot((_silu(gate) * up).astype(BF16), wd_ref[c0:c1, :])
        y = part if y is None else y + part
    x2 = x1_ref[...] + mod_ref[5:6, :] * y
    if final:
        yc_ref, yl_ref = outs
        out = _rms(x2) * nn_ref[...]
        is_ctx = pl.program_id(0) < n_ctx_tiles

        @pl.when(is_ctx)
        def _():
            yc_ref[...] = out

        @pl.when(jnp.logical_not(is_ctx))
        def _():
            yl_ref[...] = out
    else:
        x2_ref, hn_ref = outs
        x2_ref[...] = x2
        hn_ref[...] = (_rms(x2) * nn_ref[...] * (1.0 + modn_ref[1:2, :]) + modn_ref[0:1, :]).astype(BF16)


def kernel(x_prompt, x_sample, state_gla, c, c_ctx, w_ada, b_ada, norm1, norm2, w_in, w_gk_f, b_gk_f,
           w_gk_b, b_gk_b, gla_norm, w_a_out, conv_w, w_b_out, w_c_out, w_o, w_up, w_down, norm_f):
    b_ctx, seq, d = x_prompt.shape
    b_lat, dec_seq, _ = x_sample.shape
    depth = w_ada.shape[0]
    d_ff = w_down.shape[1]
    nc, nl = b_ctx * seq, b_lat * dec_seq
    nt = nc + nl
    l1, l2 = dec_seq // GRID_W, GRID_W
    assert seq == SUB and d_ff % 128 == 0
    assert nc % dec_seq == 0 and l2 % CT_NB == 0 and l1 % CT_KB == 0
    n_sub_tot, n_sub_lat = nt // SUB, dec_seq // SUB
    assert (nc // SUB) % n_sub_lat == 0
    n_cond = -(-(1 + b_lat) // 8) * 8

    class Tiling:
        def __init__(self, tm):
            assert nc % tm == 0 and dec_seq % tm == 0
            self.tm, self.n, self.n_ctx, self.per_lat = tm, nt // tm, nc // tm, dec_seq // tm

        def cond(self, i):
            return jnp.where(i < self.n_ctx, 0, 1 + (i - self.n_ctx) // self.per_lat)

        def ctx_blk(self, i):
            return jnp.minimum(i, self.n_ctx - 1)

        def lat_blk(self, i):
            return jnp.maximum(i - self.n_ctx, 0)

        def tok(self, w):
            return pl.BlockSpec((self.tm, w), lambda i, *_: (i, 0))

        def tok_ctx(self, w):
            return pl.BlockSpec((self.tm, w), lambda i, *_: (self.ctx_blk(i), 0))

        def tok_lat(self, w):
            return pl.BlockSpec((self.tm, w), lambda i, *_: (self.lat_blk(i), 0))

        def mod(self, layer):
            return pl.BlockSpec((None, None, N_MOD, d), lambda i, *_: (layer, self.cond(i), 0, 0))

    t_gla, t_mix, t_out, t_ffn = Tiling(TM_GLA), Tiling(TM_MIX), Tiling(TM_OUT), Tiling(TM_FFN)

    cond = jnp.concatenate([c_ctx[None, :], c, jnp.zeros((n_cond - 1 - b_lat, d), F32)], axis=0)
    mods = _ada(cond, w_ada, b_ada).reshape(depth, n_cond, N_MOD, d)

    cc, sc_ = _cos_sin(FN_GW, FN_GW, FN_GW, FN_GW ** -0.5)
    cs_tab = jnp.asarray(np.concatenate([cc, sc_], axis=0), F32).astype(BF16)
    cl, sl = _cos_sin(seq, seq, seq, seq ** -0.5)
    m_ctx_tab = jnp.asarray(np.concatenate([cl, -sl], axis=0), F32).astype(BF16)
    c1, s1 = _cos_sin(l1, l1, l1, l1 ** -0.5)
    m1_tab = jnp.asarray(np.concatenate([c1, -s1], axis=0), F32).astype(BF16)
    tcn, tsn = _cos_sin(l2, l1, l1 * l2, 1.0)
    tc_tab = jnp.asarray(np.repeat(tcn[:, :, None], 128, axis=2), F32)
    ts_tab = jnp.asarray(np.repeat(tsn[:, :, None], 128, axis=2), F32)
    c2, s2 = _cos_sin(l2, l2, l2, l2 ** -0.5)
    m2_tab = jnp.asarray(np.block([[c2, s2], [-s2, c2]]), F32).astype(BF16)

    xc, xl = x_prompt.reshape(nc, d), x_sample.reshape(nl, d)
    x = None

    gla_cols = 2 * N_HEADS * DK + 2 * V_W
    mix_col0 = gla_cols + 2 * LOWRANK
    w_gla = w_in[:, :, :gla_cols + GK_PAD].astype(BF16)
    w_mix = w_in[:, :, mix_col0:].astype(BF16)
    zf = jnp.zeros((depth, LOWRANK, N_HEADS, DK), F32)
    top = jnp.stack([w_gk_f.reshape(depth, LOWRANK, N_HEADS, DK), zf], axis=3).reshape(depth, LOWRANK, FB_W)
    bot = jnp.stack([zf, w_gk_b.reshape(depth, LOWRANK, N_HEADS, DK)], axis=3).reshape(depth, LOWRANK, FB_W)
    wgk2 = jnp.concatenate([top, bot, jnp.zeros((depth, GK_PAD - 2 * LOWRANK, FB_W), F32)], axis=1).astype(BF16)
    bgk = jnp.stack([b_gk_f.reshape(depth, N_HEADS, DK), b_gk_b.reshape(depth, N_HEADS, DK)],
                    axis=2).reshape(depth, 1, FB_W)
    wb, wa, wc_out, wo = (t.astype(BF16) for t in (w_b_out, w_a_out, w_c_out, w_o))
    wup, wdn = w_up.astype(BF16), w_down.astype(BF16)
    n1, n2, gn = norm1[:, None, :], norm2[:, None, :], gla_norm[:, None, :]
    s0_ctx = jnp.zeros((1, b_ctx, N_HEADS, DV, 2 * DK), F32)
    s0_lat = state_gla.transpose(1, 0, 3, 5, 2, 4).reshape(depth, b_lat, N_HEADS, DV, 2 * DK)

    h = pl.pallas_call(
        functools.partial(_prenorm_kernel, n_ctx_tiles=t_ffn.n_ctx),
        grid=(t_ffn.n,),
        in_specs=[t_ffn.tok_ctx(d), t_ffn.tok_lat(d), t_ffn.mod(0), _layer_spec(0, (1, d))],
        out_specs=t_ffn.tok(d),
        out_shape=jax.ShapeDtypeStruct((nt, d), BF16),
        compiler_params=_params(("arbitrary",)),
        name="prenorm",
    )(xc, xl, mods, n1)

    new_states = []
    y_out = None
    for l in range(depth):
        tok = t_gla.tok
        og, qin, oin, u, g = pl.pallas_call(
            _gla_in_kernel,
            grid=(t_gla.n,),
            in_specs=[tok(d), _layer_spec(l, (d, gla_cols + GK_PAD)), _layer_spec(l, (GK_PAD, FB_W)),
                      _layer_spec(l, (1, FB_W))],
            out_specs=[tok(V_W), tok(FB_W), tok(V_W),
                       pl.BlockSpec((TM_GLA // SUB, N_HEADS, DV, CPS * 2 * DK), lambda i: (i, 0, 0, 0)),
                       pl.BlockSpec((TM_GLA // SUB, N_HEADS, 1, CPS * 2 * DK), lambda i: (i, 0, 0, 0))],
            out_shape=[jax.ShapeDtypeStruct((nt, V_W), BF16), jax.ShapeDtypeStruct((nt, FB_W), BF16),
                       jax.ShapeDtypeStruct((nt, V_W), BF16),
                       jax.ShapeDtypeStruct((n_sub_tot, N_HEADS, DV, CPS * 2 * DK), BF16),
                       jax.ShapeDtypeStruct((n_sub_tot, N_HEADS, 1, CPS * 2 * DK), F32)],
            scratch_shapes=[pltpu.VMEM((TM_GLA, FB_W), F32), pltpu.VMEM((TM_GLA, FB_W), F32),
                            pltpu.VMEM((TM_GLA, V_W), BF16), pltpu.VMEM((TM_GLA, FB_W), F32)],
            compiler_params=_params(("arbitrary",)),
            name="gla_in",
        )(h, w_gla, wgk2, bgk)

        tok = t_mix.tok
        fx, g0, g2, yp = pl.pallas_call(
            functools.partial(_mix_in_kernel, n_ctx_tiles=t_mix.n_ctx, period_ctx=seq, period_lat=GRID_W),
            grid=(t_mix.n,),
            in_specs=[tok(d), _layer_spec(l, (d, 3 * SC_W + FN_W + 3 * d)),
                      _layer_spec(l, (3, SC_W)), _layer_spec(l, (SC_W, d))],
            out_specs=[pl.BlockSpec((FN_GROUPS, TM_MIX, FN_GW), lambda i: (0, i, 0)), tok(d), tok(d), tok(d)],
            out_shape=[jax.ShapeDtypeStruct((FN_GROUPS, nt, FN_GW), F32)] + [jax.ShapeDtypeStruct((nt, d), BF16)] * 3,
            compiler_params=_params(("arbitrary",)),
            name="mix_in",
        )(h, w_mix, conv_w, wb)

        ss_ctx, sfin = _scan(u, g, s0_ctx, 0, b_ctx, seq // SUB, 0, math.gcd(b_ctx, 4), N_HEADS)
        ss_lat, _ = _scan(u, g, s0_lat, l, b_lat, n_sub_lat, nc // SUB, 1, 1)
        new_states.append(sfin)

        cs_spec = _const_spec((2 * FN_GW, FN_GW))
        fr_ctx = pl.pallas_call(
            _dft_direct_kernel,
            grid=(b_ctx,),
            in_specs=[pl.BlockSpec((FN_GROUPS, seq, FN_GW), lambda s: (0, s, 0)), _const_spec((2 * seq, seq)),
                      cs_spec],
            out_specs=pl.BlockSpec((seq, FN_W), lambda s: (s, 0)),
            out_shape=jax.ShapeDtypeStruct((nc, FN_W), BF16),
            compiler_params=_params(("arbitrary",)),
            name="dft_ctx",
        )(fx, m_ctx_tab, cs_tab)
        lat_blk0 = nc // dec_seq
        yct = pl.pallas_call(
            functools.partial(_ct1_kernel, l1=l1, l2=l2),
            grid=(b_lat, l2 // CT_NB),
            in_specs=[pl.BlockSpec((FN_GROUPS, dec_seq, FN_GW), lambda b, j: (0, lat_blk0 + b, 0)),
                      _const_spec((2 * l1, l1)),
                      pl.BlockSpec((CT_NB, l1, 128), lambda b, j: (j, 0, 0)),
                      pl.BlockSpec((CT_NB, l1, 128), lambda b, j: (j, 0, 0))],
            out_specs=pl.BlockSpec((None, 2, l1, CT_NB, FN_W), lambda b, j: (b, 0, 0, j, 0)),
            out_shape=jax.ShapeDtypeStruct((b_lat, 2, l1, l2, FN_W), BF16),
            scratch_shapes=[pltpu.VMEM((FN_GROUPS, l1 * _pitch(l2), FN_GW), F32)]
            + [pltpu.VMEM((FN_GROUPS, CT_NB * _pitch(l1), FN_GW), F32)] * 2,
            compiler_params=_params(("arbitrary", "arbitrary")),
            name="dft_lat1",
        )(fx, m1_tab, tc_tab, ts_tab)
        fr_lat = pl.pallas_call(
            functools.partial(_ct2_kernel, l2=l2),
            grid=(b_lat, l1 // CT_KB),
            in_specs=[pl.BlockSpec((None, 2, CT_KB, l2, FN_W), lambda b, j: (b, 0, j, 0, 0)),
                      _const_spec((2 * l2, 2 * l2)), cs_spec],
            out_specs=pl.BlockSpec((None, l2, CT_KB, FN_W), lambda b, j: (b, 0, j, 0)),
            out_shape=jax.ShapeDtypeStruct((b_lat, l2, l1, FN_W), BF16),
            scratch_shapes=[pltpu.VMEM((FN_GROUPS, CT_KB * _pitch(l2), FN_GW), F32)] * 2,
            compiler_params=_params(("arbitrary", "arbitrary")),
            name="dft_lat2",
        )(yct, m2_tab, cs_tab).reshape(nl, FN_W)

        tok = t_out.tok
        split_x = x is None
        ss_spec = lambda blk: pl.BlockSpec((TM_OUT // SUB, N_HEADS, DV, CPS * 2 * DK),
                                           lambda i: (blk(i), 0, 0, 0))
        x1, h2 = pl.pallas_call(
            functools.partial(_mix_out_kernel, n_ctx_tiles=t_out.n_ctx, split_x=split_x),
            grid=(t_out.n,),
            in_specs=([t_out.tok_ctx(d), t_out.tok_lat(d)] if split_x else [tok(d)]) + [
                tok(V_W), tok(FB_W), ss_spec(t_out.ctx_blk), ss_spec(t_out.lat_blk), tok(V_W),
                t_out.tok_ctx(FN_W), t_out.tok_lat(FN_W), tok(d), tok(d), tok(d), t_out.mod(l),
                _layer_spec(l, (1, DV)), _layer_spec(l, (1, d)),
                _layer_spec(l, (V_W, d)), _layer_spec(l, (FN_W, d)), _layer_spec(l, (d, d))],
            out_specs=[tok(d), tok(d)],
            out_shape=[jax.ShapeDtypeStruct((nt, d), F32), jax.ShapeDtypeStruct((nt, d), BF16)],
            scratch_shapes=[pltpu.VMEM((TM_OUT, V_W), F32)],
            compiler_params=_params(("arbitrary",)),
            name="mix_out",
        )(*((xc, xl) if split_x else (x,)), oin, qin, ss_ctx, ss_lat, og, fr_ctx, fr_lat, g0, g2, yp, mods,
          gn, n2, wa, wc_out, wo)

        final = l == depth - 1
        tok = t_ffn.tok
        nxt = depth - 1 if final else l + 1
        outs = pl.pallas_call(
            functools.partial(_ffn_kernel, final=final, n_ctx_tiles=t_ffn.n_ctx),
            grid=(t_ffn.n,),
            in_specs=[tok(d), tok(d), _layer_spec(l, (d, 2 * d_ff)), _layer_spec(l, (d_ff, d)),
                      t_ffn.mod(l), t_ffn.mod(nxt), _const_spec((1, d)) if final else _layer_spec(nxt, (1, d))],
            out_specs=[t_ffn.tok_ctx(d), t_ffn.tok_lat(d)] if final else [tok(d), tok(d)],
            out_shape=([jax.ShapeDtypeStruct((nc, d), F32), jax.ShapeDtypeStruct((nl, d), F32)] if final else
                       [jax.ShapeDtypeStruct((nt, d), F32), jax.ShapeDtypeStruct((nt, d), BF16)]),
            compiler_params=_params(("arbitrary",)),
            name="ffn",
        )(h2, x1, wup, wdn, mods, mods, norm_f[None, :] if final else n1)
        if final:
            y_out = outs
        else:
            x, h = outs

    y_prompt = y_out[0].reshape(b_ctx, seq, d)
    y_sample = y_out[1].reshape(b_lat, dec_seq, d)
    sfin_all = jnp.stack(new_states, axis=0).reshape(depth, b_ctx, N_HEADS, DV, 2, DK)
    new_state_gla = sfin_all.transpose(1, 0, 4, 2, 5, 3).astype(x_prompt.dtype)
    return (y_prompt, y_sample, new_state_gla)
```

```python
import functools
import math

import numpy as np
import jax
import jax.numpy as jnp
from jax import lax
from jax.experimental import pallas as pl
from jax.experimental.pallas import tpu as pltpu

F32 = jnp.float32
BF16 = jnp.bfloat16

GRID_W = 64
N_HEADS = 4
DK = 64
DV = 128
V_W = N_HEADS * DV
FB_W = N_HEADS * 2 * DK
LOWRANK = 16
GATE_NORMALIZER = 16.0
CHUNK = 64
SUB = 256
CPS = SUB // CHUNK
SC_W = 512
FN_GROUPS = 4
FN_GW = 128
FN_W = FN_GROUPS * FN_GW
N_MOD = 6
EPS = 1e-6
GK_PAD = 128

TM_GLA = 1024
TM_MIX = 512
TM_OUT = 512
TM_FFN = 512
FF_CHUNK = 512
CT_NB = 16
CT_KB = 16
VMEM_LIMIT = 56 * 1024 * 1024


def _dot(a, b):
    return jnp.dot(a, b, preferred_element_type=F32)


def _dot_nt(a, b):
    return lax.dot_general(a, b, (((1,), (1,)), ((), ())), preferred_element_type=F32)


def _dot_tn(a, b):
    return lax.dot_general(a, b, (((0,), (0,)), ((), ())), preferred_element_type=F32)


def _sigmoid(x):
    return 1.0 / (1.0 + jnp.exp(-x))


def _silu(x):
    return x * _sigmoid(x)


def _rms(x):
    return x * lax.rsqrt(jnp.mean(x * x, axis=-1, keepdims=True) + EPS)


def _params(sem):
    return pltpu.CompilerParams(dimension_semantics=sem, vmem_limit_bytes=VMEM_LIMIT)


def _const_spec(shape):
    nd = len(shape)
    return pl.BlockSpec(shape, lambda *_: (0,) * nd, pipeline_mode=pl.Buffered(1))


def _layer_spec(layer, shape):
    nd = len(shape)
    return pl.BlockSpec((None,) + tuple(shape), lambda *_: (layer,) + (0,) * nd, pipeline_mode=pl.Buffered(1))


def _ada_kernel(c_ref, w_ref, b_ref, o_ref):
    s = _silu(c_ref[...]).astype(BF16)
    o_ref[...] = _dot(s, w_ref[...].astype(BF16)) + b_ref[...]


def _ada(cond, w_ada, b_ada):
    depth, d, n = w_ada.shape
    rows = cond.shape[0]
    nb = 1536
    return pl.pallas_call(
        _ada_kernel,
        grid=(depth, n // nb),
        in_specs=[
            pl.BlockSpec((rows, d), lambda l, j: (0, 0)),
            pl.BlockSpec((None, d, nb), lambda l, j: (l, 0, j)),
            pl.BlockSpec((None, 1, nb), lambda l, j: (l, 0, j)),
        ],
        out_specs=pl.BlockSpec((None, rows, nb), lambda l, j: (l, 0, j)),
        out_shape=jax.ShapeDtypeStruct((depth, rows, n), F32),
        compiler_params=_params(("arbitrary", "arbitrary")),
        name="ada",
    )(cond, w_ada, b_ada.reshape(depth, 1, n))


def _split_w_in_kernel(w_ref, gla_ref, mix_ref):
    w = w_ref[...]
    gla_ref[...] = w[:, :gla_ref.shape[-1]].astype(BF16)
    mix_ref[...] = w[:, w.shape[-1] - mix_ref.shape[-1]:].astype(BF16)


def _split_w_in(w_in, gla_w, mix_w):
    depth, d, p_in = w_in.shape
    rb = 256
    return pl.pallas_call(
        _split_w_in_kernel,
        grid=(depth, d // rb),
        in_specs=[pl.BlockSpec((None, rb, p_in), lambda l, r: (l, r, 0))],
        out_specs=[pl.BlockSpec((None, rb, gla_w), lambda l, r: (l, r, 0)),
                   pl.BlockSpec((None, rb, mix_w), lambda l, r: (l, r, 0))],
        out_shape=[jax.ShapeDtypeStruct((depth, d, gla_w), BF16), jax.ShapeDtypeStruct((depth, d, mix_w), BF16)],
        compiler_params=_params(("arbitrary", "arbitrary")),
        name="split_w_in",
    )(w_in)


def _prenorm_kernel(xc_ref, xl_ref, mod_ref, n_ref, h_ref, *, n_ctx_tiles):
    x = jnp.where(pl.program_id(0) < n_ctx_tiles, xc_ref[...], xl_ref[...])
    m = mod_ref[...]
    h = _rms(x) * n_ref[...] * (1.0 + m[1:2]) + m[0:1]
    h_ref[...] = h.astype(BF16)


def _gla_in_kernel(h_ref, w_ref, wgk2_ref, bgk_ref,
                   og_ref, qin_ref, oin_ref, u_ref, g_ref,
                   q_s, k_s, v_s, la_s):
    qk_w = N_HEADS * DK
    z = _dot(h_ref[...], w_ref[...])
    lo_half = lax.broadcasted_iota(jnp.int32, (1, 2 * DK), 1) < DK

    def both_directions(x):
        tiles = []
        for p in range(N_HEADS // 2):
            t = x[:, p * 2 * DK:(p + 1) * 2 * DK]
            r = pltpu.roll(t, DK, axis=1)
            tiles += [jnp.where(lo_half, t, r), jnp.where(lo_half, r, t)]
        return jnp.concatenate(tiles, axis=1)

    q_s[...] = both_directions(z[:, :qk_w]) * (DK ** -0.5)
    k_s[...] = both_directions(z[:, qk_w:2 * qk_w])
    v_s[...] = z[:, 2 * qk_w:2 * qk_w + V_W].astype(BF16)
    og_ref[...] = z[:, 2 * qk_w + V_W:2 * qk_w + 2 * V_W].astype(BF16)
    gk = z[:, 2 * qk_w + 2 * V_W:].astype(BF16)
    lp = _dot(gk, wgk2_ref[...]) + bgk_ref[...]
    la_s[...] = (jnp.minimum(lp, 0.0) - jnp.log(1.0 + jnp.exp(-jnp.abs(lp)))) * (1.0 / GATE_NORMALIZER)

    row = lax.broadcasted_iota(jnp.int32, (SUB, SUB), 0)
    col = lax.broadcasted_iota(jnp.int32, (SUB, SUB), 1)
    same = (row & -CHUNK) == (col & -CHUNK)
    lower = same & (col <= row)
    upper = same & (col >= row)
    tri = jnp.where(lower, 1.0, 0.0).astype(BF16)
    is_f = (lax.broadcasted_iota(jnp.int32, (SUB, FB_W), 1) & (2 * DK - 1)) < DK
    rchunk = lax.broadcasted_iota(jnp.int32, (SUB, 2 * DK), 0) & -CHUNK

    def sub_tile(s, carry):
        r0 = pl.multiple_of(s * SUB, SUB)
        rows = pl.ds(r0, SUB)
        la = la_s[rows, :]
        hi = la.astype(BF16)
        r1 = la - hi.astype(F32)
        mid = r1.astype(BF16)
        lo = (r1 - mid.astype(F32)).astype(BF16)
        pre = _dot(tri, hi) + _dot(tri, mid) + _dot(tri, lo)
        tot_rows = [pre[c * CHUNK + CHUNK - 1:c * CHUNK + CHUNK, :] for c in range(CPS)]
        tot = jnp.concatenate([jnp.broadcast_to(t, (CHUNK, FB_W)) for t in tot_rows], axis=0)
        b = jnp.where(is_f, pre, tot - pre + la)
        q = q_s[rows, :]
        k = k_s[rows, :]
        qin = (q * jnp.exp(b)).astype(BF16)
        kin = (k * jnp.exp(-b)).astype(BF16)
        kout = (k * jnp.exp(tot - b)).astype(BF16)
        qin_ref[rows, :] = qin
        zero = jnp.zeros_like(qin)
        qf = jnp.where(is_f, qin, zero)
        qb = jnp.where(is_f, zero, qin)
        v = v_s[rows, :]
        for hh in range(N_HEADS):
            fb = slice(hh * 2 * DK, (hh + 1) * 2 * DK)
            vs = slice(hh * DV, (hh + 1) * DV)
            a2 = _dot_nt(jnp.concatenate([qf[:, fb], qb[:, fb]], axis=0), kin[:, fb])
            att = (jnp.where(lower, a2[:SUB], 0.0) + jnp.where(upper, a2[SUB:], 0.0)).astype(BF16)
            oin_ref[rows, vs] = _dot(att, v[:, vs]).astype(BF16)
            ko = kout[:, fb]
            kbd = jnp.concatenate([jnp.where(rchunk == c * CHUNK, ko, jnp.zeros_like(ko)) for c in range(CPS)],
                                  axis=1)
            u_ref[s, hh] = _dot_tn(v[:, vs], kbd).astype(BF16)
            g_ref[s, hh] = jnp.concatenate([jnp.exp(t[:, fb]) for t in tot_rows], axis=1)
        return carry

    lax.fori_loop(0, h_ref.shape[0] // SUB, sub_tile, 0, unroll=2)


def _mix_in_kernel(h_ref, w_ref, cw_ref, wb_ref,
                   fx_ref, g0_ref, g2_ref, yp_ref, *, n_ctx_tiles, period_ctx, period_lat):
    i = pl.program_id(0)
    d = g0_ref.shape[-1]
    z = _dot(h_ref[...], w_ref[...])
    sb, sc, sx = z[:, :SC_W], z[:, SC_W:2 * SC_W], z[:, 2 * SC_W:3 * SC_W]
    m0 = 3 * SC_W + FN_W
    u = sc * sx
    tm = h_ref.shape[0]
    period = jnp.where(i < n_ctx_tiles, period_ctx, period_lat)
    pos = lax.broadcasted_iota(jnp.int32, (tm, 1), 0) & (period - 1)
    up = jnp.where(pos == 0, 0.0, pltpu.roll(u, 1, axis=0))
    un = jnp.where(pos == period - 1, 0.0, pltpu.roll(u, tm - 1, axis=0))
    cw = cw_ref[...]
    conv = cw[0:1] * up + cw[1:2] * u + cw[2:3] * un
    yb = _dot((sb * conv).astype(BF16), wb_ref[...])

    _put_rows(fx_ref, 0, z[:, 3 * SC_W:m0])
    g0_ref[...] = _sigmoid(z[:, m0:m0 + d]).astype(BF16)
    yp_ref[...] = (_sigmoid(z[:, m0 + d:m0 + 2 * d]) * yb).astype(BF16)
    g2_ref[...] = _sigmoid(z[:, m0 + 2 * d:m0 + 3 * d]).astype(BF16)


def _scan_kernel(u_ref, g_ref, s0_ref, ss_ref, sfin_ref, *, n_sub):
    lane_f = lax.broadcasted_iota(jnp.int32, (DV, 2 * DK), 1) < DK
    for q in range(s0_ref.shape[0]):
        for hh in range(s0_ref.shape[1]):
            s0 = s0_ref[q, hh]

            def fwd(j, st):
                for c in range(CPS):
                    cs = slice(c * 2 * DK, (c + 1) * 2 * DK)
                    ss_ref[j, hh, :, cs] = st.astype(BF16)
                    st = g_ref[j, hh, :, cs] * st + u_ref[j, hh, :, cs].astype(F32)
                return st

            def bwd(j, st):
                for c in range(CPS - 1, -1, -1):
                    cs = slice(c * 2 * DK, (c + 1) * 2 * DK)
                    ss_ref[j, hh, :, cs] = jnp.where(lane_f, ss_ref[j, hh, :, cs], st.astype(BF16))
                    st = g_ref[j, hh, :, cs] * st + u_ref[j, hh, :, cs].astype(F32)
                return st

            lo = q * n_sub
            sf = lax.fori_loop(0, n_sub, lambda i, st: fwd(lo + i, st), s0)
            sb = lax.fori_loop(0, n_sub, lambda i, st: bwd(lo + n_sub - 1 - i, st), s0)
            sfin_ref[q, hh] = jnp.where(lane_f, sf, sb)


def _scan(u, g, s0, layer, n_seq, n_sub, sub0, seq_blk, head_blk):
    rows = seq_blk * n_sub
    assert n_seq % seq_blk == 0 and N_HEADS % head_blk == 0 and sub0 % rows == 0
    wide = CPS * 2 * DK
    return pl.pallas_call(
        functools.partial(_scan_kernel, n_sub=n_sub),
        grid=(n_seq // seq_blk, N_HEADS // head_blk),
        in_specs=[
            pl.BlockSpec((rows, head_blk, DV, wide), lambda b, hh: (sub0 // rows + b, hh, 0, 0)),
            pl.BlockSpec((rows, head_blk, 1, wide), lambda b, hh: (sub0 // rows + b, hh, 0, 0)),
            pl.BlockSpec((None, seq_blk, head_blk, DV, 2 * DK), lambda b, hh: (layer, b, hh, 0, 0)),
        ],
        out_specs=[
            pl.BlockSpec((rows, head_blk, DV, wide), lambda b, hh: (b, hh, 0, 0)),
            pl.BlockSpec((seq_blk, head_blk, DV, 2 * DK), lambda b, hh: (b, hh, 0, 0)),
        ],
        out_shape=[
            jax.ShapeDtypeStruct((n_seq * n_sub, N_HEADS, DV, wide), BF16),
            jax.ShapeDtypeStruct((n_seq, N_HEADS, DV, 2 * DK), F32),
        ],
        compiler_params=_params(("arbitrary", "arbitrary")),
        name="scan",
    )(u, g, s0)


def _gather_rows(ref, start, size, stride):
    return jnp.concatenate([ref[g, pl.ds(start, size, stride=stride), :] for g in range(FN_GROUPS)], axis=1)


def _put_rows(ref, r0, val):
    for g in range(FN_GROUPS):
        ref[g, r0:r0 + val.shape[0], :] = val[:, g * FN_GW:(g + 1) * FN_GW]


def _channel_dft_real(re_g, im_g, cs):
    return _dot(jnp.concatenate([re_g.astype(BF16), im_g.astype(BF16)], axis=1), cs)


def _dft_direct_kernel(x_ref, m_ref, cs_ref, o_ref):
    seq = x_ref.shape[1]
    x = jnp.concatenate([x_ref[g] for g in range(FN_GROUPS)], axis=1)
    res = _dot(m_ref[...], x.astype(BF16))
    for g in range(FN_GROUPS):
        gs = slice(g * FN_GW, (g + 1) * FN_GW)
        o_ref[:, gs] = _channel_dft_real(res[:seq, gs], res[seq:, gs], cs_ref[...]).astype(BF16)


def _pitch(n):
    return n + 4


def _ct_kernel(x_ref, m1_ref, tc_ref, ts_ref, m2_ref, cs_ref, o_ref, xs_s, yr_s, yi_s, tr_s, ti_s, *, l1, l2):
    j = pl.program_id(1)
    n_stage1 = l2 // CT_NB
    px, pt1, pt2 = _pitch(l2), _pitch(l1), _pitch(l2)

    @pl.when((pl.program_id(0) == 0) & (j == 0))
    def _():
        tr_s[...] = jnp.zeros_like(tr_s)
        ti_s[...] = jnp.zeros_like(ti_s)

    @pl.when(j == 0)
    def _():
        for n1 in range(l1):
            for g in range(FN_GROUPS):
                xs_s[g, n1 * px:n1 * px + l2, :] = x_ref[g, n1 * l2:(n1 + 1) * l2, :]

    @pl.when(j < n_stage1)
    def _():
        m = m1_ref[...]
        col0 = pl.multiple_of(j * CT_NB, CT_NB)
        for jn in range(CT_NB):
            xs = _gather_rows(xs_s, col0 + jn, l1, px)
            res = _dot(m, xs.astype(BF16))
            yr, yi = res[:l1], res[l1:]
            c = jnp.concatenate([tc_ref[col0 + jn]] * FN_GROUPS, axis=1)
            s = jnp.concatenate([ts_ref[col0 + jn]] * FN_GROUPS, axis=1)
            _put_rows(tr_s, jn * pt1, yr * c + yi * s)
            _put_rows(ti_s, jn * pt1, yi * c - yr * s)
        for k1 in range(l1):
            yr_s[k1, pl.ds(col0, CT_NB), :] = _gather_rows(tr_s, k1, CT_NB, pt1).astype(BF16)
            yi_s[k1, pl.ds(col0, CT_NB), :] = _gather_rows(ti_s, k1, CT_NB, pt1).astype(BF16)

    @pl.when(j >= n_stage1)
    def _():
        m = m2_ref[...]
        k0 = (j - n_stage1) * CT_KB
        for kk in range(CT_KB):
            rhs = jnp.concatenate([yr_s[k0 + kk], yi_s[k0 + kk]], axis=0)
            res = _dot(m, rhs)
            _put_rows(tr_s, kk * pt2, res[:l2])
            _put_rows(ti_s, kk * pt2, res[l2:])
        rows = CT_KB * pt2
        for g in range(FN_GROUPS):
            tr_s[g, :rows, :] = _channel_dft_real(tr_s[g, :rows, :], ti_s[g, :rows, :], cs_ref[...])
        for k2 in range(l2):
            o_ref[k2] = _gather_rows(tr_s, k2, CT_KB, pt2).astype(BF16)


def _cos_sin(n_out, n_in, period, scale):
    idx = (np.arange(n_out)[:, None] * np.arange(n_in)[None, :]) % period
    ang = 2.0 * np.pi * idx / period
    return np.cos(ang) * scale, np.sin(ang) * scale


def _mix_out_kernel(*refs, n_ctx_tiles, split_x):
    if split_x:
        xc_ref, xl_ref, *refs = refs
    else:
        x_ref, *refs = refs
    (oin_ref, qin_ref, ssc_ref, ssl_ref, og_ref, frc_ref, frl_ref, g0_ref, g2_ref, yp_ref, mod_ref,
     gn_ref, n2_ref, wa_ref, wc_ref, wo_ref, x1_ref, h2_ref, o_s) = refs
    is_ctx = pl.program_id(0) < n_ctx_tiles
    x = jnp.where(is_ctx, xc_ref[...], xl_ref[...]) if split_x else x_ref[...]
    for s in range(oin_ref.shape[0] // SUB):
        for hh in range(N_HEADS):
            st = jnp.where(is_ctx, ssc_ref[s, hh], ssl_ref[s, hh])
            fb = slice(hh * 2 * DK, (hh + 1) * 2 * DK)
            vs = slice(hh * DV, (hh + 1) * DV)
            for c in range(CPS):
                rows = slice(s * SUB + c * CHUNK, s * SUB + (c + 1) * CHUNK)
                o_s[rows, vs] = oin_ref[rows, vs].astype(F32) + _dot_nt(qin_ref[rows, fb], st[:, c * 2 * DK:(c + 1) * 2 * DK])
    gn = gn_ref[...]
    parts = []
    for hh in range(N_HEADS):
        vs = slice(hh * DV, (hh + 1) * DV)
        parts.append((_rms(o_s[:, vs]) * gn * _silu(og_ref[:, vs].astype(F32))).astype(BF16))
    ya = _dot(jnp.concatenate(parts, axis=1), wa_ref[...])
    yc = _dot(jnp.where(is_ctx, frc_ref[...], frl_ref[...]), wc_ref[...])
    y = g0_ref[...].astype(F32) * ya + yp_ref[...].astype(F32) + g2_ref[...].astype(F32) * yc
    mix = _dot(y.astype(BF16), wo_ref[...])
    m = mod_ref[...]
    x1 = x + m[2:3] * mix
    x1_ref[...] = x1
    h2_ref[...] = (_rms(x1) * n2_ref[...] * (1.0 + m[4:5]) + m[3:4]).astype(BF16)


def _ffn_kernel(h2_ref, x1_ref, wup_ref, wd_ref, mod_ref, modn_ref, nn_ref, *outs, final, n_ctx_tiles):
    h2 = h2_ref[...]
    d_ff = wd_ref.shape[0]
    y = None
    for c0 in range(0, d_ff, FF_CHUNK):
        c1 = min(c0 + FF_CHUNK, d_ff)
        gate = _dot(h2, wup_ref[:, c0:c1])
        up = _dot(h2, wup_ref[:, d_ff + c0:d_ff + c1])
        part = _dot((_silu(gate) * up).astype(BF16), wd_ref[c0:c1, :])
        y = part if y is None else y + part
    x2 = x1_ref[...] + mod_ref[5:6, :] * y
    if final:
        yc_ref, yl_ref = outs
        out = _rms(x2) * nn_ref[...]
        is_ctx = pl.program_id(0) < n_ctx_tiles

        @pl.when(is_ctx)
        def _():
            yc_ref[...] = out

        @pl.when(jnp.logical_not(is_ctx))
        def _():
            yl_ref[...] = out
    else:
        x2_ref, hn_ref = outs
        x2_ref[...] = x2
        hn_ref[...] = (_rms(x2) * nn_ref[...] * (1.0 + modn_ref[1:2, :]) + modn_ref[0:1, :]).astype(BF16)


def kernel(x_prompt, x_sample, state_gla, c, c_ctx, w_ada, b_ada, norm1, norm2, w_in, w_gk_f, b_gk_f,
           w_gk_b, b_gk_b, gla_norm, w_a_out, conv_w, w_b_out, w_c_out, w_o, w_up, w_down, norm_f):
    b_ctx, seq, d = x_prompt.shape
    b_lat, dec_seq, _ = x_sample.shape
    depth = w_ada.shape[0]
    d_ff = w_down.shape[1]
    nc, nl = b_ctx * seq, b_lat * dec_seq
    nt = nc + nl
    l1, l2 = dec_seq // GRID_W, GRID_W
    assert seq == SUB and d_ff % 128 == 0
    assert nc % dec_seq == 0 and l2 % CT_NB == 0 and l1 % CT_KB == 0
    n_sub_tot, n_sub_lat = nt // SUB, dec_seq // SUB
    assert (nc // SUB) % n_sub_lat == 0
    n_cond = -(-(1 + b_lat) // 8) * 8

    class Tiling:
        def __init__(self, tm):
            assert nc % tm == 0 and dec_seq % tm == 0
            self.tm, self.n, self.n_ctx, self.per_lat = tm, nt // tm, nc // tm, dec_seq // tm

        def cond(self, i):
            return jnp.where(i < self.n_ctx, 0, 1 + (i - self.n_ctx) // self.per_lat)

        def ctx_blk(self, i):
            return jnp.minimum(i, self.n_ctx - 1)

        def lat_blk(self, i):
            return jnp.maximum(i - self.n_ctx, 0)

        def tok(self, w):
            return pl.BlockSpec((self.tm, w), lambda i, *_: (i, 0))

        def tok_ctx(self, w):
            return pl.BlockSpec((self.tm, w), lambda i, *_: (self.ctx_blk(i), 0))

        def tok_lat(self, w):
            return pl.BlockSpec((self.tm, w), lambda i, *_: (self.lat_blk(i), 0))

        def mod(self, layer):
            return pl.BlockSpec((None, None, N_MOD, d), lambda i, *_: (layer, self.cond(i), 0, 0))

    t_gla, t_mix, t_out, t_ffn = Tiling(TM_GLA), Tiling(TM_MIX), Tiling(TM_OUT), Tiling(TM_FFN)

    cond = jnp.concatenate([c_ctx[None, :], c, jnp.zeros((n_cond - 1 - b_lat, d), F32)], axis=0)
    mods = _ada(cond, w_ada, b_ada).reshape(depth, n_cond, N_MOD, d)

    cc, sc_ = _cos_sin(FN_GW, FN_GW, FN_GW, FN_GW ** -0.5)
    cs_tab = jnp.asarray(np.concatenate([cc, sc_], axis=0), F32).astype(BF16)
    cl, sl = _cos_sin(seq, seq, seq, seq ** -0.5)
    m_ctx_tab = jnp.asarray(np.concatenate([cl, -sl], axis=0), F32).astype(BF16)
    c1, s1 = _cos_sin(l1, l1, l1, l1 ** -0.5)
    m1_tab = jnp.asarray(np.concatenate([c1, -s1], axis=0), F32).astype(BF16)
    tcn, tsn = _cos_sin(l2, l1, l1 * l2, 1.0)
    tc_tab = jnp.asarray(np.repeat(tcn[:, :, None], 128, axis=2), F32)
    ts_tab = jnp.asarray(np.repeat(tsn[:, :, None], 128, axis=2), F32)
    c2, s2 = _cos_sin(l2, l2, l2, l2 ** -0.5)
    m2_tab = jnp.asarray(np.block([[c2, s2], [-s2, c2]]), F32).astype(BF16)

    xc, xl = x_prompt.reshape(nc, d), x_sample.reshape(nl, d)
    x = None

    gla_cols = 2 * N_HEADS * DK + 2 * V_W
    mix_col0 = gla_cols + 2 * LOWRANK
    w_gla, w_mix = _split_w_in(w_in, gla_cols + GK_PAD, w_in.shape[-1] - mix_col0)
    zf = jnp.zeros((depth, LOWRANK, N_HEADS, DK), F32)
    top = jnp.stack([w_gk_f.reshape(depth, LOWRANK, N_HEADS, DK), zf], axis=3).reshape(depth, LOWRANK, FB_W)
    bot = jnp.stack([zf, w_gk_b.reshape(depth, LOWRANK, N_HEADS, DK)], axis=3).reshape(depth, LOWRANK, FB_W)
    wgk2 = jnp.concatenate([top, bot, jnp.zeros((depth, GK_PAD - 2 * LOWRANK, FB_W), F32)], axis=1).astype(BF16)
    bgk = jnp.stack([b_gk_f.reshape(depth, N_HEADS, DK), b_gk_b.reshape(depth, N_HEADS, DK)],
                    axis=2).reshape(depth, 1, FB_W)
    wb, wa, wc_out, wo = (t.astype(BF16) for t in (w_b_out, w_a_out, w_c_out, w_o))
    wup, wdn = w_up.astype(BF16), w_down.astype(BF16)
    n1, n2, gn = norm1[:, None, :], norm2[:, None, :], gla_norm[:, None, :]
    s0_ctx = jnp.zeros((1, b_ctx, N_HEADS, DV, 2 * DK), F32)
    s0_lat = state_gla.transpose(1, 0, 3, 5, 2, 4).reshape(depth, b_lat, N_HEADS, DV, 2 * DK)

    h = pl.pallas_call(
        functools.partial(_prenorm_kernel, n_ctx_tiles=t_ffn.n_ctx),
        grid=(t_ffn.n,),
        in_specs=[t_ffn.tok_ctx(d), t_ffn.tok_lat(d), t_ffn.mod(0), _layer_spec(0, (1, d))],
        out_specs=t_ffn.tok(d),
        out_shape=jax.ShapeDtypeStruct((nt, d), BF16),
        compiler_params=_params(("arbitrary",)),
        name="prenorm",
    )(xc, xl, mods, n1)

    new_states = []
    y_out = None
    for l in range(depth):
        tok = t_gla.tok
        og, qin, oin, u, g = pl.pallas_call(
            _gla_in_kernel,
            grid=(t_gla.n,),
            in_specs=[tok(d), _layer_spec(l, (d, gla_cols + GK_PAD)), _layer_spec(l, (GK_PAD, FB_W)),
                      _layer_spec(l, (1, FB_W))],
            out_specs=[tok(V_W), tok(FB_W), tok(V_W),
                       pl.BlockSpec((TM_GLA // SUB, N_HEADS, DV, CPS * 2 * DK), lambda i: (i, 0, 0, 0)),
                       pl.BlockSpec((TM_GLA // SUB, N_HEADS, 1, CPS * 2 * DK), lambda i: (i, 0, 0, 0))],
            out_shape=[jax.ShapeDtypeStruct((nt, V_W), BF16), jax.ShapeDtypeStruct((nt, FB_W), BF16),
                       jax.ShapeDtypeStruct((nt, V_W), BF16),
                       jax.ShapeDtypeStruct((n_sub_tot, N_HEADS, DV, CPS * 2 * DK), BF16),
                       jax.ShapeDtypeStruct((n_sub_tot, N_HEADS, 1, CPS * 2 * DK), F32)],
            scratch_shapes=[pltpu.VMEM((TM_GLA, FB_W), F32), pltpu.VMEM((TM_GLA, FB_W), F32),
                            pltpu.VMEM((TM_GLA, V_W), BF16), pltpu.VMEM((TM_GLA, FB_W), F32)],
            compiler_params=_params(("arbitrary",)),
            name="gla_in",
        )(h, w_gla, wgk2, bgk)

        tok = t_mix.tok
        fx, g0, g2, yp = pl.pallas_call(
            functools.partial(_mix_in_kernel, n_ctx_tiles=t_mix.n_ctx, period_ctx=seq, period_lat=GRID_W),
            grid=(t_mix.n,),
            in_specs=[tok(d), _layer_spec(l, (d, 3 * SC_W + FN_W + 3 * d)),
                      _layer_spec(l, (3, SC_W)), _layer_spec(l, (SC_W, d))],
            out_specs=[pl.BlockSpec((FN_GROUPS, TM_MIX, FN_GW), lambda i: (0, i, 0)), tok(d), tok(d), tok(d)],
            out_shape=[jax.ShapeDtypeStruct((FN_GROUPS, nt, FN_GW), F32)] + [jax.ShapeDtypeStruct((nt, d), BF16)] * 3,
            compiler_params=_params(("arbitrary",)),
            name="mix_in",
        )(h, w_mix, conv_w, wb)

        ss_ctx, sfin = _scan(u, g, s0_ctx, 0, b_ctx, seq // SUB, 0, math.gcd(b_ctx, 4), N_HEADS)
        ss_lat, _ = _scan(u, g, s0_lat, l, b_lat, n_sub_lat, nc // SUB, 1, 1)
        new_states.append(sfin)

        cs_spec = _const_spec((2 * FN_GW, FN_GW))
        fr_ctx = pl.pallas_call(
            _dft_direct_kernel,
            grid=(b_ctx,),
            in_specs=[pl.BlockSpec((FN_GROUPS, seq, FN_GW), lambda s: (0, s, 0)), _const_spec((2 * seq, seq)),
                      cs_spec],
            out_specs=pl.BlockSpec((seq, FN_W), lambda s: (s, 0)),
            out_shape=jax.ShapeDtypeStruct((nc, FN_W), BF16),
            compiler_params=_params(("arbitrary",)),
            name="dft_ctx",
        )(fx, m_ctx_tab, cs_tab)
        lat_blk0 = nc // dec_seq
        n_st1, n_st2 = l2 // CT_NB, l1 // CT_KB
        t_rows = max(CT_NB * _pitch(l1), CT_KB * _pitch(l2))
        fr_lat = pl.pallas_call(
            functools.partial(_ct_kernel, l1=l1, l2=l2),
            grid=(b_lat, n_st1 + n_st2),
            in_specs=[pl.BlockSpec((FN_GROUPS, dec_seq, FN_GW), lambda b, j: (0, lat_blk0 + b, 0)),
                      _const_spec((2 * l1, l1)), _const_spec((l2, l1, 128)), _const_spec((l2, l1, 128)),
                      _const_spec((2 * l2, 2 * l2)), cs_spec],
            out_specs=pl.BlockSpec((None, l2, CT_KB, FN_W), lambda b, j: (b, 0, jnp.maximum(j - n_st1, 0), 0)),
            out_shape=jax.ShapeDtypeStruct((b_lat, l2, l1, FN_W), BF16),
            scratch_shapes=[pltpu.VMEM((FN_GROUPS, l1 * _pitch(l2), FN_GW), F32),
                            pltpu.VMEM((l1, l2, FN_W), BF16), pltpu.VMEM((l1, l2, FN_W), BF16),
                            pltpu.VMEM((FN_GROUPS, t_rows, FN_GW), F32), pltpu.VMEM((FN_GROUPS, t_rows, FN_GW), F32)],
            compiler_params=_params(("arbitrary", "arbitrary")),
            name="dft_lat",
        )(fx, m1_tab, tc_tab, ts_tab, m2_tab, cs_tab).reshape(nl, FN_W)

        tok = t_out.tok
        split_x = x is None
        ss_spec = lambda blk: pl.BlockSpec((TM_OUT // SUB, N_HEADS, DV, CPS * 2 * DK),
                                           lambda i: (blk(i), 0, 0, 0))
        x1, h2 = pl.pallas_call(
            functools.partial(_mix_out_kernel, n_ctx_tiles=t_out.n_ctx, split_x=split_x),
            grid=(t_out.n,),
            in_specs=([t_out.tok_ctx(d), t_out.tok_lat(d)] if split_x else [tok(d)]) + [
                tok(V_W), tok(FB_W), ss_spec(t_out.ctx_blk), ss_spec(t_out.lat_blk), tok(V_W),
                t_out.tok_ctx(FN_W), t_out.tok_lat(FN_W), tok(d), tok(d), tok(d), t_out.mod(l),
                _layer_spec(l, (1, DV)), _layer_spec(l, (1, d)),
                _layer_spec(l, (V_W, d)), _layer_spec(l, (FN_W, d)), _layer_spec(l, (d, d))],
            out_specs=[tok(d), tok(d)],
            out_shape=[jax.ShapeDtypeStruct((nt, d), F32), jax.ShapeDtypeStruct((nt, d), BF16)],
            scratch_shapes=[pltpu.VMEM((TM_OUT, V_W), F32)],
            compiler_params=_params(("arbitrary",)),
            name="mix_out",
        )(*((xc, xl) if split_x else (x,)), oin, qin, ss_ctx, ss_lat, og, fr_ctx, fr_lat, g0, g2, yp, mods,
          gn, n2, wa, wc_out, wo)

        final = l == depth - 1
        tok = t_ffn.tok
        nxt = depth - 1 if final else l + 1
        outs = pl.pallas_call(
            functools.partial(_ffn_kernel, final=final, n_ctx_tiles=t_ffn.n_ctx),
            grid=(t_ffn.n,),
            in_specs=[tok(d), tok(d), _layer_spec(l, (d, 2 * d_ff)), _layer_spec(l, (d_ff, d)),
                      t_ffn.mod(l), t_ffn.mod(nxt), _const_spec((1, d)) if final else _layer_spec(nxt, (1, d))],
            out_specs=[t_ffn.tok_ctx(d), t_ffn.tok_lat(d)] if final else [tok(d), tok(d)],
            out_shape=([jax.ShapeDtypeStruct((nc, d), F32), jax.ShapeDtypeStruct((nl, d), F32)] if final else
                       [jax.ShapeDtypeStruct((nt, d), F32), jax.ShapeDtypeStruct((nt, d), BF16)]),
            compiler_params=_params(("arbitrary",)),
            name="ffn",
        )(h2, x1, wup, wdn, mods, mods, norm_f[None, :] if final else n1)
        if final:
            y_out = outs
        else:
            x, h = outs

    y_prompt = y_out[0].reshape(b_ctx, seq, d)
    y_sample = y_out[1].reshape(b_lat, dec_seq, d)
    sfin_all = jnp.stack(new_states, axis=0).reshape(depth, b_ctx, N_HEADS, DV, 2, DK)
    new_state_gla = sfin_all.transpose(1, 0, 4, 2, 5, 3).astype(x_prompt.dtype)
    return (y_prompt, y_sample, new_state_gla)
```

```python
import functools
import math

import numpy as np
import jax
import jax.numpy as jnp
from jax import lax
from jax.experimental import pallas as pl
from jax.experimental.pallas import tpu as pltpu

F32 = jnp.float32
BF16 = jnp.bfloat16

GRID_W = 64
N_HEADS = 4
DK = 64
DV = 128
V_W = N_HEADS * DV
FB_W = N_HEADS * 2 * DK
LOWRANK = 16
GATE_NORMALIZER = 16.0
CHUNK = 64
SUB = 256
CPS = SUB // CHUNK
SC_W = 512
FN_GROUPS = 4
FN_GW = 128
FN_W = FN_GROUPS * FN_GW
N_MOD = 6
EPS = 1e-6
GK_PAD = 128

TM_GLA = 1024
TM_MIX = 1024
TM_OUT = 512
TM_FFN = 512
TM_PRE = 2048
FF_CHUNK = 512
CT_NB = 16
CT_KB = 16
VMEM_LIMIT = 56 * 1024 * 1024


def _dot(a, b):
    return jnp.dot(a, b, preferred_element_type=F32)


def _dot_nt(a, b):
    return lax.dot_general(a, b, (((1,), (1,)), ((), ())), preferred_element_type=F32)


def _dot_tn(a, b):
    return lax.dot_general(a, b, (((0,), (0,)), ((), ())), preferred_element_type=F32)


def _sigmoid(x):
    return 1.0 / (1.0 + jnp.exp(-x))


def _silu(x):
    return x * _sigmoid(x)


def _rms(x):
    return x * lax.rsqrt(jnp.mean(x * x, axis=-1, keepdims=True) + EPS)


def _params(sem):
    return pltpu.CompilerParams(dimension_semantics=sem, vmem_limit_bytes=VMEM_LIMIT)


def _const_spec(shape):
    nd = len(shape)
    return pl.BlockSpec(shape, lambda *_: (0,) * nd, pipeline_mode=pl.Buffered(1))


def _layer_spec(layer, shape):
    nd = len(shape)
    return pl.BlockSpec((None,) + tuple(shape), lambda *_: (layer,) + (0,) * nd, pipeline_mode=pl.Buffered(1))


def _ada_kernel(c_ref, w_ref, b_ref, o_ref):
    s = _silu(c_ref[...]).astype(BF16)
    o_ref[...] = _dot(s, w_ref[...].astype(BF16)) + b_ref[...]


def _ada(cond, w_ada, b_ada):
    depth, d, n = w_ada.shape
    rows = cond.shape[0]
    nb = 1536
    return pl.pallas_call(
        _ada_kernel,
        grid=(depth, n // nb),
        in_specs=[
            pl.BlockSpec((rows, d), lambda l, j: (0, 0)),
            pl.BlockSpec((None, d, nb), lambda l, j: (l, 0, j)),
            pl.BlockSpec((None, 1, nb), lambda l, j: (l, 0, j)),
        ],
        out_specs=pl.BlockSpec((None, rows, nb), lambda l, j: (l, 0, j)),
        out_shape=jax.ShapeDtypeStruct((depth, rows, n), F32),
        compiler_params=_params(("arbitrary", "arbitrary")),
        name="ada",
    )(cond, w_ada, b_ada.reshape(depth, 1, n))


def _prenorm_kernel(xc_ref, xl_ref, mod_ref, n_ref, h_ref, *, n_ctx_tiles):
    x = jnp.where(pl.program_id(0) < n_ctx_tiles, xc_ref[...], xl_ref[...])
    m = mod_ref[...]
    h = _rms(x) * n_ref[...] * (1.0 + m[1:2]) + m[0:1]
    h_ref[...] = h.astype(BF16)


def _gla_in_kernel(h_ref, w_ref, wgk2_ref, bgk_ref,
                   og_ref, qin_ref, oin_ref, u_ref, g_ref,
                   q_s, k_s, v_s, la_s):
    qk_w = N_HEADS * DK
    z = _dot(h_ref[...], w_ref[...])
    lo_half = lax.broadcasted_iota(jnp.int32, (1, 2 * DK), 1) < DK

    def both_directions(x):
        tiles = []
        for p in range(N_HEADS // 2):
            t = x[:, p * 2 * DK:(p + 1) * 2 * DK]
            r = pltpu.roll(t, DK, axis=1)
            tiles += [jnp.where(lo_half, t, r), jnp.where(lo_half, r, t)]
        return jnp.concatenate(tiles, axis=1)

    q_s[...] = both_directions(z[:, :qk_w]) * (DK ** -0.5)
    k_s[...] = both_directions(z[:, qk_w:2 * qk_w])
    v_s[...] = z[:, 2 * qk_w:2 * qk_w + V_W].astype(BF16)
    og_ref[...] = z[:, 2 * qk_w + V_W:2 * qk_w + 2 * V_W].astype(BF16)
    gk = z[:, 2 * qk_w + 2 * V_W:].astype(BF16)
    lp = _dot(gk, wgk2_ref[...]) + bgk_ref[...]
    la_s[...] = (jnp.minimum(lp, 0.0) - jnp.log(1.0 + jnp.exp(-jnp.abs(lp)))) * (1.0 / GATE_NORMALIZER)

    row = lax.broadcasted_iota(jnp.int32, (SUB, SUB), 0)
    col = lax.broadcasted_iota(jnp.int32, (SUB, SUB), 1)
    same = (row & -CHUNK) == (col & -CHUNK)
    lower = same & (col <= row)
    upper = same & (col >= row)
    tri = jnp.where(lower, 1.0, 0.0).astype(BF16)
    is_f = (lax.broadcasted_iota(jnp.int32, (SUB, FB_W), 1) & (2 * DK - 1)) < DK
    rchunk = lax.broadcasted_iota(jnp.int32, (SUB, 2 * DK), 0) & -CHUNK

    def sub_tile(s, carry):
        r0 = pl.multiple_of(s * SUB, SUB)
        rows = pl.ds(r0, SUB)
        la = la_s[rows, :]
        hi = la.astype(BF16)
        r1 = la - hi.astype(F32)
        mid = r1.astype(BF16)
        lo = (r1 - mid.astype(F32)).astype(BF16)
        pre = _dot(tri, hi) + _dot(tri, mid) + _dot(tri, lo)
        tot_rows = [pre[c * CHUNK + CHUNK - 1:c * CHUNK + CHUNK, :] for c in range(CPS)]
        tot = jnp.concatenate([jnp.broadcast_to(t, (CHUNK, FB_W)) for t in tot_rows], axis=0)
        b = jnp.where(is_f, pre, tot - pre + la)
        q = q_s[rows, :]
        k = k_s[rows, :]
        qin = (q * jnp.exp(b)).astype(BF16)
        kin = (k * jnp.exp(-b)).astype(BF16)
        kout = (k * jnp.exp(tot - b)).astype(BF16)
        qin_ref[rows, :] = qin
        zero = jnp.zeros_like(qin)
        qf = jnp.where(is_f, qin, zero)
        qb = jnp.where(is_f, zero, qin)
        v = v_s[rows, :]
        for hh in range(N_HEADS):
            fb = slice(hh * 2 * DK, (hh + 1) * 2 * DK)
            vs = slice(hh * DV, (hh + 1) * DV)
            a2 = _dot_nt(jnp.concatenate([qf[:, fb], qb[:, fb]], axis=0), kin[:, fb])
            att = (jnp.where(lower, a2[:SUB], 0.0) + jnp.where(upper, a2[SUB:], 0.0)).astype(BF16)
            oin_ref[rows, vs] = _dot(att, v[:, vs]).astype(BF16)
            ko = kout[:, fb]
            kbd = jnp.concatenate([jnp.where(rchunk == c * CHUNK, ko, jnp.zeros_like(ko)) for c in range(CPS)],
                                  axis=1)
            u_ref[s, hh] = _dot_tn(v[:, vs], kbd).astype(BF16)
            g_ref[s, hh] = jnp.concatenate([jnp.exp(t[:, fb]) for t in tot_rows], axis=1)
        return carry

    lax.fori_loop(0, h_ref.shape[0] // SUB, sub_tile, 0, unroll=2)


def _mix_in_kernel(h_ref, w_ref, cw_ref, wb_ref,
                   fx_ref, g0_ref, g2_ref, yp_ref, *, n_ctx_tiles, period_ctx, period_lat):
    i = pl.program_id(0)
    d = g0_ref.shape[-1]
    z = _dot(h_ref[...], w_ref[...])
    sb, sc, sx = z[:, :SC_W], z[:, SC_W:2 * SC_W], z[:, 2 * SC_W:3 * SC_W]
    m0 = 3 * SC_W + FN_W
    u = sc * sx
    tm = h_ref.shape[0]
    period = jnp.where(i < n_ctx_tiles, period_ctx, period_lat)
    pos = lax.broadcasted_iota(jnp.int32, (tm, 1), 0) & (period - 1)
    up = jnp.where(pos == 0, 0.0, pltpu.roll(u, 1, axis=0))
    un = jnp.where(pos == period - 1, 0.0, pltpu.roll(u, tm - 1, axis=0))
    cw = cw_ref[...]
    conv = cw[0:1] * up + cw[1:2] * u + cw[2:3] * un
    yb = _dot((sb * conv).astype(BF16), wb_ref[...])

    _put_rows(fx_ref, 0, z[:, 3 * SC_W:m0])
    g0_ref[...] = _sigmoid(z[:, m0:m0 + d]).astype(BF16)
    yp_ref[...] = (_sigmoid(z[:, m0 + d:m0 + 2 * d]) * yb).astype(BF16)
    g2_ref[...] = _sigmoid(z[:, m0 + 2 * d:m0 + 3 * d]).astype(BF16)


def _scan_kernel(u_ref, g_ref, s0_ref, ss_ref, sfin_ref, *, n_sub):
    lane_f = lax.broadcasted_iota(jnp.int32, (DV, 2 * DK), 1) < DK
    for q in range(s0_ref.shape[0]):
        for hh in range(s0_ref.shape[1]):
            s0 = s0_ref[q, hh]

            def fwd(j, st):
                for c in range(CPS):
                    cs = slice(c * 2 * DK, (c + 1) * 2 * DK)
                    ss_ref[j, hh, :, cs] = st.astype(BF16)
                    st = g_ref[j, hh, :, cs] * st + u_ref[j, hh, :, cs].astype(F32)
                return st

            def bwd(j, st):
                for c in range(CPS - 1, -1, -1):
                    cs = slice(c * 2 * DK, (c + 1) * 2 * DK)
                    ss_ref[j, hh, :, cs] = jnp.where(lane_f, ss_ref[j, hh, :, cs], st.astype(BF16))
                    st = g_ref[j, hh, :, cs] * st + u_ref[j, hh, :, cs].astype(F32)
                return st

            lo = q * n_sub
            sf = lax.fori_loop(0, n_sub, lambda i, st: fwd(lo + i, st), s0)
            sb = lax.fori_loop(0, n_sub, lambda i, st: bwd(lo + n_sub - 1 - i, st), s0)
            sfin_ref[q, hh] = jnp.where(lane_f, sf, sb)


def _scan(u, g, s0, layer, n_seq, n_sub, sub0, seq_blk, head_blk):
    rows = seq_blk * n_sub
    assert n_seq % seq_blk == 0 and N_HEADS % head_blk == 0 and sub0 % rows == 0
    wide = CPS * 2 * DK
    return pl.pallas_call(
        functools.partial(_scan_kernel, n_sub=n_sub),
        grid=(n_seq // seq_blk, N_HEADS // head_blk),
        in_specs=[
            pl.BlockSpec((rows, head_blk, DV, wide), lambda b, hh: (sub0 // rows + b, hh, 0, 0)),
            pl.BlockSpec((rows, head_blk, 1, wide), lambda b, hh: (sub0 // rows + b, hh, 0, 0)),
            pl.BlockSpec((None, seq_blk, head_blk, DV, 2 * DK), lambda b, hh: (layer, b, hh, 0, 0)),
        ],
        out_specs=[
            pl.BlockSpec((rows, head_blk, DV, wide), lambda b, hh: (b, hh, 0, 0)),
            pl.BlockSpec((seq_blk, head_blk, DV, 2 * DK), lambda b, hh: (b, hh, 0, 0)),
        ],
        out_shape=[
            jax.ShapeDtypeStruct((n_seq * n_sub, N_HEADS, DV, wide), BF16),
            jax.ShapeDtypeStruct((n_seq, N_HEADS, DV, 2 * DK), F32),
        ],
        compiler_params=_params(("arbitrary", "arbitrary")),
        name="scan",
    )(u, g, s0)


def _gather_rows(ref, start, size, stride):
    return jnp.concatenate([ref[g, pl.ds(start, size, stride=stride), :] for g in range(FN_GROUPS)], axis=1)


def _put_rows(ref, r0, val):
    for g in range(FN_GROUPS):
        ref[g, r0:r0 + val.shape[0], :] = val[:, g * FN_GW:(g + 1) * FN_GW]


def _channel_dft_real(re_g, im_g, cs):
    return _dot(jnp.concatenate([re_g.astype(BF16), im_g.astype(BF16)], axis=1), cs)


def _dft_direct_kernel(x_ref, m_ref, cs_ref, o_ref, *, seq):
    for q in range(x_ref.shape[1] // seq):
        rows = slice(q * seq, (q + 1) * seq)
        x = jnp.concatenate([x_ref[g, rows, :] for g in range(FN_GROUPS)], axis=1)
        res = _dot(m_ref[...], x.astype(BF16))
        for g in range(FN_GROUPS):
            gs = slice(g * FN_GW, (g + 1) * FN_GW)
            o_ref[rows, gs] = _channel_dft_real(res[:seq, gs], res[seq:, gs], cs_ref[...]).astype(BF16)


def _pitch(n):
    return n + 4


def _ct_kernel(x_ref, m1_ref, tc_ref, ts_ref, m2_ref, cs_ref, o_ref, xs_s, yr_s, yi_s, tr_s, ti_s, *, l1, l2):
    j = pl.program_id(1)
    n_stage1 = l2 // CT_NB
    px, pt1, pt2 = _pitch(l2), _pitch(l1), _pitch(l2)

    @pl.when((pl.program_id(0) == 0) & (j == 0))
    def _():
        tr_s[...] = jnp.zeros_like(tr_s)
        ti_s[...] = jnp.zeros_like(ti_s)

    @pl.when(j == 0)
    def _():
        for n1 in range(l1):
            for g in range(FN_GROUPS):
                xs_s[g, n1 * px:n1 * px + l2, :] = x_ref[g, n1 * l2:(n1 + 1) * l2, :]

    @pl.when(j < n_stage1)
    def _():
        m = m1_ref[...]
        col0 = pl.multiple_of(j * CT_NB, CT_NB)
        for jn in range(CT_NB):
            xs = _gather_rows(xs_s, col0 + jn, l1, px)
            res = _dot(m, xs.astype(BF16))
            yr, yi = res[:l1], res[l1:]
            c = jnp.concatenate([tc_ref[col0 + jn]] * FN_GROUPS, axis=1)
            s = jnp.concatenate([ts_ref[col0 + jn]] * FN_GROUPS, axis=1)
            _put_rows(tr_s, jn * pt1, yr * c + yi * s)
            _put_rows(ti_s, jn * pt1, yi * c - yr * s)
        for k1 in range(l1):
            yr_s[k1, pl.ds(col0, CT_NB), :] = _gather_rows(tr_s, k1, CT_NB, pt1).astype(BF16)
            yi_s[k1, pl.ds(col0, CT_NB), :] = _gather_rows(ti_s, k1, CT_NB, pt1).astype(BF16)

    @pl.when(j >= n_stage1)
    def _():
        m = m2_ref[...]
        k0 = (j - n_stage1) * CT_KB
        for kk in range(CT_KB):
            rhs = jnp.concatenate([yr_s[k0 + kk], yi_s[k0 + kk]], axis=0)
            res = _dot(m, rhs)
            _put_rows(tr_s, kk * pt2, res[:l2])
            _put_rows(ti_s, kk * pt2, res[l2:])
        rows = CT_KB * pt2
        for g in range(FN_GROUPS):
            tr_s[g, :rows, :] = _channel_dft_real(tr_s[g, :rows, :], ti_s[g, :rows, :], cs_ref[...])
        for k2 in range(l2):
            o_ref[k2] = _gather_rows(tr_s, k2, CT_KB, pt2).astype(BF16)


def _cos_sin(n_out, n_in, period, scale):
    idx = (np.arange(n_out)[:, None] * np.arange(n_in)[None, :]) % period
    ang = 2.0 * np.pi * idx / period
    return np.cos(ang) * scale, np.sin(ang) * scale


def _mix_out_kernel(*refs, n_ctx_tiles, split_x):
    if split_x:
        xc_ref, xl_ref, *refs = refs
    else:
        x_ref, *refs = refs
    (oin_ref, qin_ref, ssc_ref, ssl_ref, og_ref, frc_ref, frl_ref, g0_ref, g2_ref, yp_ref, mod_ref,
     gn_ref, n2_ref, wa_ref, wc_ref, wo_ref, x1_ref, h2_ref, o_s) = refs
    is_ctx = pl.program_id(0) < n_ctx_tiles
    x = jnp.where(is_ctx, xc_ref[...], xl_ref[...]) if split_x else x_ref[...]
    for s in range(oin_ref.shape[0] // SUB):
        for hh in range(N_HEADS):
            st = jnp.where(is_ctx, ssc_ref[s, hh], ssl_ref[s, hh])
            fb = slice(hh * 2 * DK, (hh + 1) * 2 * DK)
            vs = slice(hh * DV, (hh + 1) * DV)
            for c in range(CPS):
                rows = slice(s * SUB + c * CHUNK, s * SUB + (c + 1) * CHUNK)
                o_s[rows, vs] = oin_ref[rows, vs].astype(F32) + _dot_nt(qin_ref[rows, fb], st[:, c * 2 * DK:(c + 1) * 2 * DK])
    gn = gn_ref[...]
    parts = []
    for hh in range(N_HEADS):
        vs = slice(hh * DV, (hh + 1) * DV)
        parts.append((_rms(o_s[:, vs]) * gn * _silu(og_ref[:, vs].astype(F32))).astype(BF16))
    ya = _dot(jnp.concatenate(parts, axis=1), wa_ref[...])
    yc = _dot(jnp.where(is_ctx, frc_ref[...], frl_ref[...]), wc_ref[...])
    y = g0_ref[...].astype(F32) * ya + yp_ref[...].astype(F32) + g2_ref[...].astype(F32) * yc
    mix = _dot(y.astype(BF16), wo_ref[...])
    m = mod_ref[...]
    x1 = x + m[2:3] * mix
    x1_ref[...] = x1
    h2_ref[...] = (_rms(x1) * n2_ref[...] * (1.0 + m[4:5]) + m[3:4]).astype(BF16)


def _ffn_kernel(h2_ref, x1_ref, wup_ref, wd_ref, mod_ref, modn_ref, nn_ref, *outs, final, n_ctx_tiles):
    h2 = h2_ref[...]
    d_ff = wd_ref.shape[0]
    y = None
    for c0 in range(0, d_ff, FF_CHUNK):
        c1 = min(c0 + FF_CHUNK, d_ff)
        gate = _dot(h2, wup_ref[:, c0:c1])
        up = _dot(h2, wup_ref[:, d_ff + c0:d_ff + c1])
        part = _dot((_silu(gate) * up).astype(BF16), wd_ref[c0:c1, :])
        y = part if y is None else y + part
    x2 = x1_ref[...] + mod_ref[5:6, :] * y
    if final:
        yc_ref, yl_ref = outs
        out = _rms(x2) * nn_ref[...]
        is_ctx = pl.program_id(0) < n_ctx_tiles

        @pl.when(is_ctx)
        def _():
            yc_ref[...] = out

        @pl.when(jnp.logical_not(is_ctx))
        def _():
            yl_ref[...] = out
    else:
        x2_ref, hn_ref = outs
        x2_ref[...] = x2
        hn_ref[...] = (_rms(x2) * nn_ref[...] * (1.0 + modn_ref[1:2, :]) + modn_ref[0:1, :]).astype(BF16)


def kernel(x_prompt, x_sample, state_gla, c, c_ctx, w_ada, b_ada, norm1, norm2, w_in, w_gk_f, b_gk_f,
           w_gk_b, b_gk_b, gla_norm, w_a_out, conv_w, w_b_out, w_c_out, w_o, w_up, w_down, norm_f):
    b_ctx, seq, d = x_prompt.shape
    b_lat, dec_seq, _ = x_sample.shape
    depth = w_ada.shape[0]
    d_ff = w_down.shape[1]
    nc, nl = b_ctx * seq, b_lat * dec_seq
    nt = nc + nl
    l1, l2 = dec_seq // GRID_W, GRID_W
    assert seq == SUB and d_ff % 128 == 0
    assert nc % dec_seq == 0 and l2 % CT_NB == 0 and l1 % CT_KB == 0
    n_sub_tot, n_sub_lat = nt // SUB, dec_seq // SUB
    assert (nc // SUB) % n_sub_lat == 0
    n_cond = -(-(1 + b_lat) // 8) * 8

    class Tiling:
        def __init__(self, tm):
            assert nc % tm == 0 and dec_seq % tm == 0
            self.tm, self.n, self.n_ctx, self.per_lat = tm, nt // tm, nc // tm, dec_seq // tm

        def cond(self, i):
            return jnp.where(i < self.n_ctx, 0, 1 + (i - self.n_ctx) // self.per_lat)

        def ctx_blk(self, i):
            return jnp.minimum(i, self.n_ctx - 1)

        def lat_blk(self, i):
            return jnp.maximum(i - self.n_ctx, 0)

        def tok(self, w):
            return pl.BlockSpec((self.tm, w), lambda i, *_: (i, 0))

        def tok_ctx(self, w):
            return pl.BlockSpec((self.tm, w), lambda i, *_: (self.ctx_blk(i), 0))

        def tok_lat(self, w):
            return pl.BlockSpec((self.tm, w), lambda i, *_: (self.lat_blk(i), 0))

        def mod(self, layer):
            return pl.BlockSpec((None, None, N_MOD, d), lambda i, *_: (layer, self.cond(i), 0, 0))

    t_gla, t_mix, t_out, t_ffn = Tiling(TM_GLA), Tiling(TM_MIX), Tiling(TM_OUT), Tiling(TM_FFN)
    t_pre = Tiling(math.gcd(math.gcd(nc, dec_seq), TM_PRE))

    cond = jnp.concatenate([c_ctx[None, :], c, jnp.zeros((n_cond - 1 - b_lat, d), F32)], axis=0)
    mods = _ada(cond, w_ada, b_ada).reshape(depth, n_cond, N_MOD, d)

    cc, sc_ = _cos_sin(FN_GW, FN_GW, FN_GW, FN_GW ** -0.5)
    cs_tab = jnp.asarray(np.concatenate([cc, sc_], axis=0), F32).astype(BF16)
    cl, sl = _cos_sin(seq, seq, seq, seq ** -0.5)
    m_ctx_tab = jnp.asarray(np.concatenate([cl, -sl], axis=0), F32).astype(BF16)
    c1, s1 = _cos_sin(l1, l1, l1, l1 ** -0.5)
    m1_tab = jnp.asarray(np.concatenate([c1, -s1], axis=0), F32).astype(BF16)
    tcn, tsn = _cos_sin(l2, l1, l1 * l2, 1.0)
    tc_tab = jnp.asarray(np.repeat(tcn[:, :, None], 128, axis=2), F32)
    ts_tab = jnp.asarray(np.repeat(tsn[:, :, None], 128, axis=2), F32)
    c2, s2 = _cos_sin(l2, l2, l2, l2 ** -0.5)
    m2_tab = jnp.asarray(np.block([[c2, s2], [-s2, c2]]), F32).astype(BF16)

    xc, xl = x_prompt.reshape(nc, d), x_sample.reshape(nl, d)
    x = None

    gla_cols = 2 * N_HEADS * DK + 2 * V_W
    mix_col0 = gla_cols + 2 * LOWRANK
    w_gla = w_in[:, :, :gla_cols + GK_PAD].astype(BF16)
    w_mix = w_in[:, :, mix_col0:].astype(BF16)
    zf = jnp.zeros((depth, LOWRANK, N_HEADS, DK), F32)
    top = jnp.stack([w_gk_f.reshape(depth, LOWRANK, N_HEADS, DK), zf], axis=3).reshape(depth, LOWRANK, FB_W)
    bot = jnp.stack([zf, w_gk_b.reshape(depth, LOWRANK, N_HEADS, DK)], axis=3).reshape(depth, LOWRANK, FB_W)
    wgk2 = jnp.concatenate([top, bot, jnp.zeros((depth, GK_PAD - 2 * LOWRANK, FB_W), F32)], axis=1).astype(BF16)
    bgk = jnp.stack([b_gk_f.reshape(depth, N_HEADS, DK), b_gk_b.reshape(depth, N_HEADS, DK)],
                    axis=2).reshape(depth, 1, FB_W)
    wb, wa, wc_out, wo = (t.astype(BF16) for t in (w_b_out, w_a_out, w_c_out, w_o))
    wup, wdn = w_up.astype(BF16), w_down.astype(BF16)
    n1, n2, gn = norm1[:, None, :], norm2[:, None, :], gla_norm[:, None, :]
    s0_ctx = jnp.zeros((1, b_ctx, N_HEADS, DV, 2 * DK), F32)
    s0_lat = state_gla.transpose(1, 0, 3, 5, 2, 4).reshape(depth, b_lat, N_HEADS, DV, 2 * DK)

    h = pl.pallas_call(
        functools.partial(_prenorm_kernel, n_ctx_tiles=t_pre.n_ctx),
        grid=(t_pre.n,),
        in_specs=[t_pre.tok_ctx(d), t_pre.tok_lat(d), t_pre.mod(0), _layer_spec(0, (1, d))],
        out_specs=t_pre.tok(d),
        out_shape=jax.ShapeDtypeStruct((nt, d), BF16),
        compiler_params=_params(("arbitrary",)),
        name="prenorm",
    )(xc, xl, mods, n1)

    new_states = []
    y_out = None
    for l in range(depth):
        tok = t_gla.tok
        og, qin, oin, u, g = pl.pallas_call(
            _gla_in_kernel,
            grid=(t_gla.n,),
            in_specs=[tok(d), _layer_spec(l, (d, gla_cols + GK_PAD)), _layer_spec(l, (GK_PAD, FB_W)),
                      _layer_spec(l, (1, FB_W))],
            out_specs=[tok(V_W), tok(FB_W), tok(V_W),
                       pl.BlockSpec((TM_GLA // SUB, N_HEADS, DV, CPS * 2 * DK), lambda i: (i, 0, 0, 0)),
                       pl.BlockSpec((TM_GLA // SUB, N_HEADS, 1, CPS * 2 * DK), lambda i: (i, 0, 0, 0))],
            out_shape=[jax.ShapeDtypeStruct((nt, V_W), BF16), jax.ShapeDtypeStruct((nt, FB_W), BF16),
                       jax.ShapeDtypeStruct((nt, V_W), BF16),
                       jax.ShapeDtypeStruct((n_sub_tot, N_HEADS, DV, CPS * 2 * DK), BF16),
                       jax.ShapeDtypeStruct((n_sub_tot, N_HEADS, 1, CPS * 2 * DK), F32)],
            scratch_shapes=[pltpu.VMEM((TM_GLA, FB_W), F32), pltpu.VMEM((TM_GLA, FB_W), F32),
                            pltpu.VMEM((TM_GLA, V_W), BF16), pltpu.VMEM((TM_GLA, FB_W), F32)],
            compiler_params=_params(("arbitrary",)),
            name="gla_in",
        )(h, w_gla, wgk2, bgk)

        tok = t_mix.tok
        fx, g0, g2, yp = pl.pallas_call(
            functools.partial(_mix_in_kernel, n_ctx_tiles=t_mix.n_ctx, period_ctx=seq, period_lat=GRID_W),
            grid=(t_mix.n,),
            in_specs=[tok(d), _layer_spec(l, (d, 3 * SC_W + FN_W + 3 * d)),
                      _layer_spec(l, (3, SC_W)), _layer_spec(l, (SC_W, d))],
            out_specs=[pl.BlockSpec((FN_GROUPS, TM_MIX, FN_GW), lambda i: (0, i, 0)), tok(d), tok(d), tok(d)],
            out_shape=[jax.ShapeDtypeStruct((FN_GROUPS, nt, FN_GW), F32)] + [jax.ShapeDtypeStruct((nt, d), BF16)] * 3,
            compiler_params=_params(("arbitrary",)),
            name="mix_in",
        )(h, w_mix, conv_w, wb)

        ss_ctx, sfin = _scan(u, g, s0_ctx, 0, b_ctx, seq // SUB, 0, math.gcd(b_ctx, 4), N_HEADS)
        ss_lat, _ = _scan(u, g, s0_lat, l, b_lat, n_sub_lat, nc // SUB, 1, 1)
        new_states.append(sfin)

        cs_spec = _const_spec((2 * FN_GW, FN_GW))
        sq = math.gcd(b_ctx, 4)
        fr_ctx = pl.pallas_call(
            functools.partial(_dft_direct_kernel, seq=seq),
            grid=(b_ctx // sq,),
            in_specs=[pl.BlockSpec((FN_GROUPS, sq * seq, FN_GW), lambda s: (0, s, 0)),
                      _const_spec((2 * seq, seq)), cs_spec],
            out_specs=pl.BlockSpec((sq * seq, FN_W), lambda s: (s, 0)),
            out_shape=jax.ShapeDtypeStruct((nc, FN_W), BF16),
            compiler_params=_params(("arbitrary",)),
            name="dft_ctx",
        )(fx, m_ctx_tab, cs_tab)
        lat_blk0 = nc // dec_seq
        n_st1, n_st2 = l2 // CT_NB, l1 // CT_KB
        t_rows = max(CT_NB * _pitch(l1), CT_KB * _pitch(l2))
        fr_lat = pl.pallas_call(
            functools.partial(_ct_kernel, l1=l1, l2=l2),
            grid=(b_lat, n_st1 + n_st2),
            in_specs=[pl.BlockSpec((FN_GROUPS, dec_seq, FN_GW), lambda b, j: (0, lat_blk0 + b, 0)),
                      _const_spec((2 * l1, l1)), _const_spec((l2, l1, 128)), _const_spec((l2, l1, 128)),
                      _const_spec((2 * l2, 2 * l2)), cs_spec],
            out_specs=pl.BlockSpec((None, l2, CT_KB, FN_W), lambda b, j: (b, 0, jnp.maximum(j - n_st1, 0), 0)),
            out_shape=jax.ShapeDtypeStruct((b_lat, l2, l1, FN_W), BF16),
            scratch_shapes=[pltpu.VMEM((FN_GROUPS, l1 * _pitch(l2), FN_GW), F32),
                            pltpu.VMEM((l1, l2, FN_W), BF16), pltpu.VMEM((l1, l2, FN_W), BF16),
                            pltpu.VMEM((FN_GROUPS, t_rows, FN_GW), F32), pltpu.VMEM((FN_GROUPS, t_rows, FN_GW), F32)],
            compiler_params=_params(("arbitrary", "arbitrary")),
            name="dft_lat",
        )(fx, m1_tab, tc_tab, ts_tab, m2_tab, cs_tab).reshape(nl, FN_W)

        tok = t_out.tok
        split_x = x is None
        ss_spec = lambda blk: pl.BlockSpec((TM_OUT // SUB, N_HEADS, DV, CPS * 2 * DK),
                                           lambda i: (blk(i), 0, 0, 0))
        x1, h2 = pl.pallas_call(
            functools.partial(_mix_out_kernel, n_ctx_tiles=t_out.n_ctx, split_x=split_x),
            grid=(t_out.n,),
            in_specs=([t_out.tok_ctx(d), t_out.tok_lat(d)] if split_x else [tok(d)]) + [
                tok(V_W), tok(FB_W), ss_spec(t_out.ctx_blk), ss_spec(t_out.lat_blk), tok(V_W),
                t_out.tok_ctx(FN_W), t_out.tok_lat(FN_W), tok(d), tok(d), tok(d), t_out.mod(l),
                _layer_spec(l, (1, DV)), _layer_spec(l, (1, d)),
                _layer_spec(l, (V_W, d)), _layer_spec(l, (FN_W, d)), _layer_spec(l, (d, d))],
            out_specs=[tok(d), tok(d)],
            out_shape=[jax.ShapeDtypeStruct((nt, d), F32), jax.ShapeDtypeStruct((nt, d), BF16)],
            scratch_shapes=[pltpu.VMEM((TM_OUT, V_W), F32)],
            compiler_params=_params(("arbitrary",)),
            name="mix_out",
        )(*((xc, xl) if split_x else (x,)), oin, qin, ss_ctx, ss_lat, og, fr_ctx, fr_lat, g0, g2, yp, mods,
          gn, n2, wa, wc_out, wo)

        final = l == depth - 1
        tok = t_ffn.tok
        nxt = depth - 1 if final else l + 1
        outs = pl.pallas_call(
            functools.partial(_ffn_kernel, final=final, n_ctx_tiles=t_ffn.n_ctx),
            grid=(t_ffn.n,),
            in_specs=[tok(d), tok(d), _layer_spec(l, (d, 2 * d_ff)), _layer_spec(l, (d_ff, d)),
                      t_ffn.mod(l), t_ffn.mod(nxt), _const_spec((1, d)) if final else _layer_spec(nxt, (1, d))],
            out_specs=[t_ffn.tok_ctx(d), t_ffn.tok_lat(d)] if final else [tok(d), tok(d)],
            out_shape=([jax.ShapeDtypeStruct((nc, d), F32), jax.ShapeDtypeStruct((nl, d), F32)] if final else
                       [jax.ShapeDtypeStruct((nt, d), F32), jax.ShapeDtypeStruct((nt, d), BF16)]),
            compiler_params=_params(("arbitrary",)),
            name="ffn",
        )(h2, x1, wup, wdn, mods, mods, norm_f[None, :] if final else n1)
        if final:
            y_out = outs
        else:
            x, h = outs

    y_prompt = y_out[0].reshape(b_ctx, seq, d)
    y_sample = y_out[1].reshape(b_lat, dec_seq, d)
    sfin_all = jnp.stack(new_states, axis=0).reshape(depth, b_ctx, N_HEADS, DV, 2, DK)
    new_state_gla = sfin_all.transpose(1, 0, 4, 2, 5, 3).astype(x_prompt.dtype)
    return (y_prompt, y_sample, new_state_gla)
```

```python
import functools
import math

import numpy as np
import jax
import jax.numpy as jnp
from jax import lax
from jax.experimental import pallas as pl
from jax.experimental.pallas import tpu as pltpu

F32 = jnp.float32
BF16 = jnp.bfloat16

GRID_W = 64
N_HEADS = 4
DK = 64
DV = 128
V_W = N_HEADS * DV
FB_W = N_HEADS * 2 * DK
LOWRANK = 16
GATE_NORMALIZER = 16.0
CHUNK = 64
SUB = 256
CPS = SUB // CHUNK
SC_W = 512
FN_GROUPS = 4
FN_GW = 128
FN_W = FN_GROUPS * FN_GW
N_MOD = 6
EPS = 1e-6
GK_PAD = 128

TM_GLA = 1024
TM_MIX = 1024
TM_OUT = 512
TM_FFN = 512
TM_PRE = 2048
FF_CHUNK = 512
CT_NB = 16
CT_KB = 16
VMEM_LIMIT = 56 * 1024 * 1024


def _dot(a, b):
    return jnp.dot(a, b, preferred_element_type=F32)


def _dot_nt(a, b):
    return lax.dot_general(a, b, (((1,), (1,)), ((), ())), preferred_element_type=F32)


def _dot_tn(a, b):
    return lax.dot_general(a, b, (((0,), (0,)), ((), ())), preferred_element_type=F32)


def _sigmoid(x):
    return 1.0 / (1.0 + jnp.exp(-x))


def _silu(x):
    return x * _sigmoid(x)


def _rms(x):
    return x * lax.rsqrt(jnp.mean(x * x, axis=-1, keepdims=True) + EPS)


def _params(sem):
    return pltpu.CompilerParams(dimension_semantics=sem, vmem_limit_bytes=VMEM_LIMIT)


def _const_spec(shape):
    nd = len(shape)
    return pl.BlockSpec(shape, lambda *_: (0,) * nd, pipeline_mode=pl.Buffered(1))


def _layer_spec(layer, shape):
    nd = len(shape)
    return pl.BlockSpec((None,) + tuple(shape), lambda *_: (layer,) + (0,) * nd, pipeline_mode=pl.Buffered(1))


def _ada_kernel(c_ref, w_ref, b_ref, o_ref):
    s = _silu(c_ref[...]).astype(BF16)
    o_ref[...] = _dot(s, w_ref[...].astype(BF16)) + b_ref[...]


def _ada(cond, w_ada, b_ada):
    depth, d, n = w_ada.shape
    rows = cond.shape[0]
    nb = 1536
    return pl.pallas_call(
        _ada_kernel,
        grid=(depth, n // nb),
        in_specs=[
            pl.BlockSpec((rows, d), lambda l, j: (0, 0)),
            pl.BlockSpec((None, d, nb), lambda l, j: (l, 0, j)),
            pl.BlockSpec((None, 1, nb), lambda l, j: (l, 0, j)),
        ],
        out_specs=pl.BlockSpec((None, rows, nb), lambda l, j: (l, 0, j)),
        out_shape=jax.ShapeDtypeStruct((depth, rows, n), F32),
        compiler_params=_params(("arbitrary", "arbitrary")),
        name="ada",
    )(cond, w_ada, b_ada.reshape(depth, 1, n))


def _prenorm_kernel(xc_ref, xl_ref, mod_ref, n_ref, h_ref, *, n_ctx_tiles):
    x = jnp.where(pl.program_id(0) < n_ctx_tiles, xc_ref[...], xl_ref[...])
    m = mod_ref[...]
    h = _rms(x) * n_ref[...] * (1.0 + m[1:2]) + m[0:1]
    h_ref[...] = h.astype(BF16)


def _gla_in_kernel(h_ref, w_ref, wgk2_ref, bgk_ref,
                   og_ref, qin_ref, oin_ref, u_ref, g_ref,
                   q_s, k_s, v_s, la_s):
    qk_w = N_HEADS * DK
    z = _dot(h_ref[...], w_ref[...])
    lo_half = lax.broadcasted_iota(jnp.int32, (1, 2 * DK), 1) < DK

    def both_directions(x):
        tiles = []
        for p in range(N_HEADS // 2):
            t = x[:, p * 2 * DK:(p + 1) * 2 * DK]
            r = pltpu.roll(t, DK, axis=1)
            tiles += [jnp.where(lo_half, t, r), jnp.where(lo_half, r, t)]
        return jnp.concatenate(tiles, axis=1)

    q_s[...] = both_directions(z[:, :qk_w]) * (DK ** -0.5)
    k_s[...] = both_directions(z[:, qk_w:2 * qk_w])
    v_s[...] = z[:, 2 * qk_w:2 * qk_w + V_W].astype(BF16)
    og_ref[...] = z[:, 2 * qk_w + V_W:2 * qk_w + 2 * V_W].astype(BF16)
    gk = z[:, 2 * qk_w + 2 * V_W:].astype(BF16)
    lp = _dot(gk, wgk2_ref[...]) + bgk_ref[...]
    la_s[...] = (jnp.minimum(lp, 0.0) - jnp.log(1.0 + jnp.exp(-jnp.abs(lp)))) * (1.0 / GATE_NORMALIZER)

    row = lax.broadcasted_iota(jnp.int32, (SUB, SUB), 0)
    col = lax.broadcasted_iota(jnp.int32, (SUB, SUB), 1)
    same = (row & -CHUNK) == (col & -CHUNK)
    lower = same & (col <= row)
    upper = same & (col >= row)
    tri = jnp.where(lower, 1.0, 0.0).astype(BF16)
    is_f = (lax.broadcasted_iota(jnp.int32, (SUB, FB_W), 1) & (2 * DK - 1)) < DK
    rchunk = lax.broadcasted_iota(jnp.int32, (SUB, 2 * DK), 0) & -CHUNK

    def sub_tile(s, carry):
        r0 = pl.multiple_of(s * SUB, SUB)
        rows = pl.ds(r0, SUB)
        la = la_s[rows, :]
        hi = la.astype(BF16)
        r1 = la - hi.astype(F32)
        mid = r1.astype(BF16)
        lo = (r1 - mid.astype(F32)).astype(BF16)
        pre = _dot(tri, hi) + _dot(tri, mid) + _dot(tri, lo)
        tot_rows = [pre[c * CHUNK + CHUNK - 1:c * CHUNK + CHUNK, :] for c in range(CPS)]
        tot = jnp.concatenate([jnp.broadcast_to(t, (CHUNK, FB_W)) for t in tot_rows], axis=0)
        b = jnp.where(is_f, pre, tot - pre + la)
        q = q_s[rows, :]
        k = k_s[rows, :]
        qin = (q * jnp.exp(b)).astype(BF16)
        kin = (k * jnp.exp(-b)).astype(BF16)
        kout = (k * jnp.exp(tot - b)).astype(BF16)
        qin_ref[rows, :] = qin
        zero = jnp.zeros_like(qin)
        qf = jnp.where(is_f, qin, zero)
        qb = jnp.where(is_f, zero, qin)
        v = v_s[rows, :]
        for hh in range(N_HEADS):
            fb = slice(hh * 2 * DK, (hh + 1) * 2 * DK)
            vs = slice(hh * DV, (hh + 1) * DV)
            a2 = _dot_nt(jnp.concatenate([qf[:, fb], qb[:, fb]], axis=0), kin[:, fb])
            att = (jnp.where(lower, a2[:SUB], 0.0) + jnp.where(upper, a2[SUB:], 0.0)).astype(BF16)
            oin_ref[rows, vs] = _dot(att, v[:, vs]).astype(BF16)
            ko = kout[:, fb]
            kbd = jnp.concatenate([jnp.where(rchunk == c * CHUNK, ko, jnp.zeros_like(ko)) for c in range(CPS)],
                                  axis=1)
            u_ref[s, hh] = _dot_tn(v[:, vs], kbd).astype(BF16)
            g_ref[s, hh] = jnp.concatenate([jnp.exp(t[:, fb]) for t in tot_rows], axis=1)
        return carry

    lax.fori_loop(0, h_ref.shape[0] // SUB, sub_tile, 0, unroll=2)


def _mix_in_kernel(h_ref, w_ref, cw_ref, wb_ref,
                   fx_ref, g0_ref, g2_ref, yp_ref, *, n_ctx_tiles, period_ctx, period_lat):
    i = pl.program_id(0)
    d = g0_ref.shape[-1]
    z = _dot(h_ref[...], w_ref[...])
    sb, sc, sx = z[:, :SC_W], z[:, SC_W:2 * SC_W], z[:, 2 * SC_W:3 * SC_W]
    m0 = 3 * SC_W + FN_W
    u = sc * sx
    tm = h_ref.shape[0]
    period = jnp.where(i < n_ctx_tiles, period_ctx, period_lat)
    pos = lax.broadcasted_iota(jnp.int32, (tm, 1), 0) & (period - 1)
    up = jnp.where(pos == 0, 0.0, pltpu.roll(u, 1, axis=0))
    un = jnp.where(pos == period - 1, 0.0, pltpu.roll(u, tm - 1, axis=0))
    cw = cw_ref[...]
    conv = cw[0:1] * up + cw[1:2] * u + cw[2:3] * un
    yb = _dot((sb * conv).astype(BF16), wb_ref[...])

    _put_rows(fx_ref, 0, z[:, 3 * SC_W:m0])
    g0_ref[...] = _sigmoid(z[:, m0:m0 + d]).astype(BF16)
    yp_ref[...] = (_sigmoid(z[:, m0 + d:m0 + 2 * d]) * yb).astype(BF16)
    g2_ref[...] = _sigmoid(z[:, m0 + 2 * d:m0 + 3 * d]).astype(BF16)


def _scan_kernel(u_ref, g_ref, s0_ref, ss_ref, sfin_ref, *, n_sub):
    lane_f = lax.broadcasted_iota(jnp.int32, (DV, 2 * DK), 1) < DK
    for q in range(s0_ref.shape[0]):
        for hh in range(s0_ref.shape[1]):
            s0 = s0_ref[q, hh]

            def fwd(j, st):
                for c in range(CPS):
                    cs = slice(c * 2 * DK, (c + 1) * 2 * DK)
                    ss_ref[j, hh, :, cs] = st.astype(BF16)
                    st = g_ref[j, hh, :, cs] * st + u_ref[j, hh, :, cs].astype(F32)
                return st

            def bwd(j, st):
                for c in range(CPS - 1, -1, -1):
                    cs = slice(c * 2 * DK, (c + 1) * 2 * DK)
                    ss_ref[j, hh, :, cs] = jnp.where(lane_f, ss_ref[j, hh, :, cs], st.astype(BF16))
                    st = g_ref[j, hh, :, cs] * st + u_ref[j, hh, :, cs].astype(F32)
                return st

            lo = q * n_sub
            sf = lax.fori_loop(0, n_sub, lambda i, st: fwd(lo + i, st), s0)
            sb = lax.fori_loop(0, n_sub, lambda i, st: bwd(lo + n_sub - 1 - i, st), s0)
            sfin_ref[q, hh] = jnp.where(lane_f, sf, sb)


def _scan(u, g, s0, layer, n_seq, n_sub, sub0, seq_blk, head_blk):
    rows = seq_blk * n_sub
    assert n_seq % seq_blk == 0 and N_HEADS % head_blk == 0 and sub0 % rows == 0
    wide = CPS * 2 * DK
    return pl.pallas_call(
        functools.partial(_scan_kernel, n_sub=n_sub),
        grid=(n_seq // seq_blk, N_HEADS // head_blk),
        in_specs=[
            pl.BlockSpec((rows, head_blk, DV, wide), lambda b, hh: (sub0 // rows + b, hh, 0, 0)),
            pl.BlockSpec((rows, head_blk, 1, wide), lambda b, hh: (sub0 // rows + b, hh, 0, 0)),
            pl.BlockSpec((None, seq_blk, head_blk, DV, 2 * DK), lambda b, hh: (layer, b, hh, 0, 0)),
        ],
        out_specs=[
            pl.BlockSpec((rows, head_blk, DV, wide), lambda b, hh: (b, hh, 0, 0)),
            pl.BlockSpec((seq_blk, head_blk, DV, 2 * DK), lambda b, hh: (b, hh, 0, 0)),
        ],
        out_shape=[
            jax.ShapeDtypeStruct((n_seq * n_sub, N_HEADS, DV, wide), BF16),
            jax.ShapeDtypeStruct((n_seq, N_HEADS, DV, 2 * DK), F32),
        ],
        compiler_params=_params(("arbitrary", "arbitrary")),
        name="scan",
    )(u, g, s0)


def _gather_rows(ref, start, size, stride):
    return jnp.concatenate([ref[g, pl.ds(start, size, stride=stride), :] for g in range(FN_GROUPS)], axis=1)


def _put_rows(ref, r0, val):
    for g in range(FN_GROUPS):
        ref[g, r0:r0 + val.shape[0], :] = val[:, g * FN_GW:(g + 1) * FN_GW]


def _channel_dft_real(re_g, im_g, cs):
    return _dot(jnp.concatenate([re_g.astype(BF16), im_g.astype(BF16)], axis=1), cs)


def _dft_direct_kernel(x_ref, m_ref, cs_ref, o_ref, *, seq):
    for q in range(x_ref.shape[1] // seq):
        rows = slice(q * seq, (q + 1) * seq)
        x = jnp.concatenate([x_ref[g, rows, :] for g in range(FN_GROUPS)], axis=1)
        res = _dot(m_ref[...], x.astype(BF16))
        for g in range(FN_GROUPS):
            gs = slice(g * FN_GW, (g + 1) * FN_GW)
            o_ref[rows, gs] = _channel_dft_real(res[:seq, gs], res[seq:, gs], cs_ref[...]).astype(BF16)


def _pitch(n):
    return n + 4


def _ct_kernel(x_ref, m1_ref, tc_ref, ts_ref, m2_ref, cs_ref, o_ref, xs_s, yr_s, yi_s, tr_s, ti_s, *, l1, l2):
    j = pl.program_id(1)
    n_stage1 = l2 // CT_NB
    px, pt1, pt2 = _pitch(l2), _pitch(l1), _pitch(l2)

    @pl.when((pl.program_id(0) == 0) & (j == 0))
    def _():
        tr_s[...] = jnp.zeros_like(tr_s)
        ti_s[...] = jnp.zeros_like(ti_s)

    @pl.when(j == 0)
    def _():
        for n1 in range(l1):
            for g in range(FN_GROUPS):
                xs_s[g, n1 * px:n1 * px + l2, :] = x_ref[g, n1 * l2:(n1 + 1) * l2, :]

    @pl.when(j < n_stage1)
    def _():
        m = m1_ref[...]
        col0 = pl.multiple_of(j * CT_NB, CT_NB)
        for jn in range(CT_NB):
            xs = _gather_rows(xs_s, col0 + jn, l1, px)
            res = _dot(m, xs.astype(BF16))
            yr, yi = res[:l1], res[l1:]
            c = jnp.concatenate([tc_ref[col0 + jn]] * FN_GROUPS, axis=1)
            s = jnp.concatenate([ts_ref[col0 + jn]] * FN_GROUPS, axis=1)
            _put_rows(tr_s, jn * pt1, yr * c + yi * s)
            _put_rows(ti_s, jn * pt1, yi * c - yr * s)
        for k1 in range(l1):
            yr_s[k1, pl.ds(col0, CT_NB), :] = _gather_rows(tr_s, k1, CT_NB, pt1).astype(BF16)
            yi_s[k1, pl.ds(col0, CT_NB), :] = _gather_rows(ti_s, k1, CT_NB, pt1).astype(BF16)

    @pl.when(j >= n_stage1)
    def _():
        m = m2_ref[...]
        k0 = (j - n_stage1) * CT_KB
        for kk in range(CT_KB):
            rhs = jnp.concatenate([yr_s[k0 + kk], yi_s[k0 + kk]], axis=0)
            res = _dot(m, rhs)
            _put_rows(tr_s, kk * pt2, res[:l2])
            _put_rows(ti_s, kk * pt2, res[l2:])
        rows = CT_KB * pt2
        for g in range(FN_GROUPS):
            tr_s[g, :rows, :] = _channel_dft_real(tr_s[g, :rows, :], ti_s[g, :rows, :], cs_ref[...])
        for k2 in range(l2):
            o_ref[k2] = _gather_rows(tr_s, k2, CT_KB, pt2).astype(BF16)


def _cos_sin(n_out, n_in, period, scale):
    idx = (np.arange(n_out)[:, None] * np.arange(n_in)[None, :]) % period
    ang = 2.0 * np.pi * idx / period
    return np.cos(ang) * scale, np.sin(ang) * scale


def _mix_out_kernel(oin_ref, qin_ref, ssc_ref, ssl_ref, og_ref, frc_ref, frl_ref, g0_ref, g2_ref, yp_ref,
                    gn_ref, wa_ref, wc_ref, y_ref, o_s, *, n_ctx_tiles):
    is_ctx = pl.program_id(0) < n_ctx_tiles
    for s in range(oin_ref.shape[0] // SUB):
        for hh in range(N_HEADS):
            st = jnp.where(is_ctx, ssc_ref[s, hh], ssl_ref[s, hh])
            fb = slice(hh * 2 * DK, (hh + 1) * 2 * DK)
            vs = slice(hh * DV, (hh + 1) * DV)
            for c in range(CPS):
                rows = slice(s * SUB + c * CHUNK, s * SUB + (c + 1) * CHUNK)
                o_s[rows, vs] = oin_ref[rows, vs].astype(F32) + _dot_nt(qin_ref[rows, fb], st[:, c * 2 * DK:(c + 1) * 2 * DK])
    gn = gn_ref[...]
    parts = []
    for hh in range(N_HEADS):
        vs = slice(hh * DV, (hh + 1) * DV)
        parts.append((_rms(o_s[:, vs]) * gn * _silu(og_ref[:, vs].astype(F32))).astype(BF16))
    ya = _dot(jnp.concatenate(parts, axis=1), wa_ref[...])
    yc = _dot(jnp.where(is_ctx, frc_ref[...], frl_ref[...]), wc_ref[...])
    y = g0_ref[...].astype(F32) * ya + yp_ref[...].astype(F32) + g2_ref[...].astype(F32) * yc
    y_ref[...] = y.astype(BF16)


def _ffn_kernel(*refs, final, split_x, n_ctx_tiles):
    if split_x:
        xc_ref, xl_ref, *refs = refs
    else:
        x_ref, *refs = refs
    y_ref, wo_ref, wup_ref, wd_ref, mod_ref, modn_ref, n2_ref, nn_ref, *outs = refs
    is_ctx = pl.program_id(0) < n_ctx_tiles
    x = jnp.where(is_ctx, xc_ref[...], xl_ref[...]) if split_x else x_ref[...]
    m = mod_ref[...]
    x1 = x + m[2:3] * _dot(y_ref[...], wo_ref[...])
    h2 = (_rms(x1) * n2_ref[...] * (1.0 + m[4:5]) + m[3:4]).astype(BF16)
    d_ff = wd_ref.shape[0]
    acc = None
    for c0 in range(0, d_ff, FF_CHUNK):
        c1 = min(c0 + FF_CHUNK, d_ff)
        gate = _dot(h2, wup_ref[:, c0:c1])
        up = _dot(h2, wup_ref[:, d_ff + c0:d_ff + c1])
        part = _dot((_silu(gate) * up).astype(BF16), wd_ref[c0:c1, :])
        acc = part if acc is None else acc + part
    x2 = x1 + m[5:6] * acc
    if final:
        yc_ref, yl_ref = outs
        out = _rms(x2) * nn_ref[...]

        @pl.when(is_ctx)
        def _():
            yc_ref[...] = out

        @pl.when(jnp.logical_not(is_ctx))
        def _():
            yl_ref[...] = out
    else:
        x2_ref, hn_ref = outs
        x2_ref[...] = x2
        hn_ref[...] = (_rms(x2) * nn_ref[...] * (1.0 + modn_ref[1:2, :]) + modn_ref[0:1, :]).astype(BF16)


def kernel(x_prompt, x_sample, state_gla, c, c_ctx, w_ada, b_ada, norm1, norm2, w_in, w_gk_f, b_gk_f,
           w_gk_b, b_gk_b, gla_norm, w_a_out, conv_w, w_b_out, w_c_out, w_o, w_up, w_down, norm_f):
    b_ctx, seq, d = x_prompt.shape
    b_lat, dec_seq, _ = x_sample.shape
    depth = w_ada.shape[0]
    d_ff = w_down.shape[1]
    nc, nl = b_ctx * seq, b_lat * dec_seq
    nt = nc + nl
    l1, l2 = dec_seq // GRID_W, GRID_W
    assert seq == SUB and d_ff % 128 == 0
    assert nc % dec_seq == 0 and l2 % CT_NB == 0 and l1 % CT_KB == 0
    n_sub_tot, n_sub_lat = nt // SUB, dec_seq // SUB
    assert (nc // SUB) % n_sub_lat == 0
    n_cond = -(-(1 + b_lat) // 8) * 8

    class Tiling:
        def __init__(self, tm):
            assert nc % tm == 0 and dec_seq % tm == 0
            self.tm, self.n, self.n_ctx, self.per_lat = tm, nt // tm, nc // tm, dec_seq // tm

        def cond(self, i):
            return jnp.where(i < self.n_ctx, 0, 1 + (i - self.n_ctx) // self.per_lat)

        def ctx_blk(self, i):
            return jnp.minimum(i, self.n_ctx - 1)

        def lat_blk(self, i):
            return jnp.maximum(i - self.n_ctx, 0)

        def tok(self, w):
            return pl.BlockSpec((self.tm, w), lambda i, *_: (i, 0))

        def tok_ctx(self, w):
            return pl.BlockSpec((self.tm, w), lambda i, *_: (self.ctx_blk(i), 0))

        def tok_lat(self, w):
            return pl.BlockSpec((self.tm, w), lambda i, *_: (self.lat_blk(i), 0))

        def mod(self, layer):
            return pl.BlockSpec((None, None, N_MOD, d), lambda i, *_: (layer, self.cond(i), 0, 0))

    t_gla, t_mix, t_out, t_ffn = Tiling(TM_GLA), Tiling(TM_MIX), Tiling(TM_OUT), Tiling(TM_FFN)
    t_pre = Tiling(math.gcd(math.gcd(nc, dec_seq), TM_PRE))

    cond = jnp.concatenate([c_ctx[None, :], c, jnp.zeros((n_cond - 1 - b_lat, d), F32)], axis=0)
    mods = _ada(cond, w_ada, b_ada).reshape(depth, n_cond, N_MOD, d)

    cc, sc_ = _cos_sin(FN_GW, FN_GW, FN_GW, FN_GW ** -0.5)
    cs_tab = jnp.asarray(np.concatenate([cc, sc_], axis=0), F32).astype(BF16)
    cl, sl = _cos_sin(seq, seq, seq, seq ** -0.5)
    m_ctx_tab = jnp.asarray(np.concatenate([cl, -sl], axis=0), F32).astype(BF16)
    c1, s1 = _cos_sin(l1, l1, l1, l1 ** -0.5)
    m1_tab = jnp.asarray(np.concatenate([c1, -s1], axis=0), F32).astype(BF16)
    tcn, tsn = _cos_sin(l2, l1, l1 * l2, 1.0)
    tc_tab = jnp.asarray(np.repeat(tcn[:, :, None], 128, axis=2), F32)
    ts_tab = jnp.asarray(np.repeat(tsn[:, :, None], 128, axis=2), F32)
    c2, s2 = _cos_sin(l2, l2, l2, l2 ** -0.5)
    m2_tab = jnp.asarray(np.block([[c2, s2], [-s2, c2]]), F32).astype(BF16)

    xc, xl = x_prompt.reshape(nc, d), x_sample.reshape(nl, d)
    x = None

    gla_cols = 2 * N_HEADS * DK + 2 * V_W
    mix_col0 = gla_cols + 2 * LOWRANK
    w_gla = w_in.astype(BF16)
    w_mix = w_gla[:, :, mix_col0:]
    zf = jnp.zeros((depth, LOWRANK, N_HEADS, DK), F32)
    top = jnp.stack([w_gk_f.reshape(depth, LOWRANK, N_HEADS, DK), zf], axis=3).reshape(depth, LOWRANK, FB_W)
    bot = jnp.stack([zf, w_gk_b.reshape(depth, LOWRANK, N_HEADS, DK)], axis=3).reshape(depth, LOWRANK, FB_W)
    wgk2 = jnp.concatenate([top, bot, jnp.zeros((depth, GK_PAD - 2 * LOWRANK, FB_W), F32)], axis=1).astype(BF16)
    bgk = jnp.stack([b_gk_f.reshape(depth, N_HEADS, DK), b_gk_b.reshape(depth, N_HEADS, DK)],
                    axis=2).reshape(depth, 1, FB_W)
    wb, wa, wc_out, wo = (t.astype(BF16) for t in (w_b_out, w_a_out, w_c_out, w_o))
    wup, wdn = w_up.astype(BF16), w_down.astype(BF16)
    n1, n2, gn = norm1[:, None, :], norm2[:, None, :], gla_norm[:, None, :]
    s0_ctx = jnp.zeros((1, b_ctx, N_HEADS, DV, 2 * DK), F32)
    s0_lat = state_gla.transpose(1, 0, 3, 5, 2, 4).reshape(depth, b_lat, N_HEADS, DV, 2 * DK)

    h = pl.pallas_call(
        functools.partial(_prenorm_kernel, n_ctx_tiles=t_pre.n_ctx),
        grid=(t_pre.n,),
        in_specs=[t_pre.tok_ctx(d), t_pre.tok_lat(d), t_pre.mod(0), _layer_spec(0, (1, d))],
        out_specs=t_pre.tok(d),
        out_shape=jax.ShapeDtypeStruct((nt, d), BF16),
        compiler_params=_params(("arbitrary",)),
        name="prenorm",
    )(xc, xl, mods, n1)

    new_states = []
    y_out = None
    for l in range(depth):
        tok = t_gla.tok
        og, qin, oin, u, g = pl.pallas_call(
            _gla_in_kernel,
            grid=(t_gla.n,),
            in_specs=[tok(d), _layer_spec(l, (d, gla_cols + GK_PAD)), _layer_spec(l, (GK_PAD, FB_W)),
                      _layer_spec(l, (1, FB_W))],
            out_specs=[tok(V_W), tok(FB_W), tok(V_W),
                       pl.BlockSpec((TM_GLA // SUB, N_HEADS, DV, CPS * 2 * DK), lambda i: (i, 0, 0, 0)),
                       pl.BlockSpec((TM_GLA // SUB, N_HEADS, 1, CPS * 2 * DK), lambda i: (i, 0, 0, 0))],
            out_shape=[jax.ShapeDtypeStruct((nt, V_W), BF16), jax.ShapeDtypeStruct((nt, FB_W), BF16),
                       jax.ShapeDtypeStruct((nt, V_W), BF16),
                       jax.ShapeDtypeStruct((n_sub_tot, N_HEADS, DV, CPS * 2 * DK), BF16),
                       jax.ShapeDtypeStruct((n_sub_tot, N_HEADS, 1, CPS * 2 * DK), F32)],
            scratch_shapes=[pltpu.VMEM((TM_GLA, FB_W), F32), pltpu.VMEM((TM_GLA, FB_W), F32),
                            pltpu.VMEM((TM_GLA, V_W), BF16), pltpu.VMEM((TM_GLA, FB_W), F32)],
            compiler_params=_params(("arbitrary",)),
            name="gla_in",
        )(h, w_gla, wgk2, bgk)

        tok = t_mix.tok
        fx, g0, g2, yp = pl.pallas_call(
            functools.partial(_mix_in_kernel, n_ctx_tiles=t_mix.n_ctx, period_ctx=seq, period_lat=GRID_W),
            grid=(t_mix.n,),
            in_specs=[tok(d), _layer_spec(l, (d, 3 * SC_W + FN_W + 3 * d)),
                      _layer_spec(l, (3, SC_W)), _layer_spec(l, (SC_W, d))],
            out_specs=[pl.BlockSpec((FN_GROUPS, TM_MIX, FN_GW), lambda i: (0, i, 0)), tok(d), tok(d), tok(d)],
            out_shape=[jax.ShapeDtypeStruct((FN_GROUPS, nt, FN_GW), F32)] + [jax.ShapeDtypeStruct((nt, d), BF16)] * 3,
            compiler_params=_params(("arbitrary",)),
            name="mix_in",
        )(h, w_mix, conv_w, wb)

        ss_ctx, sfin = _scan(u, g, s0_ctx, 0, b_ctx, seq // SUB, 0, math.gcd(b_ctx, 4), N_HEADS)
        ss_lat, _ = _scan(u, g, s0_lat, l, b_lat, n_sub_lat, nc // SUB, 1, 1)
        new_states.append(sfin)

        cs_spec = _const_spec((2 * FN_GW, FN_GW))
        sq = math.gcd(b_ctx, 4)
        fr_ctx = pl.pallas_call(
            functools.partial(_dft_direct_kernel, seq=seq),
            grid=(b_ctx // sq,),
            in_specs=[pl.BlockSpec((FN_GROUPS, sq * seq, FN_GW), lambda s: (0, s, 0)),
                      _const_spec((2 * seq, seq)), cs_spec],
            out_specs=pl.BlockSpec((sq * seq, FN_W), lambda s: (s, 0)),
            out_shape=jax.ShapeDtypeStruct((nc, FN_W), BF16),
            compiler_params=_params(("arbitrary",)),
            name="dft_ctx",
        )(fx, m_ctx_tab, cs_tab)
        lat_blk0 = nc // dec_seq
        n_st1, n_st2 = l2 // CT_NB, l1 // CT_KB
        t_rows = max(CT_NB * _pitch(l1), CT_KB * _pitch(l2))
        fr_lat = pl.pallas_call(
            functools.partial(_ct_kernel, l1=l1, l2=l2),
            grid=(b_lat, n_st1 + n_st2),
            in_specs=[pl.BlockSpec((FN_GROUPS, dec_seq, FN_GW), lambda b, j: (0, lat_blk0 + b, 0)),
                      _const_spec((2 * l1, l1)), _const_spec((l2, l1, 128)), _const_spec((l2, l1, 128)),
                      _const_spec((2 * l2, 2 * l2)), cs_spec],
            out_specs=pl.BlockSpec((None, l2, CT_KB, FN_W), lambda b, j: (b, 0, jnp.maximum(j - n_st1, 0), 0)),
            out_shape=jax.ShapeDtypeStruct((b_lat, l2, l1, FN_W), BF16),
            scratch_shapes=[pltpu.VMEM((FN_GROUPS, l1 * _pitch(l2), FN_GW), F32),
                            pltpu.VMEM((l1, l2, FN_W), BF16), pltpu.VMEM((l1, l2, FN_W), BF16),
                            pltpu.VMEM((FN_GROUPS, t_rows, FN_GW), F32), pltpu.VMEM((FN_GROUPS, t_rows, FN_GW), F32)],
            compiler_params=_params(("arbitrary", "arbitrary")),
            name="dft_lat",
        )(fx, m1_tab, tc_tab, ts_tab, m2_tab, cs_tab).reshape(nl, FN_W)

        tok = t_out.tok
        split_x = x is None
        ss_spec = lambda blk: pl.BlockSpec((TM_OUT // SUB, N_HEADS, DV, CPS * 2 * DK),
                                           lambda i: (blk(i), 0, 0, 0))
        y_mix = pl.pallas_call(
            functools.partial(_mix_out_kernel, n_ctx_tiles=t_out.n_ctx),
            grid=(t_out.n,),
            in_specs=[tok(V_W), tok(FB_W), ss_spec(t_out.ctx_blk), ss_spec(t_out.lat_blk), tok(V_W),
                      t_out.tok_ctx(FN_W), t_out.tok_lat(FN_W), tok(d), tok(d), tok(d),
                      _layer_spec(l, (1, DV)), _layer_spec(l, (V_W, d)), _layer_spec(l, (FN_W, d))],
            out_specs=tok(d),
            out_shape=jax.ShapeDtypeStruct((nt, d), BF16),
            scratch_shapes=[pltpu.VMEM((TM_OUT, V_W), F32)],
            compiler_params=_params(("arbitrary",)),
            name="mix_out",
        )(oin, qin, ss_ctx, ss_lat, og, fr_ctx, fr_lat, g0, g2, yp, gn, wa, wc_out)

        final = l == depth - 1
        tok = t_ffn.tok
        nxt = depth - 1 if final else l + 1
        outs = pl.pallas_call(
            functools.partial(_ffn_kernel, final=final, split_x=split_x, n_ctx_tiles=t_ffn.n_ctx),
            grid=(t_ffn.n,),
            in_specs=([t_ffn.tok_ctx(d), t_ffn.tok_lat(d)] if split_x else [tok(d)]) + [
                tok(d), _layer_spec(l, (d, d)), _layer_spec(l, (d, 2 * d_ff)), _layer_spec(l, (d_ff, d)),
                t_ffn.mod(l), t_ffn.mod(nxt), _layer_spec(l, (1, d)),
                _const_spec((1, d)) if final else _layer_spec(nxt, (1, d))],
            out_specs=[t_ffn.tok_ctx(d), t_ffn.tok_lat(d)] if final else [tok(d), tok(d)],
            out_shape=([jax.ShapeDtypeStruct((nc, d), F32), jax.ShapeDtypeStruct((nl, d), F32)] if final else
                       [jax.ShapeDtypeStruct((nt, d), F32), jax.ShapeDtypeStruct((nt, d), BF16)]),
            compiler_params=_params(("arbitrary",)),
            name="ffn",
        )(*((xc, xl) if split_x else (x,)), y_mix, wo, wup, wdn, mods, mods, n2, norm_f[None, :] if final else n1)
        if final:
            y_out = outs
        else:
            x, h = outs

    y_prompt = y_out[0].reshape(b_ctx, seq, d)
    y_sample = y_out[1].reshape(b_lat, dec_seq, d)
    sfin_all = jnp.stack(new_states, axis=0).reshape(depth, b_ctx, N_HEADS, DV, 2, DK)
    new_state_gla = sfin_all.transpose(1, 0, 4, 2, 5, 3).astype(x_prompt.dtype)
    return (y_prompt, y_sample, new_state_gla)
```

```python
import functools
import math

import numpy as np
import jax
import jax.numpy as jnp
from jax import lax
from jax.experimental import pallas as pl
from jax.experimental.pallas import tpu as pltpu

F32 = jnp.float32
BF16 = jnp.bfloat16

GRID_W = 64
N_HEADS = 4
DK = 64
DV = 128
V_W = N_HEADS * DV
FB_W = N_HEADS * 2 * DK
LOWRANK = 16
GATE_NORMALIZER = 16.0
CHUNK = 64
SUB = 256
CPS = SUB // CHUNK
SC_W = 512
FN_GROUPS = 4
FN_GW = 128
FN_W = FN_GROUPS * FN_GW
N_MOD = 6
EPS = 1e-6
GK_PAD = 128

TM_GLA = 1024
TM_MIX = 1024
TM_OUT = 1024
TM_FFN = 512
TM_PRE = 2048
FF_CHUNK = 512
CT_NB = 16
CT_KB = 16
VMEM_LIMIT = 56 * 1024 * 1024


def _dot(a, b):
    return jnp.dot(a, b, preferred_element_type=F32)


def _dot_nt(a, b):
    return lax.dot_general(a, b, (((1,), (1,)), ((), ())), preferred_element_type=F32)


def _dot_tn(a, b):
    return lax.dot_general(a, b, (((0,), (0,)), ((), ())), preferred_element_type=F32)


def _sigmoid(x):
    return 1.0 / (1.0 + jnp.exp(-x))


def _silu(x):
    return x * _sigmoid(x)


def _rms(x):
    return x * lax.rsqrt(jnp.mean(x * x, axis=-1, keepdims=True) + EPS)


def _params(sem):
    return pltpu.CompilerParams(dimension_semantics=sem, vmem_limit_bytes=VMEM_LIMIT)


def _const_spec(shape):
    nd = len(shape)
    return pl.BlockSpec(shape, lambda *_: (0,) * nd, pipeline_mode=pl.Buffered(1))


def _layer_spec(layer, shape):
    nd = len(shape)
    return pl.BlockSpec((None,) + tuple(shape), lambda *_: (layer,) + (0,) * nd, pipeline_mode=pl.Buffered(1))


def _ada_kernel(c_ref, w_ref, b_ref, o_ref):
    s = _silu(c_ref[...]).astype(BF16)
    o_ref[...] = _dot(s, w_ref[...].astype(BF16)) + b_ref[...]


def _ada(cond, w_ada, b_ada):
    depth, d, n = w_ada.shape
    rows = cond.shape[0]
    nb = 1536
    return pl.pallas_call(
        _ada_kernel,
        grid=(depth, n // nb),
        in_specs=[
            pl.BlockSpec((rows, d), lambda l, j: (0, 0)),
            pl.BlockSpec((None, d, nb), lambda l, j: (l, 0, j)),
            pl.BlockSpec((None, 1, nb), lambda l, j: (l, 0, j)),
        ],
        out_specs=pl.BlockSpec((None, rows, nb), lambda l, j: (l, 0, j)),
        out_shape=jax.ShapeDtypeStruct((depth, rows, n), F32),
        compiler_params=_params(("arbitrary", "arbitrary")),
        name="ada",
    )(cond, w_ada, b_ada.reshape(depth, 1, n))


def _prenorm_kernel(xc_ref, xl_ref, mod_ref, n_ref, h_ref, *, n_ctx_tiles):
    x = jnp.where(pl.program_id(0) < n_ctx_tiles, xc_ref[...], xl_ref[...])
    m = mod_ref[...]
    h = _rms(x) * n_ref[...] * (1.0 + m[1:2]) + m[0:1]
    h_ref[...] = h.astype(BF16)


def _gla_in_kernel(h_ref, w_ref, wgk2_ref, bgk_ref,
                   og_ref, qin_ref, oin_ref, u_ref, g_ref,
                   q_s, k_s, v_s, la_s, pre_s):
    qk_w = N_HEADS * DK
    tm = h_ref.shape[0]
    h = h_ref[...]
    gk = _dot(h, w_ref[:, 2 * qk_w + 2 * V_W:]).astype(BF16)
    lp = _dot(gk, wgk2_ref[...]) + bgk_ref[...]
    la = (jnp.minimum(lp, 0.0) - jnp.log(1.0 + jnp.exp(-jnp.abs(lp)))) * (1.0 / GATE_NORMALIZER)
    la_s[...] = la
    pos = lax.broadcasted_iota(jnp.int32, (tm, 1), 0) & (CHUNK - 1)
    pre = la
    for sh in (1, 2, 4, 8, 16, 32):
        pre = pre + jnp.where(pos >= sh, pltpu.roll(pre, sh, axis=0), 0.0)
    pre_s[...] = pre

    z = _dot(h, w_ref[:, :2 * qk_w + 2 * V_W])
    lo_half = lax.broadcasted_iota(jnp.int32, (1, 2 * DK), 1) < DK

    def both_directions(x):
        tiles = []
        for p in range(N_HEADS // 2):
            t = x[:, p * 2 * DK:(p + 1) * 2 * DK]
            r = pltpu.roll(t, DK, axis=1)
            tiles += [jnp.where(lo_half, t, r), jnp.where(lo_half, r, t)]
        return jnp.concatenate(tiles, axis=1)

    q_s[...] = both_directions(z[:, :qk_w]) * (DK ** -0.5)
    k_s[...] = both_directions(z[:, qk_w:2 * qk_w])
    v_s[...] = z[:, 2 * qk_w:2 * qk_w + V_W].astype(BF16)
    og_ref[...] = z[:, 2 * qk_w + V_W:2 * qk_w + 2 * V_W].astype(BF16)

    row = lax.broadcasted_iota(jnp.int32, (SUB, SUB), 0)
    col = lax.broadcasted_iota(jnp.int32, (SUB, SUB), 1)
    same = (row & -CHUNK) == (col & -CHUNK)
    lower = same & (col <= row)
    upper = same & (col >= row)
    is_f = (lax.broadcasted_iota(jnp.int32, (SUB, FB_W), 1) & (2 * DK - 1)) < DK
    rchunk = lax.broadcasted_iota(jnp.int32, (SUB, 2 * DK), 0) & -CHUNK

    def sub_tile(s, carry):
        r0 = pl.multiple_of(s * SUB, SUB)
        rows = pl.ds(r0, SUB)
        la = la_s[rows, :]
        pre = pre_s[rows, :]
        tot_rows = [pre[c * CHUNK + CHUNK - 1:c * CHUNK + CHUNK, :] for c in range(CPS)]
        tot = jnp.concatenate([jnp.broadcast_to(t, (CHUNK, FB_W)) for t in tot_rows], axis=0)
        b = jnp.where(is_f, pre, tot - pre + la)
        q = q_s[rows, :]
        k = k_s[rows, :]
        qin = (q * jnp.exp(b)).astype(BF16)
        kin = (k * jnp.exp(-b)).astype(BF16)
        kout = (k * jnp.exp(tot - b)).astype(BF16)
        qin_ref[rows, :] = qin
        zero = jnp.zeros_like(qin)
        qf = jnp.where(is_f, qin, zero)
        qb = jnp.where(is_f, zero, qin)
        v = v_s[rows, :]
        for hh in range(N_HEADS):
            fb = slice(hh * 2 * DK, (hh + 1) * 2 * DK)
            vs = slice(hh * DV, (hh + 1) * DV)
            a2 = _dot_nt(jnp.concatenate([qf[:, fb], qb[:, fb]], axis=0), kin[:, fb])
            att = (jnp.where(lower, a2[:SUB], 0.0) + jnp.where(upper, a2[SUB:], 0.0)).astype(BF16)
            oin_ref[rows, vs] = _dot(att, v[:, vs]).astype(BF16)
            ko = kout[:, fb]
            kbd = jnp.concatenate([jnp.where(rchunk == c * CHUNK, ko, jnp.zeros_like(ko)) for c in range(CPS)],
                                  axis=1)
            u_ref[s, hh] = _dot_tn(v[:, vs], kbd).astype(BF16)
            g_ref[s, hh] = jnp.concatenate([jnp.exp(t[:, fb]) for t in tot_rows], axis=1)
        return carry

    lax.fori_loop(0, h_ref.shape[0] // SUB, sub_tile, 0, unroll=2)


def _mix_in_kernel(h_ref, w_ref, cw_ref, wb_ref,
                   fx_ref, g0_ref, g2_ref, yp_ref, *, n_ctx_tiles, period_ctx, period_lat):
    i = pl.program_id(0)
    d = g0_ref.shape[-1]
    z = _dot(h_ref[...], w_ref[...])
    sb, sc, sx = z[:, :SC_W], z[:, SC_W:2 * SC_W], z[:, 2 * SC_W:3 * SC_W]
    m0 = 3 * SC_W + FN_W
    u = sc * sx
    tm = h_ref.shape[0]
    period = jnp.where(i < n_ctx_tiles, period_ctx, period_lat)
    pos = lax.broadcasted_iota(jnp.int32, (tm, 1), 0) & (period - 1)
    up = jnp.where(pos == 0, 0.0, pltpu.roll(u, 1, axis=0))
    un = jnp.where(pos == period - 1, 0.0, pltpu.roll(u, tm - 1, axis=0))
    cw = cw_ref[...]
    conv = cw[0:1] * up + cw[1:2] * u + cw[2:3] * un
    yb = _dot((sb * conv).astype(BF16), wb_ref[...])

    _put_rows(fx_ref, 0, z[:, 3 * SC_W:m0])
    g0_ref[...] = _sigmoid(z[:, m0:m0 + d]).astype(BF16)
    yp_ref[...] = (_sigmoid(z[:, m0 + d:m0 + 2 * d]) * yb).astype(BF16)
    g2_ref[...] = _sigmoid(z[:, m0 + 2 * d:m0 + 3 * d]).astype(BF16)


def _scan_kernel(u_ref, g_ref, s0_ref, ss_ref, sfin_ref, *, n_sub):
    lane_f = lax.broadcasted_iota(jnp.int32, (DV, 2 * DK), 1) < DK
    for q in range(s0_ref.shape[0]):
        for hh in range(s0_ref.shape[1]):
            s0 = s0_ref[q, hh]

            def fwd(j, st):
                for c in range(CPS):
                    cs = slice(c * 2 * DK, (c + 1) * 2 * DK)
                    ss_ref[j, hh, :, cs] = st.astype(BF16)
                    st = g_ref[j, hh, :, cs] * st + u_ref[j, hh, :, cs].astype(F32)
                return st

            def bwd(j, st):
                for c in range(CPS - 1, -1, -1):
                    cs = slice(c * 2 * DK, (c + 1) * 2 * DK)
                    ss_ref[j, hh, :, cs] = jnp.where(lane_f, ss_ref[j, hh, :, cs], st.astype(BF16))
                    st = g_ref[j, hh, :, cs] * st + u_ref[j, hh, :, cs].astype(F32)
                return st

            lo = q * n_sub
            sf = lax.fori_loop(0, n_sub, lambda i, st: fwd(lo + i, st), s0)
            sb = lax.fori_loop(0, n_sub, lambda i, st: bwd(lo + n_sub - 1 - i, st), s0)
            sfin_ref[q, hh] = jnp.where(lane_f, sf, sb)


def _scan(u, g, s0, layer, n_seq, n_sub, sub0, seq_blk, head_blk):
    rows = seq_blk * n_sub
    assert n_seq % seq_blk == 0 and N_HEADS % head_blk == 0 and sub0 % rows == 0
    wide = CPS * 2 * DK
    return pl.pallas_call(
        functools.partial(_scan_kernel, n_sub=n_sub),
        grid=(n_seq // seq_blk, N_HEADS // head_blk),
        in_specs=[
            pl.BlockSpec((rows, head_blk, DV, wide), lambda b, hh: (sub0 // rows + b, hh, 0, 0)),
            pl.BlockSpec((rows, head_blk, 1, wide), lambda b, hh: (sub0 // rows + b, hh, 0, 0)),
            pl.BlockSpec((None, seq_blk, head_blk, DV, 2 * DK), lambda b, hh: (layer, b, hh, 0, 0)),
        ],
        out_specs=[
            pl.BlockSpec((rows, head_blk, DV, wide), lambda b, hh: (b, hh, 0, 0)),
            pl.BlockSpec((seq_blk, head_blk, DV, 2 * DK), lambda b, hh: (b, hh, 0, 0)),
        ],
        out_shape=[
            jax.ShapeDtypeStruct((n_seq * n_sub, N_HEADS, DV, wide), BF16),
            jax.ShapeDtypeStruct((n_seq, N_HEADS, DV, 2 * DK), F32),
        ],
        compiler_params=_params(("arbitrary", "arbitrary")),
        name="scan",
    )(u, g, s0)


def _gather_rows(ref, start, size, stride):
    return jnp.concatenate([ref[g, pl.ds(start, size, stride=stride), :] for g in range(FN_GROUPS)], axis=1)


def _put_rows(ref, r0, val):
    for g in range(FN_GROUPS):
        ref[g, r0:r0 + val.shape[0], :] = val[:, g * FN_GW:(g + 1) * FN_GW]


def _channel_dft_real(re_g, im_g, cs):
    return _dot(jnp.concatenate([re_g.astype(BF16), im_g.astype(BF16)], axis=1), cs)


def _dft_direct_kernel(x_ref, m_ref, cs_ref, o_ref, *, seq):
    for q in range(x_ref.shape[1] // seq):
        rows = slice(q * seq, (q + 1) * seq)
        x = jnp.concatenate([x_ref[g, rows, :] for g in range(FN_GROUPS)], axis=1)
        res = _dot(m_ref[...], x.astype(BF16))
        for g in range(FN_GROUPS):
            gs = slice(g * FN_GW, (g + 1) * FN_GW)
            o_ref[rows, gs] = _channel_dft_real(res[:seq, gs], res[seq:, gs], cs_ref[...]).astype(BF16)


def _pitch(n):
    return n + 4


def _ct_kernel(x_ref, m1_ref, tc_ref, ts_ref, m2_ref, cs_ref, o_ref, xs_s, yr_s, yi_s, tr_s, ti_s, *, l1, l2):
    j = pl.program_id(1)
    n_stage1 = l2 // CT_NB
    px, pt1, pt2 = _pitch(l2), _pitch(l1), _pitch(l2)

    @pl.when((pl.program_id(0) == 0) & (j == 0))
    def _():
        tr_s[...] = jnp.zeros_like(tr_s)
        ti_s[...] = jnp.zeros_like(ti_s)

    @pl.when(j == 0)
    def _():
        for n1 in range(l1):
            for g in range(FN_GROUPS):
                xs_s[g, n1 * px:n1 * px + l2, :] = x_ref[g, n1 * l2:(n1 + 1) * l2, :]

    @pl.when(j < n_stage1)
    def _():
        m = m1_ref[...]
        col0 = pl.multiple_of(j * CT_NB, CT_NB)
        for jn in range(CT_NB):
            xs = _gather_rows(xs_s, col0 + jn, l1, px)
            res = _dot(m, xs.astype(BF16))
            yr, yi = res[:l1], res[l1:]
            c = jnp.concatenate([tc_ref[col0 + jn]] * FN_GROUPS, axis=1)
            s = jnp.concatenate([ts_ref[col0 + jn]] * FN_GROUPS, axis=1)
            _put_rows(tr_s, jn * pt1, yr * c + yi * s)
            _put_rows(ti_s, jn * pt1, yi * c - yr * s)
        for k1 in range(l1):
            yr_s[k1, pl.ds(col0, CT_NB), :] = _gather_rows(tr_s, k1, CT_NB, pt1).astype(BF16)
            yi_s[k1, pl.ds(col0, CT_NB), :] = _gather_rows(ti_s, k1, CT_NB, pt1).astype(BF16)

    @pl.when(j >= n_stage1)
    def _():
        m = m2_ref[...]
        k0 = (j - n_stage1) * CT_KB
        for kk in range(CT_KB):
            rhs = jnp.concatenate([yr_s[k0 + kk], yi_s[k0 + kk]], axis=0)
            res = _dot(m, rhs)
            _put_rows(tr_s, kk * pt2, res[:l2])
            _put_rows(ti_s, kk * pt2, res[l2:])
        rows = CT_KB * pt2
        for g in range(FN_GROUPS):
            tr_s[g, :rows, :] = _channel_dft_real(tr_s[g, :rows, :], ti_s[g, :rows, :], cs_ref[...])
        for k2 in range(l2):
            o_ref[k2] = _gather_rows(tr_s, k2, CT_KB, pt2).astype(BF16)


def _cos_sin(n_out, n_in, period, scale):
    idx = (np.arange(n_out)[:, None] * np.arange(n_in)[None, :]) % period
    ang = 2.0 * np.pi * idx / period
    return np.cos(ang) * scale, np.sin(ang) * scale


def _mix_out_kernel(oin_ref, qin_ref, ssc_ref, ssl_ref, og_ref, frc_ref, frl_ref, g0_ref, g2_ref, yp_ref,
                    gn_ref, wa_ref, wc_ref, y_ref, o_s, *, n_ctx_tiles):
    is_ctx = pl.program_id(0) < n_ctx_tiles
    for s in range(oin_ref.shape[0] // SUB):
        for hh in range(N_HEADS):
            st = jnp.where(is_ctx, ssc_ref[s, hh], ssl_ref[s, hh])
            fb = slice(hh * 2 * DK, (hh + 1) * 2 * DK)
            vs = slice(hh * DV, (hh + 1) * DV)
            for c in range(CPS):
                rows = slice(s * SUB + c * CHUNK, s * SUB + (c + 1) * CHUNK)
                o_s[rows, vs] = oin_ref[rows, vs].astype(F32) + _dot_nt(qin_ref[rows, fb], st[:, c * 2 * DK:(c + 1) * 2 * DK])
    gn = gn_ref[...]
    parts = []
    for hh in range(N_HEADS):
        vs = slice(hh * DV, (hh + 1) * DV)
        parts.append((_rms(o_s[:, vs]) * gn * _silu(og_ref[:, vs].astype(F32))).astype(BF16))
    ya = _dot(jnp.concatenate(parts, axis=1), wa_ref[...])
    yc = _dot(jnp.where(is_ctx, frc_ref[...], frl_ref[...]), wc_ref[...])
    y = g0_ref[...].astype(F32) * ya + yp_ref[...].astype(F32) + g2_ref[...].astype(F32) * yc
    y_ref[...] = y.astype(BF16)


def _ffn_kernel(*refs, final, split_x, n_ctx_tiles):
    if split_x:
        xc_ref, xl_ref, *refs = refs
    else:
        x_ref, *refs = refs
    y_ref, wo_ref, wup_ref, wd_ref, mod_ref, modn_ref, n2_ref, nn_ref, *outs = refs
    is_ctx = pl.program_id(0) < n_ctx_tiles
    x = jnp.where(is_ctx, xc_ref[...], xl_ref[...]) if split_x else x_ref[...]
    m = mod_ref[...]
    x1 = x + m[2:3] * _dot(y_ref[...], wo_ref[...])
    h2 = (_rms(x1) * n2_ref[...] * (1.0 + m[4:5]) + m[3:4]).astype(BF16)
    d_ff = wd_ref.shape[0]
    acc = None
    for c0 in range(0, d_ff, FF_CHUNK):
        c1 = min(c0 + FF_CHUNK, d_ff)
        gate = _dot(h2, wup_ref[:, c0:c1])
        up = _dot(h2, wup_ref[:, d_ff + c0:d_ff + c1])
        part = _dot((_silu(gate) * up).astype(BF16), wd_ref[c0:c1, :])
        acc = part if acc is None else acc + part
    x2 = x1 + m[5:6] * acc
    if final:
        yc_ref, yl_ref = outs
        out = _rms(x2) * nn_ref[...]

        @pl.when(is_ctx)
        def _():
            yc_ref[...] = out

        @pl.when(jnp.logical_not(is_ctx))
        def _():
            yl_ref[...] = out
    else:
        x2_ref, hn_ref = outs
        x2_ref[...] = x2
        hn_ref[...] = (_rms(x2) * nn_ref[...] * (1.0 + modn_ref[1:2, :]) + modn_ref[0:1, :]).astype(BF16)


def kernel(x_prompt, x_sample, state_gla, c, c_ctx, w_ada, b_ada, norm1, norm2, w_in, w_gk_f, b_gk_f,
           w_gk_b, b_gk_b, gla_norm, w_a_out, conv_w, w_b_out, w_c_out, w_o, w_up, w_down, norm_f):
    b_ctx, seq, d = x_prompt.shape
    b_lat, dec_seq, _ = x_sample.shape
    depth = w_ada.shape[0]
    d_ff = w_down.shape[1]
    nc, nl = b_ctx * seq, b_lat * dec_seq
    nt = nc + nl
    l1, l2 = dec_seq // GRID_W, GRID_W
    assert seq == SUB and d_ff % 128 == 0
    assert nc % dec_seq == 0 and l2 % CT_NB == 0 and l1 % CT_KB == 0
    n_sub_tot, n_sub_lat = nt // SUB, dec_seq // SUB
    assert (nc // SUB) % n_sub_lat == 0
    n_cond = -(-(1 + b_lat) // 8) * 8

    class Tiling:
        def __init__(self, tm):
            assert nc % tm == 0 and dec_seq % tm == 0
            self.tm, self.n, self.n_ctx, self.per_lat = tm, nt // tm, nc // tm, dec_seq // tm

        def cond(self, i):
            return jnp.where(i < self.n_ctx, 0, 1 + (i - self.n_ctx) // self.per_lat)

        def ctx_blk(self, i):
            return jnp.minimum(i, self.n_ctx - 1)

        def lat_blk(self, i):
            return jnp.maximum(i - self.n_ctx, 0)

        def tok(self, w):
            return pl.BlockSpec((self.tm, w), lambda i, *_: (i, 0))

        def tok_ctx(self, w):
            return pl.BlockSpec((self.tm, w), lambda i, *_: (self.ctx_blk(i), 0))

        def tok_lat(self, w):
            return pl.BlockSpec((self.tm, w), lambda i, *_: (self.lat_blk(i), 0))

        def mod(self, layer):
            return pl.BlockSpec((None, None, N_MOD, d), lambda i, *_: (layer, self.cond(i), 0, 0))

    t_gla, t_mix, t_out, t_ffn = Tiling(TM_GLA), Tiling(TM_MIX), Tiling(TM_OUT), Tiling(TM_FFN)
    t_pre = Tiling(math.gcd(math.gcd(nc, dec_seq), TM_PRE))

    cond = jnp.concatenate([c_ctx[None, :], c, jnp.zeros((n_cond - 1 - b_lat, d), F32)], axis=0)
    mods = _ada(cond, w_ada, b_ada).reshape(depth, n_cond, N_MOD, d)

    cc, sc_ = _cos_sin(FN_GW, FN_GW, FN_GW, FN_GW ** -0.5)
    cs_tab = jnp.asarray(np.concatenate([cc, sc_], axis=0), F32).astype(BF16)
    cl, sl = _cos_sin(seq, seq, seq, seq ** -0.5)
    m_ctx_tab = jnp.asarray(np.concatenate([cl, -sl], axis=0), F32).astype(BF16)
    c1, s1 = _cos_sin(l1, l1, l1, l1 ** -0.5)
    m1_tab = jnp.asarray(np.concatenate([c1, -s1], axis=0), F32).astype(BF16)
    tcn, tsn = _cos_sin(l2, l1, l1 * l2, 1.0)
    tc_tab = jnp.asarray(np.repeat(tcn[:, :, None], 128, axis=2), F32)
    ts_tab = jnp.asarray(np.repeat(tsn[:, :, None], 128, axis=2), F32)
    c2, s2 = _cos_sin(l2, l2, l2, l2 ** -0.5)
    m2_tab = jnp.asarray(np.block([[c2, s2], [-s2, c2]]), F32).astype(BF16)

    xc, xl = x_prompt.reshape(nc, d), x_sample.reshape(nl, d)
    x = None

    gla_cols = 2 * N_HEADS * DK + 2 * V_W
    mix_col0 = gla_cols + 2 * LOWRANK
    w_in16 = w_in.astype(BF16)
    w_gla = w_in16[:, :, :gla_cols + GK_PAD]
    w_mix = w_in16[:, :, mix_col0:]
    zf = jnp.zeros((depth, LOWRANK, N_HEADS, DK), F32)
    top = jnp.stack([w_gk_f.reshape(depth, LOWRANK, N_HEADS, DK), zf], axis=3).reshape(depth, LOWRANK, FB_W)
    bot = jnp.stack([zf, w_gk_b.reshape(depth, LOWRANK, N_HEADS, DK)], axis=3).reshape(depth, LOWRANK, FB_W)
    wgk2 = jnp.concatenate([top, bot, jnp.zeros((depth, GK_PAD - 2 * LOWRANK, FB_W), F32)], axis=1).astype(BF16)
    bgk = jnp.stack([b_gk_f.reshape(depth, N_HEADS, DK), b_gk_b.reshape(depth, N_HEADS, DK)],
                    axis=2).reshape(depth, 1, FB_W)
    wb, wa, wc_out, wo = (t.astype(BF16) for t in (w_b_out, w_a_out, w_c_out, w_o))
    wup, wdn = w_up.astype(BF16), w_down.astype(BF16)
    n1, n2, gn = norm1[:, None, :], norm2[:, None, :], gla_norm[:, None, :]
    s0_ctx = jnp.zeros((1, b_ctx, N_HEADS, DV, 2 * DK), F32)
    s0_lat = state_gla.transpose(1, 0, 3, 5, 2, 4).reshape(depth, b_lat, N_HEADS, DV, 2 * DK)

    h = pl.pallas_call(
        functools.partial(_prenorm_kernel, n_ctx_tiles=t_pre.n_ctx),
        grid=(t_pre.n,),
        in_specs=[t_pre.tok_ctx(d), t_pre.tok_lat(d), t_pre.mod(0), _layer_spec(0, (1, d))],
        out_specs=t_pre.tok(d),
        out_shape=jax.ShapeDtypeStruct((nt, d), BF16),
        compiler_params=_params(("arbitrary",)),
        name="prenorm",
    )(xc, xl, mods, n1)

    new_states = []
    y_out = None
    for l in range(depth):
        tok = t_gla.tok
        og, qin, oin, u, g = pl.pallas_call(
            _gla_in_kernel,
            grid=(t_gla.n,),
            in_specs=[tok(d), _layer_spec(l, (d, gla_cols + GK_PAD)), _layer_spec(l, (GK_PAD, FB_W)),
                      _layer_spec(l, (1, FB_W))],
            out_specs=[tok(V_W), tok(FB_W), tok(V_W),
                       pl.BlockSpec((TM_GLA // SUB, N_HEADS, DV, CPS * 2 * DK), lambda i: (i, 0, 0, 0)),
                       pl.BlockSpec((TM_GLA // SUB, N_HEADS, 1, CPS * 2 * DK), lambda i: (i, 0, 0, 0))],
            out_shape=[jax.ShapeDtypeStruct((nt, V_W), BF16), jax.ShapeDtypeStruct((nt, FB_W), BF16),
                       jax.ShapeDtypeStruct((nt, V_W), BF16),
                       jax.ShapeDtypeStruct((n_sub_tot, N_HEADS, DV, CPS * 2 * DK), BF16),
                       jax.ShapeDtypeStruct((n_sub_tot, N_HEADS, 1, CPS * 2 * DK), F32)],
            scratch_shapes=[pltpu.VMEM((TM_GLA, FB_W), F32), pltpu.VMEM((TM_GLA, FB_W), F32),
                            pltpu.VMEM((TM_GLA, V_W), BF16), pltpu.VMEM((TM_GLA, FB_W), F32),
                            pltpu.VMEM((TM_GLA, FB_W), F32)],
            compiler_params=_params(("arbitrary",)),
            name="gla_in",
        )(h, w_gla, wgk2, bgk)

        tok = t_mix.tok
        fx, g0, g2, yp = pl.pallas_call(
            functools.partial(_mix_in_kernel, n_ctx_tiles=t_mix.n_ctx, period_ctx=seq, period_lat=GRID_W),
            grid=(t_mix.n,),
            in_specs=[tok(d), _layer_spec(l, (d, 3 * SC_W + FN_W + 3 * d)),
                      _layer_spec(l, (3, SC_W)), _layer_spec(l, (SC_W, d))],
            out_specs=[pl.BlockSpec((FN_GROUPS, TM_MIX, FN_GW), lambda i: (0, i, 0)), tok(d), tok(d), tok(d)],
            out_shape=[jax.ShapeDtypeStruct((FN_GROUPS, nt, FN_GW), F32)] + [jax.ShapeDtypeStruct((nt, d), BF16)] * 3,
            compiler_params=_params(("arbitrary",)),
            name="mix_in",
        )(h, w_mix, conv_w, wb)

        ss_ctx, sfin = _scan(u, g, s0_ctx, 0, b_ctx, seq // SUB, 0, math.gcd(b_ctx, 4), N_HEADS)
        ss_lat, _ = _scan(u, g, s0_lat, l, b_lat, n_sub_lat, nc // SUB, 1, 1)
        new_states.append(sfin)

        cs_spec = _const_spec((2 * FN_GW, FN_GW))
        sq = math.gcd(b_ctx, 4)
        fr_ctx = pl.pallas_call(
            functools.partial(_dft_direct_kernel, seq=seq),
            grid=(b_ctx // sq,),
            in_specs=[pl.BlockSpec((FN_GROUPS, sq * seq, FN_GW), lambda s: (0, s, 0)),
                      _const_spec((2 * seq, seq)), cs_spec],
            out_specs=pl.BlockSpec((sq * seq, FN_W), lambda s: (s, 0)),
            out_shape=jax.ShapeDtypeStruct((nc, FN_W), BF16),
            compiler_params=_params(("arbitrary",)),
            name="dft_ctx",
        )(fx, m_ctx_tab, cs_tab)
        lat_blk0 = nc // dec_seq
        n_st1, n_st2 = l2 // CT_NB, l1 // CT_KB
        t_rows = max(CT_NB * _pitch(l1), CT_KB * _pitch(l2))
        fr_lat = pl.pallas_call(
            functools.partial(_ct_kernel, l1=l1, l2=l2),
            grid=(b_lat, n_st1 + n_st2),
            in_specs=[pl.BlockSpec((FN_GROUPS, dec_seq, FN_GW), lambda b, j: (0, lat_blk0 + b, 0)),
                      _const_spec((2 * l1, l1)), _const_spec((l2, l1, 128)), _const_spec((l2, l1, 128)),
                      _const_spec((2 * l2, 2 * l2)), cs_spec],
            out_specs=pl.BlockSpec((None, l2, CT_KB, FN_W), lambda b, j: (b, 0, jnp.maximum(j - n_st1, 0), 0)),
            out_shape=jax.ShapeDtypeStruct((b_lat, l2, l1, FN_W), BF16),
            scratch_shapes=[pltpu.VMEM((FN_GROUPS, l1 * _pitch(l2), FN_GW), F32),
                            pltpu.VMEM((l1, l2, FN_W), BF16), pltpu.VMEM((l1, l2, FN_W), BF16),
                            pltpu.VMEM((FN_GROUPS, t_rows, FN_GW), F32), pltpu.VMEM((FN_GROUPS, t_rows, FN_GW), F32)],
            compiler_params=_params(("arbitrary", "arbitrary")),
            name="dft_lat",
        )(fx, m1_tab, tc_tab, ts_tab, m2_tab, cs_tab).reshape(nl, FN_W)

        tok = t_out.tok
        split_x = x is None
        ss_spec = lambda blk: pl.BlockSpec((TM_OUT // SUB, N_HEADS, DV, CPS * 2 * DK),
                                           lambda i: (blk(i), 0, 0, 0))
        y_mix = pl.pallas_call(
            functools.partial(_mix_out_kernel, n_ctx_tiles=t_out.n_ctx),
            grid=(t_out.n,),
            in_specs=[tok(V_W), tok(FB_W), ss_spec(t_out.ctx_blk), ss_spec(t_out.lat_blk), tok(V_W),
                      t_out.tok_ctx(FN_W), t_out.tok_lat(FN_W), tok(d), tok(d), tok(d),
                      _layer_spec(l, (1, DV)), _layer_spec(l, (V_W, d)), _layer_spec(l, (FN_W, d))],
            out_specs=tok(d),
            out_shape=jax.ShapeDtypeStruct((nt, d), BF16),
            scratch_shapes=[pltpu.VMEM((TM_OUT, V_W), F32)],
            compiler_params=_params(("arbitrary",)),
            name="mix_out",
        )(oin, qin, ss_ctx, ss_lat, og, fr_ctx, fr_lat, g0, g2, yp, gn, wa, wc_out)

        final = l == depth - 1
        tok = t_ffn.tok
        nxt = depth - 1 if final else l + 1
        outs = pl.pallas_call(
            functools.partial(_ffn_kernel, final=final, split_x=split_x, n_ctx_tiles=t_ffn.n_ctx),
            grid=(t_ffn.n,),
            in_specs=([t_ffn.tok_ctx(d), t_ffn.tok_lat(d)] if split_x else [tok(d)]) + [
                tok(d), _layer_spec(l, (d, d)), _layer_spec(l, (d, 2 * d_ff)), _layer_spec(l, (d_ff, d)),
                t_ffn.mod(l), t_ffn.mod(nxt), _layer_spec(l, (1, d)),
                _const_spec((1, d)) if final else _layer_spec(nxt, (1, d))],
            out_specs=[t_ffn.tok_ctx(d), t_ffn.tok_lat(d)] if final else [tok(d), tok(d)],
            out_shape=([jax.ShapeDtypeStruct((nc, d), F32), jax.ShapeDtypeStruct((nl, d), F32)] if final else
                       [jax.ShapeDtypeStruct((nt, d), F32), jax.ShapeDtypeStruct((nt, d), BF16)]),
            compiler_params=_params(("arbitrary",)),
            name="ffn",
        )(*((xc, xl) if split_x else (x,)), y_mix, wo, wup, wdn, mods, mods, n2, norm_f[None, :] if final else n1)
        if final:
            y_out = outs
        else:
            x, h = outs

    y_prompt = y_out[0].reshape(b_ctx, seq, d)
    y_sample = y_out[1].reshape(b_lat, dec_seq, d)
    sfin_all = jnp.stack(new_states, axis=0).reshape(depth, b_ctx, N_HEADS, DV, 2, DK)
    new_state_gla = sfin_all.transpose(1, 0, 4, 2, 5, 3).astype(x_prompt.dtype)
    return (y_prompt, y_sample, new_state_gla)
```

```python
import functools
import math

import numpy as np
import jax
import jax.numpy as jnp
from jax import lax
from jax.experimental import pallas as pl
from jax.experimental.pallas import tpu as pltpu

F32 = jnp.float32
BF16 = jnp.bfloat16

GRID_W = 64
N_HEADS = 4
DK = 64
DV = 128
V_W = N_HEADS * DV
FB_W = N_HEADS * 2 * DK
LOWRANK = 16
GATE_NORMALIZER = 16.0
CHUNK = 64
SUB = 256
CPS = SUB // CHUNK
SC_W = 512
FN_GROUPS = 4
FN_GW = 128
FN_W = FN_GROUPS * FN_GW
N_MOD = 6
EPS = 1e-6
GK_PAD = 128
SUBLANES = 8

TM_GLA = 1024
TM_MIX = 1024
TM_OUT = 1024
TM_FFN = 512
TM_PRE = 2048
FF_CHUNK = 512
CT_NB = 16
CT_KB = 16
VMEM_LIMIT = 56 * 1024 * 1024


def _dot(a, b):
    return jnp.dot(a, b, preferred_element_type=F32)


def _dot_nt(a, b):
    return lax.dot_general(a, b, (((1,), (1,)), ((), ())), preferred_element_type=F32)


def _dot_tn(a, b):
    return lax.dot_general(a, b, (((0,), (0,)), ((), ())), preferred_element_type=F32)


def _sigmoid(x):
    return 1.0 / (1.0 + jnp.exp(-x))


def _silu(x):
    return x * _sigmoid(x)


def _rms(x):
    return x * lax.rsqrt(jnp.mean(x * x, axis=-1, keepdims=True) + EPS)


def _params(sem):
    return pltpu.CompilerParams(dimension_semantics=sem, vmem_limit_bytes=VMEM_LIMIT)


def _const_spec(shape):
    nd = len(shape)
    return pl.BlockSpec(shape, lambda *_: (0,) * nd, pipeline_mode=pl.Buffered(1))


def _layer_spec(layer, shape):
    nd = len(shape)
    return pl.BlockSpec((None,) + tuple(shape), lambda *_: (layer,) + (0,) * nd, pipeline_mode=pl.Buffered(1))


def _ada_kernel(c_ref, w_ref, b_ref, o_ref):
    s = _silu(c_ref[...]).astype(BF16)
    o_ref[...] = _dot(s, w_ref[...].astype(BF16)) + b_ref[...]


def _ada(cond, w_ada, b_ada):
    depth, d, n = w_ada.shape
    rows = cond.shape[0]
    nb = 1536
    return pl.pallas_call(
        _ada_kernel,
        grid=(depth, n // nb),
        in_specs=[
            pl.BlockSpec((rows, d), lambda l, j: (0, 0)),
            pl.BlockSpec((None, d, nb), lambda l, j: (l, 0, j)),
            pl.BlockSpec((None, 1, nb), lambda l, j: (l, 0, j)),
        ],
        out_specs=pl.BlockSpec((None, rows, nb), lambda l, j: (l, 0, j)),
        out_shape=jax.ShapeDtypeStruct((depth, rows, n), F32),
        compiler_params=_params(("arbitrary", "arbitrary")),
        name="ada",
    )(cond, w_ada, b_ada.reshape(depth, 1, n))


def _prenorm_kernel(xc_ref, xl_ref, mod_ref, n_ref, h_ref, *, n_ctx_tiles):
    x = jnp.where(pl.program_id(0) < n_ctx_tiles, xc_ref[...], xl_ref[...])
    m = mod_ref[...]
    h = _rms(x) * n_ref[...] * (1.0 + m[1:2]) + m[0:1]
    h_ref[...] = h.astype(BF16)


def _gla_in_kernel(h_ref, w_ref, wgk2_ref, bgk_ref,
                   og_ref, qin_ref, oin_ref, u_ref, g_ref,
                   q_s, k_s, v_s, la_s, pre_s):
    qk_w = N_HEADS * DK
    tm = h_ref.shape[0]
    h = h_ref[...]
    gk = _dot(h, w_ref[:, 2 * qk_w + 2 * V_W:]).astype(BF16)
    lp = _dot(gk, wgk2_ref[...]) + bgk_ref[...]
    la = (jnp.minimum(lp, 0.0) - jnp.log(1.0 + jnp.exp(-jnp.abs(lp)))) * (1.0 / GATE_NORMALIZER)
    la_s[...] = la
    z = _dot(h, w_ref[:, :2 * qk_w + 2 * V_W])
    srow = lax.broadcasted_iota(jnp.int32, (1, SUBLANES, 1), 1)

    def chunk_prefix(x):
        x = x.reshape(SUB // SUBLANES, SUBLANES, FB_W)
        for sh in (1, 2, 4):
            x = x + jnp.where(srow >= sh, pltpu.roll(x, sh, axis=1), 0.0)
        groups = [x[i] for i in range(SUB // SUBLANES)]
        per_chunk = CHUNK // SUBLANES
        for i in range(len(groups)):
            if i % per_chunk:
                groups[i] = groups[i] + jnp.broadcast_to(groups[i - 1][SUBLANES - 1:SUBLANES, :], (SUBLANES, FB_W))
        return jnp.concatenate(groups, axis=0)

    lo_half = lax.broadcasted_iota(jnp.int32, (1, 2 * DK), 1) < DK

    def both_directions(x):
        tiles = []
        for p in range(N_HEADS // 2):
            t = x[:, p * 2 * DK:(p + 1) * 2 * DK]
            r = pltpu.roll(t, DK, axis=1)
            tiles += [jnp.where(lo_half, t, r), jnp.where(lo_half, r, t)]
        return jnp.concatenate(tiles, axis=1)

    q_s[...] = both_directions(z[:, :qk_w]) * (DK ** -0.5)
    k_s[...] = both_directions(z[:, qk_w:2 * qk_w])
    v_s[...] = z[:, 2 * qk_w:2 * qk_w + V_W].astype(BF16)
    og_ref[...] = z[:, 2 * qk_w + V_W:].astype(BF16)
    n_sub = tm // SUB
    pre_s[0] = chunk_prefix(la_s[0:SUB, :])

    row = lax.broadcasted_iota(jnp.int32, (SUB, SUB), 0)
    col = lax.broadcasted_iota(jnp.int32, (SUB, SUB), 1)
    same = (row & -CHUNK) == (col & -CHUNK)
    lower = same & (col <= row)
    upper = same & (col >= row)
    is_f = (lax.broadcasted_iota(jnp.int32, (SUB, FB_W), 1) & (2 * DK - 1)) < DK
    rchunk = lax.broadcasted_iota(jnp.int32, (SUB, 2 * DK), 0) & -CHUNK

    def sub_tile(s, carry):
        r0 = pl.multiple_of(s * SUB, SUB)
        rows = pl.ds(r0, SUB)
        la = la_s[rows, :]
        pre = pre_s[s & 1]
        nxt = pl.multiple_of(jnp.minimum(s + 1, n_sub - 1) * SUB, SUB)
        pre_s[(s + 1) & 1] = chunk_prefix(la_s[pl.ds(nxt, SUB), :])
        tot_rows = [pre[c * CHUNK + CHUNK - 1:c * CHUNK + CHUNK, :] for c in range(CPS)]
        tot = jnp.concatenate([jnp.broadcast_to(t, (CHUNK, FB_W)) for t in tot_rows], axis=0)
        b = jnp.where(is_f, pre, tot - pre + la)
        q = q_s[rows, :]
        k = k_s[rows, :]
        qin = (q * jnp.exp(b)).astype(BF16)
        kin = (k * jnp.exp(-b)).astype(BF16)
        kout = (k * jnp.exp(tot - b)).astype(BF16)
        qin_ref[rows, :] = qin
        zero = jnp.zeros_like(qin)
        qf = jnp.where(is_f, qin, zero)
        qb = jnp.where(is_f, zero, qin)
        v = v_s[rows, :]
        for hh in range(N_HEADS):
            fb = slice(hh * 2 * DK, (hh + 1) * 2 * DK)
            vs = slice(hh * DV, (hh + 1) * DV)
            a2 = _dot_nt(jnp.concatenate([qf[:, fb], qb[:, fb]], axis=0), kin[:, fb])
            att = (jnp.where(lower, a2[:SUB], 0.0) + jnp.where(upper, a2[SUB:], 0.0)).astype(BF16)
            oin_ref[rows, vs] = _dot(att, v[:, vs]).astype(BF16)
            ko = kout[:, fb]
            kbd = jnp.concatenate([jnp.where(rchunk == c * CHUNK, ko, jnp.zeros_like(ko)) for c in range(CPS)],
                                  axis=1)
            u_ref[s, hh] = _dot_tn(v[:, vs], kbd).astype(BF16)
            g_ref[s, hh] = jnp.concatenate([jnp.exp(t[:, fb]) for t in tot_rows], axis=1)
        return carry

    lax.fori_loop(0, h_ref.shape[0] // SUB, sub_tile, 0, unroll=2)


def _mix_in_kernel(h_ref, w_ref, cw_ref, wb_ref,
                   fx_ref, g0_ref, g2_ref, yp_ref, *, n_ctx_tiles, period_ctx, period_lat):
    i = pl.program_id(0)
    d = g0_ref.shape[-1]
    z = _dot(h_ref[...], w_ref[...])
    sb, sc, sx = z[:, :SC_W], z[:, SC_W:2 * SC_W], z[:, 2 * SC_W:3 * SC_W]
    m0 = 3 * SC_W + FN_W
    u = sc * sx
    tm = h_ref.shape[0]
    period = jnp.where(i < n_ctx_tiles, period_ctx, period_lat)
    pos = lax.broadcasted_iota(jnp.int32, (tm, 1), 0) & (period - 1)
    up = jnp.where(pos == 0, 0.0, pltpu.roll(u, 1, axis=0))
    un = jnp.where(pos == period - 1, 0.0, pltpu.roll(u, tm - 1, axis=0))
    cw = cw_ref[...]
    conv = cw[0:1] * up + cw[1:2] * u + cw[2:3] * un
    yb = _dot((sb * conv).astype(BF16), wb_ref[...])

    _put_rows(fx_ref, 0, z[:, 3 * SC_W:m0])
    g0_ref[...] = _sigmoid(z[:, m0:m0 + d]).astype(BF16)
    yp_ref[...] = (_sigmoid(z[:, m0 + d:m0 + 2 * d]) * yb).astype(BF16)
    g2_ref[...] = _sigmoid(z[:, m0 + 2 * d:m0 + 3 * d]).astype(BF16)


def _scan_kernel(u_ref, g_ref, s0_ref, ss_ref, sfin_ref, *, n_sub):
    lane_f = lax.broadcasted_iota(jnp.int32, (DV, 2 * DK), 1) < DK
    for q in range(s0_ref.shape[0]):
        for hh in range(s0_ref.shape[1]):
            s0 = s0_ref[q, hh]

            def fwd(j, st):
                for c in range(CPS):
                    cs = slice(c * 2 * DK, (c + 1) * 2 * DK)
                    ss_ref[j, hh, :, cs] = st.astype(BF16)
                    st = g_ref[j, hh, :, cs] * st + u_ref[j, hh, :, cs].astype(F32)
                return st

            def bwd(j, st):
                for c in range(CPS - 1, -1, -1):
                    cs = slice(c * 2 * DK, (c + 1) * 2 * DK)
                    ss_ref[j, hh, :, cs] = jnp.where(lane_f, ss_ref[j, hh, :, cs], st.astype(BF16))
                    st = g_ref[j, hh, :, cs] * st + u_ref[j, hh, :, cs].astype(F32)
                return st

            lo = q * n_sub
            sf = lax.fori_loop(0, n_sub, lambda i, st: fwd(lo + i, st), s0)
            sb = lax.fori_loop(0, n_sub, lambda i, st: bwd(lo + n_sub - 1 - i, st), s0)
            sfin_ref[q, hh] = jnp.where(lane_f, sf, sb)


def _scan(u, g, s0, layer, n_seq, n_sub, sub0, seq_blk, head_blk):
    rows = seq_blk * n_sub
    assert n_seq % seq_blk == 0 and N_HEADS % head_blk == 0 and sub0 % rows == 0
    wide = CPS * 2 * DK
    return pl.pallas_call(
        functools.partial(_scan_kernel, n_sub=n_sub),
        grid=(n_seq // seq_blk, N_HEADS // head_blk),
        in_specs=[
            pl.BlockSpec((rows, head_blk, DV, wide), lambda b, hh: (sub0 // rows + b, hh, 0, 0)),
            pl.BlockSpec((rows, head_blk, 1, wide), lambda b, hh: (sub0 // rows + b, hh, 0, 0)),
            pl.BlockSpec((None, seq_blk, head_blk, DV, 2 * DK), lambda b, hh: (layer, b, hh, 0, 0)),
        ],
        out_specs=[
            pl.BlockSpec((rows, head_blk, DV, wide), lambda b, hh: (b, hh, 0, 0)),
            pl.BlockSpec((seq_blk, head_blk, DV, 2 * DK), lambda b, hh: (b, hh, 0, 0)),
        ],
        out_shape=[
            jax.ShapeDtypeStruct((n_seq * n_sub, N_HEADS, DV, wide), BF16),
            jax.ShapeDtypeStruct((n_seq, N_HEADS, DV, 2 * DK), F32),
        ],
        compiler_params=_params(("arbitrary", "arbitrary")),
        name="scan",
    )(u, g, s0)


def _gather_rows(ref, start, size, stride):
    return jnp.concatenate([ref[g, pl.ds(start, size, stride=stride), :] for g in range(FN_GROUPS)], axis=1)


def _put_rows(ref, r0, val):
    for g in range(FN_GROUPS):
        ref[g, r0:r0 + val.shape[0], :] = val[:, g * FN_GW:(g + 1) * FN_GW]


def _channel_dft_real(re_g, im_g, cs):
    return _dot(jnp.concatenate([re_g.astype(BF16), im_g.astype(BF16)], axis=1), cs)


def _dft_direct_kernel(x_ref, m_ref, cs_ref, o_ref, *, seq):
    for q in range(x_ref.shape[1] // seq):
        rows = slice(q * seq, (q + 1) * seq)
        x = jnp.concatenate([x_ref[g, rows, :] for g in range(FN_GROUPS)], axis=1)
        res = _dot(m_ref[...], x.astype(BF16))
        for g in range(FN_GROUPS):
            gs = slice(g * FN_GW, (g + 1) * FN_GW)
            o_ref[rows, gs] = _channel_dft_real(res[:seq, gs], res[seq:, gs], cs_ref[...]).astype(BF16)


def _pitch(n):
    return n + 4


def _ct_kernel(x_ref, m1_ref, m2_ref, cs_ref, o_ref, xs_s, yr_s, yi_s, tr_s, ti_s, *, l1, l2):
    j = pl.program_id(1)
    n_stage1 = l2 // CT_NB
    px, pt1, pt2 = _pitch(l2), _pitch(l1), _pitch(l2)

    @pl.when((pl.program_id(0) == 0) & (j == 0))
    def _():
        tr_s[...] = jnp.zeros_like(tr_s)
        ti_s[...] = jnp.zeros_like(ti_s)

    @pl.when(j == 0)
    def _():
        for n1 in range(l1):
            for g in range(FN_GROUPS):
                xs_s[g, n1 * px:n1 * px + l2, :] = x_ref[g, n1 * l2:(n1 + 1) * l2, :]

    @pl.when(j < n_stage1)
    def _():
        col0 = pl.multiple_of(j * CT_NB, CT_NB)
        for jn in range(CT_NB):
            xs = _gather_rows(xs_s, col0 + jn, l1, px)
            res = _dot(m1_ref[col0 + jn], xs.astype(BF16))
            _put_rows(tr_s, jn * pt1, res[:l1])
            _put_rows(ti_s, jn * pt1, res[l1:])
        for k1 in range(l1):
            yr_s[k1, pl.ds(col0, CT_NB), :] = _gather_rows(tr_s, k1, CT_NB, pt1).astype(BF16)
            yi_s[k1, pl.ds(col0, CT_NB), :] = _gather_rows(ti_s, k1, CT_NB, pt1).astype(BF16)

    @pl.when(j >= n_stage1)
    def _():
        m = m2_ref[...]
        k0 = (j - n_stage1) * CT_KB
        for kk in range(CT_KB):
            rhs = jnp.concatenate([yr_s[k0 + kk], yi_s[k0 + kk]], axis=0)
            res = _dot(m, rhs)
            _put_rows(tr_s, kk * pt2, res[:l2])
            _put_rows(ti_s, kk * pt2, res[l2:])
        rows = CT_KB * pt2
        for g in range(FN_GROUPS):
            tr_s[g, :rows, :] = _channel_dft_real(tr_s[g, :rows, :], ti_s[g, :rows, :], cs_ref[...])
        for k2 in range(l2):
            o_ref[k2] = _gather_rows(tr_s, k2, CT_KB, pt2).astype(BF16)


def _cos_sin(n_out, n_in, period, scale):
    idx = (np.arange(n_out)[:, None] * np.arange(n_in)[None, :]) % period
    ang = 2.0 * np.pi * idx / period
    return np.cos(ang) * scale, np.sin(ang) * scale


def _mix_out_kernel(oin_ref, qin_ref, ssc_ref, ssl_ref, og_ref, frc_ref, frl_ref, g0_ref, g2_ref, yp_ref,
                    gn_ref, wa_ref, wc_ref, y_ref, o_s, *, n_ctx_tiles):
    is_ctx = pl.program_id(0) < n_ctx_tiles
    for s in range(oin_ref.shape[0] // SUB):
        for hh in range(N_HEADS):
            st = jnp.where(is_ctx, ssc_ref[s, hh], ssl_ref[s, hh])
            fb = slice(hh * 2 * DK, (hh + 1) * 2 * DK)
            vs = slice(hh * DV, (hh + 1) * DV)
            for c in range(CPS):
                rows = slice(s * SUB + c * CHUNK, s * SUB + (c + 1) * CHUNK)
                o_s[rows, vs] = oin_ref[rows, vs].astype(F32) + _dot_nt(qin_ref[rows, fb], st[:, c * 2 * DK:(c + 1) * 2 * DK])
    gn = gn_ref[...]
    parts = []
    for hh in range(N_HEADS):
        vs = slice(hh * DV, (hh + 1) * DV)
        parts.append((_rms(o_s[:, vs]) * gn * _silu(og_ref[:, vs].astype(F32))).astype(BF16))
    ya = _dot(jnp.concatenate(parts, axis=1), wa_ref[...])
    yc = _dot(jnp.where(is_ctx, frc_ref[...], frl_ref[...]), wc_ref[...])
    y = g0_ref[...].astype(F32) * ya + yp_ref[...].astype(F32) + g2_ref[...].astype(F32) * yc
    y_ref[...] = y.astype(BF16)


def _ffn_kernel(*refs, final, split_x, n_ctx_tiles):
    if split_x:
        xc_ref, xl_ref, *refs = refs
    else:
        x_ref, *refs = refs
    y_ref, wo_ref, wup_ref, wd_ref, mod_ref, modn_ref, n2_ref, nn_ref, *outs = refs
    is_ctx = pl.program_id(0) < n_ctx_tiles
    x = jnp.where(is_ctx, xc_ref[...], xl_ref[...]) if split_x else x_ref[...]
    m = mod_ref[...]
    x1 = x + m[2:3] * _dot(y_ref[...], wo_ref[...])
    h2 = (_rms(x1) * n2_ref[...] * (1.0 + m[4:5]) + m[3:4]).astype(BF16)
    d_ff = wd_ref.shape[0]
    acc = None
    for c0 in range(0, d_ff, FF_CHUNK):
        c1 = min(c0 + FF_CHUNK, d_ff)
        gate = _dot(h2, wup_ref[:, c0:c1])
        up = _dot(h2, wup_ref[:, d_ff + c0:d_ff + c1])
        part = _dot((_silu(gate) * up).astype(BF16), wd_ref[c0:c1, :])
        acc = part if acc is None else acc + part
    x2 = x1 + m[5:6] * acc
    if final:
        yc_ref, yl_ref = outs
        out = _rms(x2) * nn_ref[...]

        @pl.when(is_ctx)
        def _():
            yc_ref[...] = out

        @pl.when(jnp.logical_not(is_ctx))
        def _():
            yl_ref[...] = out
    else:
        x2_ref, hn_ref = outs
        x2_ref[...] = x2
        hn_ref[...] = (_rms(x2) * nn_ref[...] * (1.0 + modn_ref[1:2, :]) + modn_ref[0:1, :]).astype(BF16)


def kernel(x_prompt, x_sample, state_gla, c, c_ctx, w_ada, b_ada, norm1, norm2, w_in, w_gk_f, b_gk_f,
           w_gk_b, b_gk_b, gla_norm, w_a_out, conv_w, w_b_out, w_c_out, w_o, w_up, w_down, norm_f):
    b_ctx, seq, d = x_prompt.shape
    b_lat, dec_seq, _ = x_sample.shape
    depth = w_ada.shape[0]
    d_ff = w_down.shape[1]
    nc, nl = b_ctx * seq, b_lat * dec_seq
    nt = nc + nl
    l1, l2 = dec_seq // GRID_W, GRID_W
    assert seq == SUB and d_ff % 128 == 0
    assert nc % dec_seq == 0 and l2 % CT_NB == 0 and l1 % CT_KB == 0
    n_sub_tot, n_sub_lat = nt // SUB, dec_seq // SUB
    assert (nc // SUB) % n_sub_lat == 0
    n_cond = -(-(1 + b_lat) // 8) * 8

    class Tiling:
        def __init__(self, tm):
            assert nc % tm == 0 and dec_seq % tm == 0
            self.tm, self.n, self.n_ctx, self.per_lat = tm, nt // tm, nc // tm, dec_seq // tm

        def cond(self, i):
            return jnp.where(i < self.n_ctx, 0, 1 + (i - self.n_ctx) // self.per_lat)

        def ctx_blk(self, i):
            return jnp.minimum(i, self.n_ctx - 1)

        def lat_blk(self, i):
            return jnp.maximum(i - self.n_ctx, 0)

        def tok(self, w):
            return pl.BlockSpec((self.tm, w), lambda i, *_: (i, 0))

        def tok_ctx(self, w):
            return pl.BlockSpec((self.tm, w), lambda i, *_: (self.ctx_blk(i), 0))

        def tok_lat(self, w):
            return pl.BlockSpec((self.tm, w), lambda i, *_: (self.lat_blk(i), 0))

        def mod(self, layer):
            return pl.BlockSpec((None, None, N_MOD, d), lambda i, *_: (layer, self.cond(i), 0, 0))

    t_gla, t_mix, t_out, t_ffn = Tiling(TM_GLA), Tiling(TM_MIX), Tiling(TM_OUT), Tiling(TM_FFN)
    t_pre = Tiling(math.gcd(math.gcd(nc, dec_seq), TM_PRE))

    cond = jnp.concatenate([c_ctx[None, :], c, jnp.zeros((n_cond - 1 - b_lat, d), F32)], axis=0)
    mods = _ada(cond, w_ada, b_ada).reshape(depth, n_cond, N_MOD, d)

    cc, sc_ = _cos_sin(FN_GW, FN_GW, FN_GW, FN_GW ** -0.5)
    cs_tab = jnp.asarray(np.concatenate([cc, sc_], axis=0), F32).astype(BF16)
    cl, sl = _cos_sin(seq, seq, seq, seq ** -0.5)
    m_ctx_tab = jnp.asarray(np.concatenate([cl, -sl], axis=0), F32).astype(BF16)
    ang = 2.0 * np.pi * ((np.arange(l1)[None, :, None] * (l2 * np.arange(l1)[None, None, :]
                                                           + np.arange(l2)[:, None, None])) % (l1 * l2)) / (l1 * l2)
    m1_tab = jnp.asarray(np.concatenate([np.cos(ang), -np.sin(ang)], axis=1) * l1 ** -0.5, F32).astype(BF16)
    c2, s2 = _cos_sin(l2, l2, l2, l2 ** -0.5)
    m2_tab = jnp.asarray(np.block([[c2, s2], [-s2, c2]]), F32).astype(BF16)

    xc, xl = x_prompt.reshape(nc, d), x_sample.reshape(nl, d)
    x = None

    gla_cols = 2 * N_HEADS * DK + 2 * V_W
    mix_col0 = gla_cols + 2 * LOWRANK
    w_in16 = w_in.astype(BF16)
    w_gla = w_in16[:, :, :gla_cols + GK_PAD]
    w_mix = w_in16[:, :, mix_col0:]
    zf = jnp.zeros((depth, LOWRANK, N_HEADS, DK), F32)
    top = jnp.stack([w_gk_f.reshape(depth, LOWRANK, N_HEADS, DK), zf], axis=3).reshape(depth, LOWRANK, FB_W)
    bot = jnp.stack([zf, w_gk_b.reshape(depth, LOWRANK, N_HEADS, DK)], axis=3).reshape(depth, LOWRANK, FB_W)
    wgk2 = jnp.concatenate([top, bot, jnp.zeros((depth, GK_PAD - 2 * LOWRANK, FB_W), F32)], axis=1).astype(BF16)
    bgk = jnp.stack([b_gk_f.reshape(depth, N_HEADS, DK), b_gk_b.reshape(depth, N_HEADS, DK)],
                    axis=2).reshape(depth, 1, FB_W)
    wb, wa, wc_out, wo = (t.astype(BF16) for t in (w_b_out, w_a_out, w_c_out, w_o))
    wup, wdn = w_up.astype(BF16), w_down.astype(BF16)
    n1, n2, gn = norm1[:, None, :], norm2[:, None, :], gla_norm[:, None, :]
    s0_ctx = jnp.zeros((1, b_ctx, N_HEADS, DV, 2 * DK), F32)
    s0_lat = state_gla.transpose(1, 0, 3, 5, 2, 4).reshape(depth, b_lat, N_HEADS, DV, 2 * DK)

    h = pl.pallas_call(
        functools.partial(_prenorm_kernel, n_ctx_tiles=t_pre.n_ctx),
        grid=(t_pre.n,),
        in_specs=[t_pre.tok_ctx(d), t_pre.tok_lat(d), t_pre.mod(0), _layer_spec(0, (1, d))],
        out_specs=t_pre.tok(d),
        out_shape=jax.ShapeDtypeStruct((nt, d), BF16),
        compiler_params=_params(("arbitrary",)),
        name="prenorm",
    )(xc, xl, mods, n1)

    new_states = []
    y_out = None
    for l in range(depth):
        tok = t_gla.tok
        og, qin, oin, u, g = pl.pallas_call(
            _gla_in_kernel,
            grid=(t_gla.n,),
            in_specs=[tok(d), _layer_spec(l, (d, gla_cols + GK_PAD)), _layer_spec(l, (GK_PAD, FB_W)),
                      _layer_spec(l, (1, FB_W))],
            out_specs=[tok(V_W), tok(FB_W), tok(V_W),
                       pl.BlockSpec((TM_GLA // SUB, N_HEADS, DV, CPS * 2 * DK), lambda i: (i, 0, 0, 0)),
                       pl.BlockSpec((TM_GLA // SUB, N_HEADS, 1, CPS * 2 * DK), lambda i: (i, 0, 0, 0))],
            out_shape=[jax.ShapeDtypeStruct((nt, V_W), BF16), jax.ShapeDtypeStruct((nt, FB_W), BF16),
                       jax.ShapeDtypeStruct((nt, V_W), BF16),
                       jax.ShapeDtypeStruct((n_sub_tot, N_HEADS, DV, CPS * 2 * DK), BF16),
                       jax.ShapeDtypeStruct((n_sub_tot, N_HEADS, 1, CPS * 2 * DK), F32)],
            scratch_shapes=[pltpu.VMEM((TM_GLA, FB_W), F32), pltpu.VMEM((TM_GLA, FB_W), F32),
                            pltpu.VMEM((TM_GLA, V_W), BF16), pltpu.VMEM((TM_GLA, FB_W), F32),
                            pltpu.VMEM((2, SUB, FB_W), F32)],
            compiler_params=_params(("arbitrary",)),
            name="gla_in",
        )(h, w_gla, wgk2, bgk)

        tok = t_mix.tok
        fx, g0, g2, yp = pl.pallas_call(
            functools.partial(_mix_in_kernel, n_ctx_tiles=t_mix.n_ctx, period_ctx=seq, period_lat=GRID_W),
            grid=(t_mix.n,),
            in_specs=[tok(d), _layer_spec(l, (d, 3 * SC_W + FN_W + 3 * d)),
                      _layer_spec(l, (3, SC_W)), _layer_spec(l, (SC_W, d))],
            out_specs=[pl.BlockSpec((FN_GROUPS, TM_MIX, FN_GW), lambda i: (0, i, 0)), tok(d), tok(d), tok(d)],
            out_shape=[jax.ShapeDtypeStruct((FN_GROUPS, nt, FN_GW), F32)] + [jax.ShapeDtypeStruct((nt, d), BF16)] * 3,
            compiler_params=_params(("arbitrary",)),
            name="mix_in",
        )(h, w_mix, conv_w, wb)

        ss_ctx, sfin = _scan(u, g, s0_ctx, 0, b_ctx, seq // SUB, 0, math.gcd(b_ctx, 4), N_HEADS)
        ss_lat, _ = _scan(u, g, s0_lat, l, b_lat, n_sub_lat, nc // SUB, 1, 1)
        new_states.append(sfin)

        cs_spec = _const_spec((2 * FN_GW, FN_GW))
        sq = math.gcd(b_ctx, 4)
        fr_ctx = pl.pallas_call(
            functools.partial(_dft_direct_kernel, seq=seq),
            grid=(b_ctx // sq,),
            in_specs=[pl.BlockSpec((FN_GROUPS, sq * seq, FN_GW), lambda s: (0, s, 0)),
                      _const_spec((2 * seq, seq)), cs_spec],
            out_specs=pl.BlockSpec((sq * seq, FN_W), lambda s: (s, 0)),
            out_shape=jax.ShapeDtypeStruct((nc, FN_W), BF16),
            compiler_params=_params(("arbitrary",)),
            name="dft_ctx",
        )(fx, m_ctx_tab, cs_tab)
        lat_blk0 = nc // dec_seq
        n_st1, n_st2 = l2 // CT_NB, l1 // CT_KB
        t_rows = max(CT_NB * _pitch(l1), CT_KB * _pitch(l2))
        fr_lat = pl.pallas_call(
            functools.partial(_ct_kernel, l1=l1, l2=l2),
            grid=(b_lat, n_st1 + n_st2),
            in_specs=[pl.BlockSpec((FN_GROUPS, dec_seq, FN_GW), lambda b, j: (0, lat_blk0 + b, 0)),
                      _const_spec((l2, 2 * l1, l1)), _const_spec((2 * l2, 2 * l2)), cs_spec],
            out_specs=pl.BlockSpec((None, l2, CT_KB, FN_W), lambda b, j: (b, 0, jnp.maximum(j - n_st1, 0), 0)),
            out_shape=jax.ShapeDtypeStruct((b_lat, l2, l1, FN_W), BF16),
            scratch_shapes=[pltpu.VMEM((FN_GROUPS, l1 * _pitch(l2), FN_GW), F32),
                            pltpu.VMEM((l1, l2, FN_W), BF16), pltpu.VMEM((l1, l2, FN_W), BF16),
                            pltpu.VMEM((FN_GROUPS, t_rows, FN_GW), F32), pltpu.VMEM((FN_GROUPS, t_rows, FN_GW), F32)],
            compiler_params=_params(("arbitrary", "arbitrary")),
            name="dft_lat",
        )(fx, m1_tab, m2_tab, cs_tab).reshape(nl, FN_W)

        tok = t_out.tok
        split_x = x is None
        ss_spec = lambda blk: pl.BlockSpec((TM_OUT // SUB, N_HEADS, DV, CPS * 2 * DK),
                                           lambda i: (blk(i), 0, 0, 0))
        y_mix = pl.pallas_call(
            functools.partial(_mix_out_kernel, n_ctx_tiles=t_out.n_ctx),
            grid=(t_out.n,),
            in_specs=[tok(V_W), tok(FB_W), ss_spec(t_out.ctx_blk), ss_spec(t_out.lat_blk), tok(V_W),
                      t_out.tok_ctx(FN_W), t_out.tok_lat(FN_W), tok(d), tok(d), tok(d),
                      _layer_spec(l, (1, DV)), _layer_spec(l, (V_W, d)), _layer_spec(l, (FN_W, d))],
            out_specs=tok(d),
            out_shape=jax.ShapeDtypeStruct((nt, d), BF16),
            scratch_shapes=[pltpu.VMEM((TM_OUT, V_W), F32)],
            compiler_params=_params(("arbitrary",)),
            name="mix_out",
        )(oin, qin, ss_ctx, ss_lat, og, fr_ctx, fr_lat, g0, g2, yp, gn, wa, wc_out)

        final = l == depth - 1
        tok = t_ffn.tok
        nxt = depth - 1 if final else l + 1
        outs = pl.pallas_call(
            functools.partial(_ffn_kernel, final=final, split_x=split_x, n_ctx_tiles=t_ffn.n_ctx),
            grid=(t_ffn.n,),
            in_specs=([t_ffn.tok_ctx(d), t_ffn.tok_lat(d)] if split_x else [tok(d)]) + [
                tok(d), _layer_spec(l, (d, d)), _layer_spec(l, (d, 2 * d_ff)), _layer_spec(l, (d_ff, d)),
                t_ffn.mod(l), t_ffn.mod(nxt), _layer_spec(l, (1, d)),
                _const_spec((1, d)) if final else _layer_spec(nxt, (1, d))],
            out_specs=[t_ffn.tok_ctx(d), t_ffn.tok_lat(d)] if final else [tok(d), tok(d)],
            out_shape=([jax.ShapeDtypeStruct((nc, d), F32), jax.ShapeDtypeStruct((nl, d), F32)] if final else
                       [jax.ShapeDtypeStruct((nt, d), F32), jax.ShapeDtypeStruct((nt, d), BF16)]),
            compiler_params=_params(("arbitrary",)),
            name="ffn",
        )(*((xc, xl) if split_x else (x,)), y_mix, wo, wup, wdn, mods, mods, n2, norm_f[None, :] if final else n1)
        if final:
            y_out = outs
        else:
            x, h = outs

    y_prompt = y_out[0].reshape(b_ctx, seq, d)
    y_sample = y_out[1].reshape(b_lat, dec_seq, d)
    sfin_all = jnp.stack(new_states, axis=0).reshape(depth, b_ctx, N_HEADS, DV, 2, DK)
    new_state_gla = sfin_all.transpose(1, 0, 4, 2, 5, 3).astype(x_prompt.dtype)
    return (y_prompt, y_sample, new_state_gla)
```

```python
import functools
import math

import numpy as np
import jax
import jax.numpy as jnp
from jax import lax
from jax.experimental import pallas as pl
from jax.experimental.pallas import tpu as pltpu

F32 = jnp.float32
BF16 = jnp.bfloat16

GRID_W = 64
N_HEADS = 4
DK = 64
DV = 128
V_W = N_HEADS * DV
FB_W = N_HEADS * 2 * DK
LOWRANK = 16
GATE_NORMALIZER = 16.0
CHUNK = 64
SUB = 256
CPS = SUB // CHUNK
SC_W = 512
FN_GROUPS = 4
FN_GW = 128
FN_W = FN_GROUPS * FN_GW
N_MOD = 6
EPS = 1e-6
GK_PAD = 128
SUBLANES = 8

TM_GLA = 1024
TM_MIX = 1024
TM_OUT = 1024
TM_FFN = 512
TM_PRE = 2048
FF_CHUNK = 512
CT_NB = 16
CT_KB = 16
VMEM_LIMIT = 56 * 1024 * 1024


def _dot(a, b):
    return jnp.dot(a, b, preferred_element_type=F32)


def _dot_nt(a, b):
    return lax.dot_general(a, b, (((1,), (1,)), ((), ())), preferred_element_type=F32)


def _dot_tn(a, b):
    return lax.dot_general(a, b, (((0,), (0,)), ((), ())), preferred_element_type=F32)


def _sigmoid(x):
    return 1.0 / (1.0 + jnp.exp(-x))


def _silu(x):
    return x * _sigmoid(x)


def _rms(x):
    return x * lax.rsqrt(jnp.mean(x * x, axis=-1, keepdims=True) + EPS)


def _params(sem):
    return pltpu.CompilerParams(dimension_semantics=sem, vmem_limit_bytes=VMEM_LIMIT)


def _const_spec(shape):
    nd = len(shape)
    return pl.BlockSpec(shape, lambda *_: (0,) * nd, pipeline_mode=pl.Buffered(1))


def _layer_spec(layer, shape):
    nd = len(shape)
    return pl.BlockSpec((None,) + tuple(shape), lambda *_: (layer,) + (0,) * nd, pipeline_mode=pl.Buffered(1))


def _ada_kernel(c_ref, w_ref, b_ref, o_ref):
    s = _silu(c_ref[...]).astype(BF16)
    o_ref[...] = _dot(s, w_ref[...].astype(BF16)) + b_ref[...]


def _ada(cond, w_ada, b_ada):
    depth, d, n = w_ada.shape
    rows = cond.shape[0]
    nb = 1536
    return pl.pallas_call(
        _ada_kernel,
        grid=(depth, n // nb),
        in_specs=[
            pl.BlockSpec((rows, d), lambda l, j: (0, 0)),
            pl.BlockSpec((None, d, nb), lambda l, j: (l, 0, j)),
            pl.BlockSpec((None, 1, nb), lambda l, j: (l, 0, j)),
        ],
        out_specs=pl.BlockSpec((None, rows, nb), lambda l, j: (l, 0, j)),
        out_shape=jax.ShapeDtypeStruct((depth, rows, n), F32),
        compiler_params=_params(("arbitrary", "arbitrary")),
        name="ada",
    )(cond, w_ada, b_ada.reshape(depth, 1, n))


def _prenorm_kernel(xc_ref, xl_ref, mod_ref, n_ref, h_ref, *, n_ctx_tiles):
    x = jnp.where(pl.program_id(0) < n_ctx_tiles, xc_ref[...], xl_ref[...])
    m = mod_ref[...]
    h = _rms(x) * n_ref[...] * (1.0 + m[1:2]) + m[0:1]
    h_ref[...] = h.astype(BF16)


def _gla_in_kernel(h_ref, w_ref, wgk2_ref, bgk_ref,
                   og_ref, qin_ref, oin_ref, u_ref, g_ref,
                   q_s, k_s, v_s, la_s, pre_s):
    qk_w = N_HEADS * DK
    tm = h_ref.shape[0]
    h = h_ref[...]
    gk = _dot(h, w_ref[:, 2 * qk_w + 2 * V_W:]).astype(BF16)
    lp = _dot(gk, wgk2_ref[...]) + bgk_ref[...]
    la = (jnp.minimum(lp, 0.0) - jnp.log(1.0 + jnp.exp(-jnp.abs(lp)))) * (1.0 / GATE_NORMALIZER)
    la_s[...] = la
    z = _dot(h, w_ref[:, :2 * qk_w + 2 * V_W])
    srow = lax.broadcasted_iota(jnp.int32, (1, SUBLANES, 1), 1)

    def chunk_prefix(x):
        x = x.reshape(SUB // SUBLANES, SUBLANES, FB_W)
        for sh in (1, 2, 4):
            x = x + jnp.where(srow >= sh, pltpu.roll(x, sh, axis=1), 0.0)
        groups = [x[i] for i in range(SUB // SUBLANES)]
        per_chunk = CHUNK // SUBLANES
        for i in range(len(groups)):
            if i % per_chunk:
                groups[i] = groups[i] + jnp.broadcast_to(groups[i - 1][SUBLANES - 1:SUBLANES, :], (SUBLANES, FB_W))
        return jnp.concatenate(groups, axis=0)

    lo_half = lax.broadcasted_iota(jnp.int32, (1, 2 * DK), 1) < DK

    def both_directions(x):
        tiles = []
        for p in range(N_HEADS // 2):
            t = x[:, p * 2 * DK:(p + 1) * 2 * DK]
            r = pltpu.roll(t, DK, axis=1)
            tiles += [jnp.where(lo_half, t, r), jnp.where(lo_half, r, t)]
        return jnp.concatenate(tiles, axis=1)

    q_s[...] = both_directions(z[:, :qk_w]) * (DK ** -0.5)
    k_s[...] = both_directions(z[:, qk_w:2 * qk_w])
    v_s[...] = z[:, 2 * qk_w:2 * qk_w + V_W].astype(BF16)
    og_ref[...] = z[:, 2 * qk_w + V_W:].astype(BF16)
    n_sub = tm // SUB
    pre_s[0] = chunk_prefix(la_s[0:SUB, :])

    row = lax.broadcasted_iota(jnp.int32, (SUB, SUB), 0)
    col = lax.broadcasted_iota(jnp.int32, (SUB, SUB), 1)
    same = (row & -CHUNK) == (col & -CHUNK)
    lower = same & (col <= row)
    upper = same & (col >= row)
    is_f = (lax.broadcasted_iota(jnp.int32, (SUB, FB_W), 1) & (2 * DK - 1)) < DK
    rchunk = lax.broadcasted_iota(jnp.int32, (SUB, 2 * DK), 0) & -CHUNK

    def sub_tile(s, carry):
        r0 = pl.multiple_of(s * SUB, SUB)
        rows = pl.ds(r0, SUB)
        la = la_s[rows, :]
        pre = pre_s[s & 1]
        nxt = pl.multiple_of(jnp.minimum(s + 1, n_sub - 1) * SUB, SUB)
        pre_s[(s + 1) & 1] = chunk_prefix(la_s[pl.ds(nxt, SUB), :])
        tot_rows = [pre[c * CHUNK + CHUNK - 1:c * CHUNK + CHUNK, :] for c in range(CPS)]
        tot = jnp.concatenate([jnp.broadcast_to(t, (CHUNK, FB_W)) for t in tot_rows], axis=0)
        b = jnp.where(is_f, pre, tot - pre + la)
        q = q_s[rows, :]
        k = k_s[rows, :]
        qin = (q * jnp.exp(b)).astype(BF16)
        kin = (k * jnp.exp(-b)).astype(BF16)
        kout = (k * jnp.exp(tot - b)).astype(BF16)
        qin_ref[rows, :] = qin
        zero = jnp.zeros_like(qin)
        qf = jnp.where(is_f, qin, zero)
        qb = jnp.where(is_f, zero, qin)
        v = v_s[rows, :]
        for hh in range(N_HEADS):
            fb = slice(hh * 2 * DK, (hh + 1) * 2 * DK)
            vs = slice(hh * DV, (hh + 1) * DV)
            a2 = _dot_nt(jnp.concatenate([qf[:, fb], qb[:, fb]], axis=0), kin[:, fb])
            att = (jnp.where(lower, a2[:SUB], 0.0) + jnp.where(upper, a2[SUB:], 0.0)).astype(BF16)
            oin_ref[rows, vs] = _dot(att, v[:, vs]).astype(BF16)
            ko = kout[:, fb]
            kbd = jnp.concatenate([jnp.where(rchunk == c * CHUNK, ko, jnp.zeros_like(ko)) for c in range(CPS)],
                                  axis=1)
            u_ref[s, hh] = _dot_tn(v[:, vs], kbd).astype(BF16)
            g_ref[s, hh] = jnp.concatenate([jnp.exp(t[:, fb]) for t in tot_rows], axis=1)
        return carry

    lax.fori_loop(0, h_ref.shape[0] // SUB, sub_tile, 0, unroll=2)


def _mix_in_kernel(h_ref, w_ref, cw_ref, wb_ref,
                   fx_ref, g0_ref, g2_ref, yp_ref, *, n_ctx_tiles, period_ctx, period_lat):
    i = pl.program_id(0)
    d = g0_ref.shape[-1]
    z = _dot(h_ref[...], w_ref[...])
    sb, sc, sx = z[:, :SC_W], z[:, SC_W:2 * SC_W], z[:, 2 * SC_W:3 * SC_W]
    m0 = 3 * SC_W + FN_W
    u = sc * sx
    tm = h_ref.shape[0]
    period = jnp.where(i < n_ctx_tiles, period_ctx, period_lat)
    pos = lax.broadcasted_iota(jnp.int32, (tm, 1), 0) & (period - 1)
    up = jnp.where(pos == 0, 0.0, pltpu.roll(u, 1, axis=0))
    un = jnp.where(pos == period - 1, 0.0, pltpu.roll(u, tm - 1, axis=0))
    cw = cw_ref[...]
    conv = cw[0:1] * up + cw[1:2] * u + cw[2:3] * un
    yb = _dot((sb * conv).astype(BF16), wb_ref[...])

    _put_rows(fx_ref, 0, z[:, 3 * SC_W:m0])
    g0_ref[...] = _sigmoid(z[:, m0:m0 + d]).astype(BF16)
    yp_ref[...] = (_sigmoid(z[:, m0 + d:m0 + 2 * d]) * yb).astype(BF16)
    g2_ref[...] = _sigmoid(z[:, m0 + 2 * d:m0 + 3 * d]).astype(BF16)


def _scan_kernel(u_ref, g_ref, s0_ref, ss_ref, sfin_ref, *, n_sub):
    lane_f = lax.broadcasted_iota(jnp.int32, (DV, 2 * DK), 1) < DK
    for q in range(s0_ref.shape[0]):
        for hh in range(s0_ref.shape[1]):
            s0 = s0_ref[q, hh]

            def fwd(j, st):
                for c in range(CPS):
                    cs = slice(c * 2 * DK, (c + 1) * 2 * DK)
                    ss_ref[j, hh, :, cs] = st.astype(BF16)
                    st = g_ref[j, hh, :, cs] * st + u_ref[j, hh, :, cs].astype(F32)
                return st

            def bwd(j, st):
                for c in range(CPS - 1, -1, -1):
                    cs = slice(c * 2 * DK, (c + 1) * 2 * DK)
                    ss_ref[j, hh, :, cs] = jnp.where(lane_f, ss_ref[j, hh, :, cs], st.astype(BF16))
                    st = g_ref[j, hh, :, cs] * st + u_ref[j, hh, :, cs].astype(F32)
                return st

            lo = q * n_sub
            sf = lax.fori_loop(0, n_sub, lambda i, st: fwd(lo + i, st), s0)
            sb = lax.fori_loop(0, n_sub, lambda i, st: bwd(lo + n_sub - 1 - i, st), s0)
            sfin_ref[q, hh] = jnp.where(lane_f, sf, sb)


def _scan(u, g, s0, layer, n_seq, n_sub, sub0, seq_blk, head_blk):
    rows = seq_blk * n_sub
    assert n_seq % seq_blk == 0 and N_HEADS % head_blk == 0 and sub0 % rows == 0
    wide = CPS * 2 * DK
    return pl.pallas_call(
        functools.partial(_scan_kernel, n_sub=n_sub),
        grid=(n_seq // seq_blk, N_HEADS // head_blk),
        in_specs=[
            pl.BlockSpec((rows, head_blk, DV, wide), lambda b, hh: (sub0 // rows + b, hh, 0, 0)),
            pl.BlockSpec((rows, head_blk, 1, wide), lambda b, hh: (sub0 // rows + b, hh, 0, 0)),
            pl.BlockSpec((None, seq_blk, head_blk, DV, 2 * DK), lambda b, hh: (layer, b, hh, 0, 0)),
        ],
        out_specs=[
            pl.BlockSpec((rows, head_blk, DV, wide), lambda b, hh: (b, hh, 0, 0)),
            pl.BlockSpec((seq_blk, head_blk, DV, 2 * DK), lambda b, hh: (b, hh, 0, 0)),
        ],
        out_shape=[
            jax.ShapeDtypeStruct((n_seq * n_sub, N_HEADS, DV, wide), BF16),
            jax.ShapeDtypeStruct((n_seq, N_HEADS, DV, 2 * DK), F32),
        ],
        compiler_params=_params(("arbitrary", "arbitrary")),
        name="scan",
    )(u, g, s0)


def _gather_rows(ref, start, size, stride):
    return jnp.concatenate([ref[g, pl.ds(start, size, stride=stride), :] for g in range(FN_GROUPS)], axis=1)


def _put_rows(ref, r0, val):
    for g in range(FN_GROUPS):
        ref[g, r0:r0 + val.shape[0], :] = val[:, g * FN_GW:(g + 1) * FN_GW]


def _channel_dft_real(re_g, im_g, cs):
    return _dot(jnp.concatenate([re_g.astype(BF16), im_g.astype(BF16)], axis=1), cs)


def _dft_direct_kernel(x_ref, m_ref, cs_ref, o_ref, *, seq):
    for q in range(x_ref.shape[1] // seq):
        rows = slice(q * seq, (q + 1) * seq)
        x = jnp.concatenate([x_ref[g, rows, :] for g in range(FN_GROUPS)], axis=1)
        res = _dot(m_ref[...], x.astype(BF16))
        for g in range(FN_GROUPS):
            gs = slice(g * FN_GW, (g + 1) * FN_GW)
            o_ref[rows, gs] = _channel_dft_real(res[:seq, gs], res[seq:, gs], cs_ref[...]).astype(BF16)


def _pitch(n):
    return n + 4


def _ct_kernel(x_ref, m1_ref, m2_ref, cs_ref, o_ref, xs_s, yr_s, yi_s, tr_s, ti_s, *, l1, l2):
    j = pl.program_id(1)
    n_stage1 = l2 // CT_NB
    px, pt1, pt2 = _pitch(l2), _pitch(l1), _pitch(l2)

    @pl.when((pl.program_id(0) == 0) & (j == 0))
    def _():
        tr_s[...] = jnp.zeros_like(tr_s)
        ti_s[...] = jnp.zeros_like(ti_s)

    @pl.when(j == 0)
    def _():
        for n1 in range(l1):
            for g in range(FN_GROUPS):
                xs_s[g, n1 * px:n1 * px + l2, :] = x_ref[g, n1 * l2:(n1 + 1) * l2, :]

    @pl.when(j < n_stage1)
    def _():
        col0 = pl.multiple_of(j * CT_NB, CT_NB)
        for jn in range(CT_NB):
            xs = _gather_rows(xs_s, col0 + jn, l1, px)
            res = _dot(m1_ref[col0 + jn], xs.astype(BF16))
            _put_rows(tr_s, jn * pt1, res[:l1])
            _put_rows(ti_s, jn * pt1, res[l1:])
        for k1 in range(l1):
            yr_s[k1, pl.ds(col0, CT_NB), :] = _gather_rows(tr_s, k1, CT_NB, pt1).astype(BF16)
            yi_s[k1, pl.ds(col0, CT_NB), :] = _gather_rows(ti_s, k1, CT_NB, pt1).astype(BF16)

    @pl.when(j >= n_stage1)
    def _():
        m = m2_ref[...]
        k0 = (j - n_stage1) * CT_KB
        for kk in range(CT_KB):
            rhs = jnp.concatenate([yr_s[k0 + kk], yi_s[k0 + kk]], axis=0)
            res = _dot(m, rhs)
            _put_rows(tr_s, kk * pt2, res[:l2])
            _put_rows(ti_s, kk * pt2, res[l2:])
        rows = CT_KB * pt2
        for g in range(FN_GROUPS):
            tr_s[g, :rows, :] = _channel_dft_real(tr_s[g, :rows, :], ti_s[g, :rows, :], cs_ref[...])
        for k2 in range(l2):
            o_ref[k2] = _gather_rows(tr_s, k2, CT_KB, pt2).astype(BF16)


def _cos_sin(n_out, n_in, period, scale):
    idx = (np.arange(n_out)[:, None] * np.arange(n_in)[None, :]) % period
    ang = 2.0 * np.pi * idx / period
    return np.cos(ang) * scale, np.sin(ang) * scale


def _mix_out_kernel(oin_ref, qin_ref, ssc_ref, ssl_ref, og_ref, frc_ref, frl_ref, g0_ref, g2_ref, yp_ref,
                    gn_ref, wa_ref, wc_ref, y_ref, o_s, *, n_ctx_tiles):
    is_ctx = pl.program_id(0) < n_ctx_tiles
    for s in range(oin_ref.shape[0] // SUB):
        for hh in range(N_HEADS):
            st = jnp.where(is_ctx, ssc_ref[s, hh], ssl_ref[s, hh])
            fb = slice(hh * 2 * DK, (hh + 1) * 2 * DK)
            vs = slice(hh * DV, (hh + 1) * DV)
            for c in range(CPS):
                rows = slice(s * SUB + c * CHUNK, s * SUB + (c + 1) * CHUNK)
                o_s[rows, vs] = oin_ref[rows, vs].astype(F32) + _dot_nt(qin_ref[rows, fb], st[:, c * 2 * DK:(c + 1) * 2 * DK])
    gn = gn_ref[...]
    parts = []
    for hh in range(N_HEADS):
        vs = slice(hh * DV, (hh + 1) * DV)
        parts.append((_rms(o_s[:, vs]) * gn * _silu(og_ref[:, vs].astype(F32))).astype(BF16))
    ya = _dot(jnp.concatenate(parts, axis=1), wa_ref[...])
    yc = _dot(jnp.where(is_ctx, frc_ref[...], frl_ref[...]), wc_ref[...])
    y = g0_ref[...].astype(F32) * ya + yp_ref[...].astype(F32) + g2_ref[...].astype(F32) * yc
    y_ref[...] = y.astype(BF16)


def _ffn_kernel(*refs, final, split_x, n_ctx_tiles):
    if split_x:
        xc_ref, xl_ref, *refs = refs
    else:
        x_ref, *refs = refs
    y_ref, wo_ref, wup_ref, wd_ref, mod_ref, modn_ref, n2_ref, nn_ref, *outs = refs
    is_ctx = pl.program_id(0) < n_ctx_tiles
    x = jnp.where(is_ctx, xc_ref[...], xl_ref[...]) if split_x else x_ref[...]
    m = mod_ref[...]
    x1 = x + m[2:3] * _dot(y_ref[...], wo_ref[...])
    h2 = (_rms(x1) * n2_ref[...] * (1.0 + m[4:5]) + m[3:4]).astype(BF16)
    d_ff = wd_ref.shape[0]
    acc = None
    for c0 in range(0, d_ff, FF_CHUNK):
        c1 = min(c0 + FF_CHUNK, d_ff)
        gate = _dot(h2, wup_ref[:, c0:c1])
        up = _dot(h2, wup_ref[:, d_ff + c0:d_ff + c1])
        part = _dot((_silu(gate) * up).astype(BF16), wd_ref[c0:c1, :])
        acc = part if acc is None else acc + part
    x2 = x1 + m[5:6] * acc
    if final:
        yc_ref, yl_ref = outs
        out = _rms(x2) * nn_ref[...]

        @pl.when(is_ctx)
        def _():
            yc_ref[...] = out

        @pl.when(jnp.logical_not(is_ctx))
        def _():
            yl_ref[...] = out
    else:
        x2_ref, hn_ref = outs
        x2_ref[...] = x2
        hn_ref[...] = (_rms(x2) * nn_ref[...] * (1.0 + modn_ref[1:2, :]) + modn_ref[0:1, :]).astype(BF16)


def kernel(x_prompt, x_sample, state_gla, c, c_ctx, w_ada, b_ada, norm1, norm2, w_in, w_gk_f, b_gk_f,
           w_gk_b, b_gk_b, gla_norm, w_a_out, conv_w, w_b_out, w_c_out, w_o, w_up, w_down, norm_f):
    b_ctx, seq, d = x_prompt.shape
    b_lat, dec_seq, _ = x_sample.shape
    depth = w_ada.shape[0]
    d_ff = w_down.shape[1]
    nc, nl = b_ctx * seq, b_lat * dec_seq
    nt = nc + nl
    l1, l2 = dec_seq // GRID_W, GRID_W
    assert seq == SUB and d_ff % 128 == 0
    assert nc % dec_seq == 0 and l2 % CT_NB == 0 and l1 % CT_KB == 0
    n_sub_tot, n_sub_lat = nt // SUB, dec_seq // SUB
    assert (nc // SUB) % n_sub_lat == 0
    n_cond = -(-(1 + b_lat) // 8) * 8

    class Tiling:
        def __init__(self, tm):
            assert nc % tm == 0 and dec_seq % tm == 0
            self.tm, self.n, self.n_ctx, self.per_lat = tm, nt // tm, nc // tm, dec_seq // tm

        def cond(self, i):
            return jnp.where(i < self.n_ctx, 0, 1 + (i - self.n_ctx) // self.per_lat)

        def ctx_blk(self, i):
            return jnp.minimum(i, self.n_ctx - 1)

        def lat_blk(self, i):
            return jnp.maximum(i - self.n_ctx, 0)

        def tok(self, w):
            return pl.BlockSpec((self.tm, w), lambda i, *_: (i, 0))

        def tok_ctx(self, w):
            return pl.BlockSpec((self.tm, w), lambda i, *_: (self.ctx_blk(i), 0))

        def tok_lat(self, w):
            return pl.BlockSpec((self.tm, w), lambda i, *_: (self.lat_blk(i), 0))

        def mod(self, layer):
            return pl.BlockSpec((None, None, N_MOD, d), lambda i, *_: (layer, self.cond(i), 0, 0))

    t_gla, t_mix, t_out, t_ffn = Tiling(TM_GLA), Tiling(TM_MIX), Tiling(TM_OUT), Tiling(TM_FFN)
    t_pre = Tiling(math.gcd(math.gcd(nc, dec_seq), TM_PRE))

    cond = jnp.concatenate([c_ctx[None, :], c, jnp.zeros((n_cond - 1 - b_lat, d), F32)], axis=0)
    mods = _ada(cond, w_ada, b_ada).reshape(depth, n_cond, N_MOD, d)

    cc, sc_ = _cos_sin(FN_GW, FN_GW, FN_GW, FN_GW ** -0.5)
    cs_tab = jnp.asarray(np.concatenate([cc, sc_], axis=0), F32).astype(BF16)
    cl, sl = _cos_sin(seq, seq, seq, seq ** -0.5)
    m_ctx_tab = jnp.asarray(np.concatenate([cl, -sl], axis=0), F32).astype(BF16)
    ang = 2.0 * np.pi * ((np.arange(l1)[None, :, None] * (l2 * np.arange(l1)[None, None, :]
                                                           + np.arange(l2)[:, None, None])) % (l1 * l2)) / (l1 * l2)
    m1_tab = jnp.asarray(np.concatenate([np.cos(ang), -np.sin(ang)], axis=1) * l1 ** -0.5, F32).astype(BF16)
    c2, s2 = _cos_sin(l2, l2, l2, l2 ** -0.5)
    m2_tab = jnp.asarray(np.block([[c2, s2], [-s2, c2]]), F32).astype(BF16)

    xc, xl = x_prompt.reshape(nc, d), x_sample.reshape(nl, d)
    x = None

    gla_cols = 2 * N_HEADS * DK + 2 * V_W
    mix_col0 = gla_cols + 2 * LOWRANK
    w_in16 = w_in.astype(BF16)
    w_gla = w_in16[:, :, :gla_cols + GK_PAD]
    w_mix = w_in16[:, :, mix_col0:]
    zf = jnp.zeros((depth, LOWRANK, N_HEADS, DK), F32)
    top = jnp.stack([w_gk_f.reshape(depth, LOWRANK, N_HEADS, DK), zf], axis=3).reshape(depth, LOWRANK, FB_W)
    bot = jnp.stack([zf, w_gk_b.reshape(depth, LOWRANK, N_HEADS, DK)], axis=3).reshape(depth, LOWRANK, FB_W)
    wgk2 = jnp.concatenate([top, bot, jnp.zeros((depth, GK_PAD - 2 * LOWRANK, FB_W), F32)], axis=1).astype(BF16)
    bgk = jnp.stack([b_gk_f.reshape(depth, N_HEADS, DK), b_gk_b.reshape(depth, N_HEADS, DK)],
                    axis=2).reshape(depth, 1, FB_W)
    wb, wa, wc_out, wo = (t.astype(BF16) for t in (w_b_out, w_a_out, w_c_out, w_o))
    wup, wdn = w_up.astype(BF16), w_down.astype(BF16)
    n1, n2, gn = norm1[:, None, :], norm2[:, None, :], gla_norm[:, None, :]
    s0_ctx = jnp.zeros((1, b_ctx, N_HEADS, DV, 2 * DK), F32)
    s0_lat = state_gla.transpose(1, 0, 3, 5, 2, 4).reshape(depth, b_lat, N_HEADS, DV, 2 * DK)

    h = pl.pallas_call(
        functools.partial(_prenorm_kernel, n_ctx_tiles=t_pre.n_ctx),
        grid=(t_pre.n,),
        in_specs=[t_pre.tok_ctx(d), t_pre.tok_lat(d), t_pre.mod(0), _layer_spec(0, (1, d))],
        out_specs=t_pre.tok(d),
        out_shape=jax.ShapeDtypeStruct((nt, d), BF16),
        compiler_params=_params(("arbitrary",)),
        name="prenorm",
    )(xc, xl, mods, n1)

    new_states = []
    y_out = None
    for l in range(depth):
        tok = t_gla.tok
        og, qin, oin, u, g = pl.pallas_call(
            _gla_in_kernel,
            grid=(t_gla.n,),
            in_specs=[tok(d), _layer_spec(l, (d, gla_cols + GK_PAD)), _layer_spec(l, (GK_PAD, FB_W)),
                      _layer_spec(l, (1, FB_W))],
            out_specs=[tok(V_W), tok(FB_W), tok(V_W),
                       pl.BlockSpec((TM_GLA // SUB, N_HEADS, DV, CPS * 2 * DK), lambda i: (i, 0, 0, 0)),
                       pl.BlockSpec((TM_GLA // SUB, N_HEADS, 1, CPS * 2 * DK), lambda i: (i, 0, 0, 0))],
            out_shape=[jax.ShapeDtypeStruct((nt, V_W), BF16), jax.ShapeDtypeStruct((nt, FB_W), BF16),
                       jax.ShapeDtypeStruct((nt, V_W), BF16),
                       jax.ShapeDtypeStruct((n_sub_tot, N_HEADS, DV, CPS * 2 * DK), BF16),
                       jax.ShapeDtypeStruct((n_sub_tot, N_HEADS, 1, CPS * 2 * DK), F32)],
            scratch_shapes=[pltpu.VMEM((TM_GLA, FB_W), F32), pltpu.VMEM((TM_GLA, FB_W), F32),
                            pltpu.VMEM((TM_GLA, V_W), BF16), pltpu.VMEM((TM_GLA, FB_W), F32),
                            pltpu.VMEM((2, SUB, FB_W), F32)],
            compiler_params=_params(("arbitrary",)),
            name="gla_in",
        )(h, w_gla, wgk2, bgk)

        tok = t_mix.tok
        fx, g0, g2, yp = pl.pallas_call(
            functools.partial(_mix_in_kernel, n_ctx_tiles=t_mix.n_ctx, period_ctx=seq, period_lat=GRID_W),
            grid=(t_mix.n,),
            in_specs=[tok(d), _layer_spec(l, (d, 3 * SC_W + FN_W + 3 * d)),
                      _layer_spec(l, (3, SC_W)), _layer_spec(l, (SC_W, d))],
            out_specs=[pl.BlockSpec((FN_GROUPS, TM_MIX, FN_GW), lambda i: (0, i, 0)), tok(d), tok(d), tok(d)],
            out_shape=[jax.ShapeDtypeStruct((FN_GROUPS, nt, FN_GW), F32)] + [jax.ShapeDtypeStruct((nt, d), BF16)] * 3,
            compiler_params=_params(("arbitrary",)),
            name="mix_in",
        )(h, w_mix, conv_w, wb)

        ss_ctx, sfin = _scan(u, g, s0_ctx, 0, b_ctx, seq // SUB, 0, math.gcd(b_ctx, 4), N_HEADS)
        ss_lat, _ = _scan(u, g, s0_lat, l, b_lat, n_sub_lat, nc // SUB, 1, N_HEADS)
        new_states.append(sfin)

        cs_spec = _const_spec((2 * FN_GW, FN_GW))
        sq = math.gcd(b_ctx, 4)
        fr_ctx = pl.pallas_call(
            functools.partial(_dft_direct_kernel, seq=seq),
            grid=(b_ctx // sq,),
            in_specs=[pl.BlockSpec((FN_GROUPS, sq * seq, FN_GW), lambda s: (0, s, 0)),
                      _const_spec((2 * seq, seq)), cs_spec],
            out_specs=pl.BlockSpec((sq * seq, FN_W), lambda s: (s, 0)),
            out_shape=jax.ShapeDtypeStruct((nc, FN_W), BF16),
            compiler_params=_params(("arbitrary",)),
            name="dft_ctx",
        )(fx, m_ctx_tab, cs_tab)
        lat_blk0 = nc // dec_seq
        n_st1, n_st2 = l2 // CT_NB, l1 // CT_KB
        t_rows = max(CT_NB * _pitch(l1), CT_KB * _pitch(l2))
        fr_lat = pl.pallas_call(
            functools.partial(_ct_kernel, l1=l1, l2=l2),
            grid=(b_lat, n_st1 + n_st2),
            in_specs=[pl.BlockSpec((FN_GROUPS, dec_seq, FN_GW), lambda b, j: (0, lat_blk0 + b, 0)),
                      _const_spec((l2, 2 * l1, l1)), _const_spec((2 * l2, 2 * l2)), cs_spec],
            out_specs=pl.BlockSpec((None, l2, CT_KB, FN_W), lambda b, j: (b, 0, jnp.maximum(j - n_st1, 0), 0)),
            out_shape=jax.ShapeDtypeStruct((b_lat, l2, l1, FN_W), BF16),
            scratch_shapes=[pltpu.VMEM((FN_GROUPS, l1 * _pitch(l2), FN_GW), F32),
                            pltpu.VMEM((l1, l2, FN_W), BF16), pltpu.VMEM((l1, l2, FN_W), BF16),
                            pltpu.VMEM((FN_GROUPS, t_rows, FN_GW), F32), pltpu.VMEM((FN_GROUPS, t_rows, FN_GW), F32)],
            compiler_params=_params(("arbitrary", "arbitrary")),
            name="dft_lat",
        )(fx, m1_tab, m2_tab, cs_tab).reshape(nl, FN_W)

        tok = t_out.tok
        split_x = x is None
        ss_spec = lambda blk: pl.BlockSpec((TM_OUT // SUB, N_HEADS, DV, CPS * 2 * DK),
                                           lambda i: (blk(i), 0, 0, 0))
        y_mix = pl.pallas_call(
            functools.partial(_mix_out_kernel, n_ctx_tiles=t_out.n_ctx),
            grid=(t_out.n,),
            in_specs=[tok(V_W), tok(FB_W), ss_spec(t_out.ctx_blk), ss_spec(t_out.lat_blk), tok(V_W),
                      t_out.tok_ctx(FN_W), t_out.tok_lat(FN_W), tok(d), tok(d), tok(d),
                      _layer_spec(l, (1, DV)), _layer_spec(l, (V_W, d)), _layer_spec(l, (FN_W, d))],
            out_specs=tok(d),
            out_shape=jax.ShapeDtypeStruct((nt, d), BF16),
            scratch_shapes=[pltpu.VMEM((TM_OUT, V_W), F32)],
            compiler_params=_params(("arbitrary",)),
            name="mix_out",
        )(oin, qin, ss_ctx, ss_lat, og, fr_ctx, fr_lat, g0, g2, yp, gn, wa, wc_out)

        final = l == depth - 1
        tok = t_ffn.tok
        nxt = depth - 1 if final else l + 1
        outs = pl.pallas_call(
            functools.partial(_ffn_kernel, final=final, split_x=split_x, n_ctx_tiles=t_ffn.n_ctx),
            grid=(t_ffn.n,),
            in_specs=([t_ffn.tok_ctx(d), t_ffn.tok_lat(d)] if split_x else [tok(d)]) + [
                tok(d), _layer_spec(l, (d, d)), _layer_spec(l, (d, 2 * d_ff)), _layer_spec(l, (d_ff, d)),
                t_ffn.mod(l), t_ffn.mod(nxt), _layer_spec(l, (1, d)),
                _const_spec((1, d)) if final else _layer_spec(nxt, (1, d))],
            out_specs=[t_ffn.tok_ctx(d), t_ffn.tok_lat(d)] if final else [tok(d), tok(d)],
            out_shape=([jax.ShapeDtypeStruct((nc, d), F32), jax.ShapeDtypeStruct((nl, d), F32)] if final else
                       [jax.ShapeDtypeStruct((nt, d), F32), jax.ShapeDtypeStruct((nt, d), BF16)]),
            compiler_params=_params(("arbitrary",)),
            name="ffn",
        )(*((xc, xl) if split_x else (x,)), y_mix, wo, wup, wdn, mods, mods, n2, norm_f[None, :] if final else n1)
        if final:
            y_out = outs
        else:
            x, h = outs

    y_prompt = y_out[0].reshape(b_ctx, seq, d)
    y_sample = y_out[1].reshape(b_lat, dec_seq, d)
    sfin_all = jnp.stack(new_states, axis=0).reshape(depth, b_ctx, N_HEADS, DV, 2, DK)
    new_state_gla = sfin_all.transpose(1, 0, 4, 2, 5, 3).astype(x_prompt.dtype)
    return (y_prompt, y_sample, new_state_gla)
```

```python
import functools
import math

import numpy as np
import jax
import jax.numpy as jnp
from jax import lax
from jax.experimental import pallas as pl
from jax.experimental.pallas import tpu as pltpu

F32 = jnp.float32
BF16 = jnp.bfloat16

GRID_W = 64
N_HEADS = 4
DK = 64
DV = 128
V_W = N_HEADS * DV
FB_W = N_HEADS * 2 * DK
LOWRANK = 16
GATE_NORMALIZER = 16.0
CHUNK = 64
SUB = 256
CPS = SUB // CHUNK
SC_W = 512
FN_GROUPS = 4
FN_GW = 128
FN_W = FN_GROUPS * FN_GW
N_MOD = 6
EPS = 1e-6
GK_PAD = 128
SUBLANES = 8

TM_GLA = 1024
TM_MIX = 1024
TM_OUT = 1024
TM_FFN = 512
TM_PRE = 2048
FF_CHUNK = 512
CT_NB = 16
CT_KB = 16
VMEM_LIMIT = 56 * 1024 * 1024


def _dot(a, b):
    return jnp.dot(a, b, preferred_element_type=F32)


def _dot_nt(a, b):
    return lax.dot_general(a, b, (((1,), (1,)), ((), ())), preferred_element_type=F32)


def _dot_tn(a, b):
    return lax.dot_general(a, b, (((0,), (0,)), ((), ())), preferred_element_type=F32)


def _sigmoid(x):
    return 1.0 / (1.0 + jnp.exp(-x))


def _silu(x):
    return x * _sigmoid(x)


def _rms(x):
    return x * lax.rsqrt(jnp.mean(x * x, axis=-1, keepdims=True) + EPS)


def _params(sem):
    return pltpu.CompilerParams(dimension_semantics=sem, vmem_limit_bytes=VMEM_LIMIT)


def _const_spec(shape):
    nd = len(shape)
    return pl.BlockSpec(shape, lambda *_: (0,) * nd, pipeline_mode=pl.Buffered(1))


def _layer_spec(layer, shape):
    nd = len(shape)
    return pl.BlockSpec((None,) + tuple(shape), lambda *_: (layer,) + (0,) * nd, pipeline_mode=pl.Buffered(1))


def _ada_kernel(c_ref, w_ref, b_ref, o_ref):
    s = _silu(c_ref[...]).astype(BF16)
    o_ref[...] = _dot(s, w_ref[...].astype(BF16)) + b_ref[...]


def _ada(cond, w_ada, b_ada):
    depth, d, n = w_ada.shape
    rows = cond.shape[0]
    nb = 1536
    return pl.pallas_call(
        _ada_kernel,
        grid=(depth, n // nb),
        in_specs=[
            pl.BlockSpec((rows, d), lambda l, j: (0, 0)),
            pl.BlockSpec((None, d, nb), lambda l, j: (l, 0, j)),
            pl.BlockSpec((None, 1, nb), lambda l, j: (l, 0, j)),
        ],
        out_specs=pl.BlockSpec((None, rows, nb), lambda l, j: (l, 0, j)),
        out_shape=jax.ShapeDtypeStruct((depth, rows, n), F32),
        compiler_params=_params(("arbitrary", "arbitrary")),
        name="ada",
    )(cond, w_ada, b_ada.reshape(depth, 1, n))


def _prenorm_kernel(xc_ref, xl_ref, mod_ref, n_ref, h_ref, *, n_ctx_tiles):
    x = jnp.where(pl.program_id(0) < n_ctx_tiles, xc_ref[...], xl_ref[...])
    m = mod_ref[...]
    h = _rms(x) * n_ref[...] * (1.0 + m[1:2]) + m[0:1]
    h_ref[...] = h.astype(BF16)


def _gla_in_kernel(h_ref, w_ref, wgk2_ref, bgk_ref,
                   og_ref, qin_ref, oin_ref, u_ref, g_ref,
                   q_s, k_s, v_s, la_s, pre_s):
    qk_w = N_HEADS * DK
    tm = h_ref.shape[0]
    h = h_ref[...]
    gk = _dot(h, w_ref[:, 2 * qk_w + 2 * V_W:]).astype(BF16)
    lp = _dot(gk, wgk2_ref[...]) + bgk_ref[...]
    la = (jnp.minimum(lp, 0.0) - jnp.log(1.0 + jnp.exp(-jnp.abs(lp)))) * (1.0 / GATE_NORMALIZER)
    la_s[...] = la
    z = _dot(h, w_ref[:, :2 * qk_w + 2 * V_W])
    srow = lax.broadcasted_iota(jnp.int32, (1, SUBLANES, 1), 1)

    def chunk_prefix(x):
        x = x.reshape(SUB // SUBLANES, SUBLANES, FB_W)
        for sh in (1, 2, 4):
            x = x + jnp.where(srow >= sh, pltpu.roll(x, sh, axis=1), 0.0)
        groups = [x[i] for i in range(SUB // SUBLANES)]
        per_chunk = CHUNK // SUBLANES
        for i in range(len(groups)):
            if i % per_chunk:
                groups[i] = groups[i] + jnp.broadcast_to(groups[i - 1][SUBLANES - 1:SUBLANES, :], (SUBLANES, FB_W))
        return jnp.concatenate(groups, axis=0)

    lo_half = lax.broadcasted_iota(jnp.int32, (1, 2 * DK), 1) < DK

    def both_directions(x):
        tiles = []
        for p in range(N_HEADS // 2):
            t = x[:, p * 2 * DK:(p + 1) * 2 * DK]
            r = pltpu.roll(t, DK, axis=1)
            tiles += [jnp.where(lo_half, t, r), jnp.where(lo_half, r, t)]
        return jnp.concatenate(tiles, axis=1)

    q_s[...] = both_directions(z[:, :qk_w]) * (DK ** -0.5)
    k_s[...] = both_directions(z[:, qk_w:2 * qk_w])
    v_s[...] = z[:, 2 * qk_w:2 * qk_w + V_W].astype(BF16)
    og_ref[...] = z[:, 2 * qk_w + V_W:].astype(BF16)
    n_sub = tm // SUB
    pre_s[0] = chunk_prefix(la_s[0:SUB, :])

    row = lax.broadcasted_iota(jnp.int32, (SUB, SUB), 0)
    col = lax.broadcasted_iota(jnp.int32, (SUB, SUB), 1)
    same = (row & -CHUNK) == (col & -CHUNK)
    lower = same & (col <= row)
    upper = same & (col >= row)
    is_f = (lax.broadcasted_iota(jnp.int32, (SUB, FB_W), 1) & (2 * DK - 1)) < DK
    rchunk = lax.broadcasted_iota(jnp.int32, (SUB, 2 * DK), 0) & -CHUNK

    def sub_tile(s, carry):
        r0 = pl.multiple_of(s * SUB, SUB)
        rows = pl.ds(r0, SUB)
        la = la_s[rows, :]
        pre = pre_s[s & 1]
        nxt = pl.multiple_of(jnp.minimum(s + 1, n_sub - 1) * SUB, SUB)
        pre_s[(s + 1) & 1] = chunk_prefix(la_s[pl.ds(nxt, SUB), :])
        tot_rows = [pre[c * CHUNK + CHUNK - 1:c * CHUNK + CHUNK, :] for c in range(CPS)]
        tot = jnp.concatenate([jnp.broadcast_to(t, (CHUNK, FB_W)) for t in tot_rows], axis=0)
        b = jnp.where(is_f, pre, tot - pre + la)
        q = q_s[rows, :]
        k = k_s[rows, :]
        qin = (q * jnp.exp(b)).astype(BF16)
        kin = (k * jnp.exp(-b)).astype(BF16)
        kout = (k * jnp.exp(tot - b)).astype(BF16)
        qin_ref[rows, :] = qin
        zero = jnp.zeros_like(qin)
        qf = jnp.where(is_f, qin, zero)
        qb = jnp.where(is_f, zero, qin)
        v = v_s[rows, :]
        for hh in range(N_HEADS):
            fb = slice(hh * 2 * DK, (hh + 1) * 2 * DK)
            vs = slice(hh * DV, (hh + 1) * DV)
            a2 = _dot_nt(jnp.concatenate([qf[:, fb], qb[:, fb]], axis=0), kin[:, fb])
            att = (jnp.where(lower, a2[:SUB], 0.0) + jnp.where(upper, a2[SUB:], 0.0)).astype(BF16)
            oin_ref[rows, vs] = _dot(att, v[:, vs]).astype(BF16)
            ko = kout[:, fb]
            kbd = jnp.concatenate([jnp.where(rchunk == c * CHUNK, ko, jnp.zeros_like(ko)) for c in range(CPS)],
                                  axis=1)
            u_ref[s, hh] = _dot_tn(v[:, vs], kbd).astype(BF16)
            g_ref[s, hh] = jnp.concatenate([jnp.exp(t[:, fb]) for t in tot_rows], axis=1)
        return carry

    lax.fori_loop(0, h_ref.shape[0] // SUB, sub_tile, 0, unroll=2)


def _mix_in_kernel(h_ref, w_ref, cw_ref, wb_ref,
                   fx_ref, g0_ref, g2_ref, yp_ref, *, n_ctx_tiles, period_ctx, period_lat):
    i = pl.program_id(0)
    d = g0_ref.shape[-1]
    z = _dot(h_ref[...], w_ref[...])
    sb, sc, sx = z[:, :SC_W], z[:, SC_W:2 * SC_W], z[:, 2 * SC_W:3 * SC_W]
    m0 = 3 * SC_W + FN_W
    u = sc * sx
    tm = h_ref.shape[0]
    period = jnp.where(i < n_ctx_tiles, period_ctx, period_lat)
    pos = lax.broadcasted_iota(jnp.int32, (tm, 1), 0) & (period - 1)
    up = jnp.where(pos == 0, 0.0, pltpu.roll(u, 1, axis=0))
    un = jnp.where(pos == period - 1, 0.0, pltpu.roll(u, tm - 1, axis=0))
    cw = cw_ref[...]
    conv = cw[0:1] * up + cw[1:2] * u + cw[2:3] * un
    yb = _dot((sb * conv).astype(BF16), wb_ref[...])

    _put_rows(fx_ref, 0, z[:, 3 * SC_W:m0])
    g0_ref[...] = _sigmoid(z[:, m0:m0 + d]).astype(BF16)
    yp_ref[...] = (_sigmoid(z[:, m0 + d:m0 + 2 * d]) * yb).astype(BF16)
    g2_ref[...] = _sigmoid(z[:, m0 + 2 * d:m0 + 3 * d]).astype(BF16)


def _scan_kernel(u_ref, g_ref, s0_ref, ss_ref, sfin_ref, *, n_sub):
    lane_f = lax.broadcasted_iota(jnp.int32, (DV, 2 * DK), 1) < DK
    for q in range(s0_ref.shape[0]):
        for hh in range(s0_ref.shape[1]):
            s0 = s0_ref[q, hh]

            def fwd(j, st):
                for c in range(CPS):
                    cs = slice(c * 2 * DK, (c + 1) * 2 * DK)
                    ss_ref[j, hh, :, cs] = st.astype(BF16)
                    st = g_ref[j, hh, :, cs] * st + u_ref[j, hh, :, cs].astype(F32)
                return st

            def bwd(j, st):
                for c in range(CPS - 1, -1, -1):
                    cs = slice(c * 2 * DK, (c + 1) * 2 * DK)
                    ss_ref[j, hh, :, cs] = jnp.where(lane_f, ss_ref[j, hh, :, cs], st.astype(BF16))
                    st = g_ref[j, hh, :, cs] * st + u_ref[j, hh, :, cs].astype(F32)
                return st

            lo = q * n_sub
            sf = lax.fori_loop(0, n_sub, lambda i, st: fwd(lo + i, st), s0)
            sb = lax.fori_loop(0, n_sub, lambda i, st: bwd(lo + n_sub - 1 - i, st), s0)
            sfin_ref[q, hh] = jnp.where(lane_f, sf, sb)


def _scan(u, g, s0, layer, n_seq, n_sub, sub0, seq_blk, head_blk):
    rows = seq_blk * n_sub
    assert n_seq % seq_blk == 0 and N_HEADS % head_blk == 0 and sub0 % rows == 0
    wide = CPS * 2 * DK
    return pl.pallas_call(
        functools.partial(_scan_kernel, n_sub=n_sub),
        grid=(n_seq // seq_blk, N_HEADS // head_blk),
        in_specs=[
            pl.BlockSpec((rows, head_blk, DV, wide), lambda b, hh: (sub0 // rows + b, hh, 0, 0)),
            pl.BlockSpec((rows, head_blk, 1, wide), lambda b, hh: (sub0 // rows + b, hh, 0, 0)),
            pl.BlockSpec((None, seq_blk, head_blk, DV, 2 * DK), lambda b, hh: (layer, b, hh, 0, 0)),
        ],
        out_specs=[
            pl.BlockSpec((rows, head_blk, DV, wide), lambda b, hh: (b, hh, 0, 0)),
            pl.BlockSpec((seq_blk, head_blk, DV, 2 * DK), lambda b, hh: (b, hh, 0, 0)),
        ],
        out_shape=[
            jax.ShapeDtypeStruct((n_seq * n_sub, N_HEADS, DV, wide), BF16),
            jax.ShapeDtypeStruct((n_seq, N_HEADS, DV, 2 * DK), F32),
        ],
        compiler_params=_params(("arbitrary", "arbitrary")),
        name="scan",
    )(u, g, s0)


def _gather_rows(ref, start, size, stride):
    return jnp.concatenate([ref[g, pl.ds(start, size, stride=stride), :] for g in range(FN_GROUPS)], axis=1)


def _put_rows(ref, r0, val):
    for g in range(FN_GROUPS):
        ref[g, r0:r0 + val.shape[0], :] = val[:, g * FN_GW:(g + 1) * FN_GW]


def _channel_dft_real(re_g, im_g, cs):
    return _dot(jnp.concatenate([re_g.astype(BF16), im_g.astype(BF16)], axis=1), cs)


def _dft_direct_kernel(x_ref, m_ref, cs_ref, o_ref, *, seq):
    for q in range(x_ref.shape[1] // seq):
        rows = slice(q * seq, (q + 1) * seq)
        x = jnp.concatenate([x_ref[g, rows, :] for g in range(FN_GROUPS)], axis=1)
        res = _dot(m_ref[...], x.astype(BF16))
        for g in range(FN_GROUPS):
            gs = slice(g * FN_GW, (g + 1) * FN_GW)
            o_ref[rows, gs] = _channel_dft_real(res[:seq, gs], res[seq:, gs], cs_ref[...]).astype(BF16)


def _pitch(n):
    return n + 4


def _ct_kernel(x_ref, m1_ref, m2_ref, cs_ref, o_ref, xs_s, yr_s, yi_s, tr_s, ti_s, *, l1, l2):
    j = pl.program_id(1)
    n_stage1 = l2 // CT_NB
    px, pt1, pt2 = _pitch(l2), _pitch(l1), _pitch(l2)

    @pl.when((pl.program_id(0) == 0) & (j == 0))
    def _():
        tr_s[...] = jnp.zeros_like(tr_s)
        ti_s[...] = jnp.zeros_like(ti_s)

    @pl.when(j == 0)
    def _():
        for n1 in range(l1):
            for g in range(FN_GROUPS):
                xs_s[g, n1 * px:n1 * px + l2, :] = x_ref[g, n1 * l2:(n1 + 1) * l2, :]

    @pl.when(j < n_stage1)
    def _():
        col0 = pl.multiple_of(j * CT_NB, CT_NB)
        for jn in range(CT_NB):
            xs = _gather_rows(xs_s, col0 + jn, l1, px)
            res = _dot(m1_ref[col0 + jn], xs.astype(BF16))
            _put_rows(tr_s, jn * pt1, res[:l1])
            _put_rows(ti_s, jn * pt1, res[l1:])
        for k1 in range(l1):
            yr_s[k1, pl.ds(col0, CT_NB), :] = _gather_rows(tr_s, k1, CT_NB, pt1).astype(BF16)
            yi_s[k1, pl.ds(col0, CT_NB), :] = _gather_rows(ti_s, k1, CT_NB, pt1).astype(BF16)

    @pl.when(j >= n_stage1)
    def _():
        m = m2_ref[...]
        k0 = (j - n_stage1) * CT_KB
        for kk in range(CT_KB):
            rhs = jnp.concatenate([yr_s[k0 + kk], yi_s[k0 + kk]], axis=0)
            res = _dot(m, rhs)
            _put_rows(tr_s, kk * pt2, res[:l2])
            _put_rows(ti_s, kk * pt2, res[l2:])
        rows = CT_KB * pt2
        for g in range(FN_GROUPS):
            tr_s[g, :rows, :] = _channel_dft_real(tr_s[g, :rows, :], ti_s[g, :rows, :], cs_ref[...])
        for k2 in range(l2):
            o_ref[k2] = _gather_rows(tr_s, k2, CT_KB, pt2).astype(BF16)


def _cos_sin(n_out, n_in, period, scale):
    idx = (np.arange(n_out)[:, None] * np.arange(n_in)[None, :]) % period
    ang = 2.0 * np.pi * idx / period
    return np.cos(ang) * scale, np.sin(ang) * scale


def _mix_out_kernel(oin_ref, qin_ref, ssc_ref, ssl_ref, og_ref, frc_ref, frl_ref,
                    gn_ref, wa_ref, wc_ref, ya_ref, yc_ref, o_s, *, n_ctx_tiles):
    is_ctx = pl.program_id(0) < n_ctx_tiles
    for s in range(oin_ref.shape[0] // SUB):
        for hh in range(N_HEADS):
            st = jnp.where(is_ctx, ssc_ref[s, hh], ssl_ref[s, hh])
            fb = slice(hh * 2 * DK, (hh + 1) * 2 * DK)
            vs = slice(hh * DV, (hh + 1) * DV)
            for c in range(CPS):
                rows = slice(s * SUB + c * CHUNK, s * SUB + (c + 1) * CHUNK)
                o_s[rows, vs] = oin_ref[rows, vs].astype(F32) + _dot_nt(qin_ref[rows, fb], st[:, c * 2 * DK:(c + 1) * 2 * DK])
    gn = gn_ref[...]
    parts = []
    for hh in range(N_HEADS):
        vs = slice(hh * DV, (hh + 1) * DV)
        parts.append((_rms(o_s[:, vs]) * gn * _silu(og_ref[:, vs].astype(F32))).astype(BF16))
    ya = _dot(jnp.concatenate(parts, axis=1), wa_ref[...])
    ya_ref[...] = ya.astype(BF16)
    yc_ref[...] = _dot(jnp.where(is_ctx, frc_ref[...], frl_ref[...]), wc_ref[...]).astype(BF16)


def _ffn_kernel(*refs, final, split_x, n_ctx_tiles):
    if split_x:
        xc_ref, xl_ref, *refs = refs
    else:
        x_ref, *refs = refs
    (ya_ref, yc_ref, g0_ref, g2_ref, yp_ref, wo_ref, wup_ref, wd_ref, mod_ref, modn_ref, n2_ref, nn_ref,
     *outs) = refs
    is_ctx = pl.program_id(0) < n_ctx_tiles
    x = jnp.where(is_ctx, xc_ref[...], xl_ref[...]) if split_x else x_ref[...]
    m = mod_ref[...]
    y = (g0_ref[...].astype(F32) * ya_ref[...].astype(F32) + yp_ref[...].astype(F32)
         + g2_ref[...].astype(F32) * yc_ref[...].astype(F32))
    x1 = x + m[2:3] * _dot(y.astype(BF16), wo_ref[...])
    h2 = (_rms(x1) * n2_ref[...] * (1.0 + m[4:5]) + m[3:4]).astype(BF16)
    d_ff = wd_ref.shape[0]
    acc = None
    for c0 in range(0, d_ff, FF_CHUNK):
        c1 = min(c0 + FF_CHUNK, d_ff)
        gate = _dot(h2, wup_ref[:, c0:c1])
        up = _dot(h2, wup_ref[:, d_ff + c0:d_ff + c1])
        part = _dot((_silu(gate) * up).astype(BF16), wd_ref[c0:c1, :])
        acc = part if acc is None else acc + part
    x2 = x1 + m[5:6] * acc
    if final:
        yc_ref, yl_ref = outs
        out = _rms(x2) * nn_ref[...]

        @pl.when(is_ctx)
        def _():
            yc_ref[...] = out

        @pl.when(jnp.logical_not(is_ctx))
        def _():
            yl_ref[...] = out
    else:
        x2_ref, hn_ref = outs
        x2_ref[...] = x2
        hn_ref[...] = (_rms(x2) * nn_ref[...] * (1.0 + modn_ref[1:2, :]) + modn_ref[0:1, :]).astype(BF16)


def kernel(x_prompt, x_sample, state_gla, c, c_ctx, w_ada, b_ada, norm1, norm2, w_in, w_gk_f, b_gk_f,
           w_gk_b, b_gk_b, gla_norm, w_a_out, conv_w, w_b_out, w_c_out, w_o, w_up, w_down, norm_f):
    b_ctx, seq, d = x_prompt.shape
    b_lat, dec_seq, _ = x_sample.shape
    depth = w_ada.shape[0]
    d_ff = w_down.shape[1]
    nc, nl = b_ctx * seq, b_lat * dec_seq
    nt = nc + nl
    l1, l2 = dec_seq // GRID_W, GRID_W
    assert seq == SUB and d_ff % 128 == 0
    assert nc % dec_seq == 0 and l2 % CT_NB == 0 and l1 % CT_KB == 0
    n_sub_tot, n_sub_lat = nt // SUB, dec_seq // SUB
    assert (nc // SUB) % n_sub_lat == 0
    n_cond = -(-(1 + b_lat) // 8) * 8

    class Tiling:
        def __init__(self, tm):
            assert nc % tm == 0 and dec_seq % tm == 0
            self.tm, self.n, self.n_ctx, self.per_lat = tm, nt // tm, nc // tm, dec_seq // tm

        def cond(self, i):
            return jnp.where(i < self.n_ctx, 0, 1 + (i - self.n_ctx) // self.per_lat)

        def ctx_blk(self, i):
            return jnp.minimum(i, self.n_ctx - 1)

        def lat_blk(self, i):
            return jnp.maximum(i - self.n_ctx, 0)

        def tok(self, w):
            return pl.BlockSpec((self.tm, w), lambda i, *_: (i, 0))

        def tok_ctx(self, w):
            return pl.BlockSpec((self.tm, w), lambda i, *_: (self.ctx_blk(i), 0))

        def tok_lat(self, w):
            return pl.BlockSpec((self.tm, w), lambda i, *_: (self.lat_blk(i), 0))

        def mod(self, layer):
            return pl.BlockSpec((None, None, N_MOD, d), lambda i, *_: (layer, self.cond(i), 0, 0))

    t_gla, t_mix, t_out, t_ffn = Tiling(TM_GLA), Tiling(TM_MIX), Tiling(TM_OUT), Tiling(TM_FFN)
    t_pre = Tiling(math.gcd(math.gcd(nc, dec_seq), TM_PRE))

    cond = jnp.concatenate([c_ctx[None, :], c, jnp.zeros((n_cond - 1 - b_lat, d), F32)], axis=0)
    mods = _ada(cond, w_ada, b_ada).reshape(depth, n_cond, N_MOD, d)

    cc, sc_ = _cos_sin(FN_GW, FN_GW, FN_GW, FN_GW ** -0.5)
    cs_tab = jnp.asarray(np.concatenate([cc, sc_], axis=0), F32).astype(BF16)
    cl, sl = _cos_sin(seq, seq, seq, seq ** -0.5)
    m_ctx_tab = jnp.asarray(np.concatenate([cl, -sl], axis=0), F32).astype(BF16)
    ang = 2.0 * np.pi * ((np.arange(l1)[None, :, None] * (l2 * np.arange(l1)[None, None, :]
                                                           + np.arange(l2)[:, None, None])) % (l1 * l2)) / (l1 * l2)
    m1_tab = jnp.asarray(np.concatenate([np.cos(ang), -np.sin(ang)], axis=1) * l1 ** -0.5, F32).astype(BF16)
    c2, s2 = _cos_sin(l2, l2, l2, l2 ** -0.5)
    m2_tab = jnp.asarray(np.block([[c2, s2], [-s2, c2]]), F32).astype(BF16)

    xc, xl = x_prompt.reshape(nc, d), x_sample.reshape(nl, d)
    x = None

    gla_cols = 2 * N_HEADS * DK + 2 * V_W
    mix_col0 = gla_cols + 2 * LOWRANK
    w_in16 = w_in.astype(BF16)
    w_gla = w_in16[:, :, :gla_cols + GK_PAD]
    w_mix = w_in16[:, :, mix_col0:]
    zf = jnp.zeros((depth, LOWRANK, N_HEADS, DK), F32)
    top = jnp.stack([w_gk_f.reshape(depth, LOWRANK, N_HEADS, DK), zf], axis=3).reshape(depth, LOWRANK, FB_W)
    bot = jnp.stack([zf, w_gk_b.reshape(depth, LOWRANK, N_HEADS, DK)], axis=3).reshape(depth, LOWRANK, FB_W)
    wgk2 = jnp.concatenate([top, bot, jnp.zeros((depth, GK_PAD - 2 * LOWRANK, FB_W), F32)], axis=1).astype(BF16)
    bgk = jnp.stack([b_gk_f.reshape(depth, N_HEADS, DK), b_gk_b.reshape(depth, N_HEADS, DK)],
                    axis=2).reshape(depth, 1, FB_W)
    wb, wa, wc_out, wo = (t.astype(BF16) for t in (w_b_out, w_a_out, w_c_out, w_o))
    wup, wdn = w_up.astype(BF16), w_down.astype(BF16)
    n1, n2, gn = norm1[:, None, :], norm2[:, None, :], gla_norm[:, None, :]
    s0_ctx = jnp.zeros((1, b_ctx, N_HEADS, DV, 2 * DK), F32)
    s0_lat = state_gla.transpose(1, 0, 3, 5, 2, 4).reshape(depth, b_lat, N_HEADS, DV, 2 * DK)

    h = pl.pallas_call(
        functools.partial(_prenorm_kernel, n_ctx_tiles=t_pre.n_ctx),
        grid=(t_pre.n,),
        in_specs=[t_pre.tok_ctx(d), t_pre.tok_lat(d), t_pre.mod(0), _layer_spec(0, (1, d))],
        out_specs=t_pre.tok(d),
        out_shape=jax.ShapeDtypeStruct((nt, d), BF16),
        compiler_params=_params(("arbitrary",)),
        name="prenorm",
    )(xc, xl, mods, n1)

    new_states = []
    y_out = None
    for l in range(depth):
        tok = t_gla.tok
        og, qin, oin, u, g = pl.pallas_call(
            _gla_in_kernel,
            grid=(t_gla.n,),
            in_specs=[tok(d), _layer_spec(l, (d, gla_cols + GK_PAD)), _layer_spec(l, (GK_PAD, FB_W)),
                      _layer_spec(l, (1, FB_W))],
            out_specs=[tok(V_W), tok(FB_W), tok(V_W),
                       pl.BlockSpec((TM_GLA // SUB, N_HEADS, DV, CPS * 2 * DK), lambda i: (i, 0, 0, 0)),
                       pl.BlockSpec((TM_GLA // SUB, N_HEADS, 1, CPS * 2 * DK), lambda i: (i, 0, 0, 0))],
            out_shape=[jax.ShapeDtypeStruct((nt, V_W), BF16), jax.ShapeDtypeStruct((nt, FB_W), BF16),
                       jax.ShapeDtypeStruct((nt, V_W), BF16),
                       jax.ShapeDtypeStruct((n_sub_tot, N_HEADS, DV, CPS * 2 * DK), BF16),
                       jax.ShapeDtypeStruct((n_sub_tot, N_HEADS, 1, CPS * 2 * DK), F32)],
            scratch_shapes=[pltpu.VMEM((TM_GLA, FB_W), F32), pltpu.VMEM((TM_GLA, FB_W), F32),
                            pltpu.VMEM((TM_GLA, V_W), BF16), pltpu.VMEM((TM_GLA, FB_W), F32),
                            pltpu.VMEM((2, SUB, FB_W), F32)],
            compiler_params=_params(("arbitrary",)),
            name="gla_in",
        )(h, w_gla, wgk2, bgk)

        tok = t_mix.tok
        fx, g0, g2, yp = pl.pallas_call(
            functools.partial(_mix_in_kernel, n_ctx_tiles=t_mix.n_ctx, period_ctx=seq, period_lat=GRID_W),
            grid=(t_mix.n,),
            in_specs=[tok(d), _layer_spec(l, (d, 3 * SC_W + FN_W + 3 * d)),
                      _layer_spec(l, (3, SC_W)), _layer_spec(l, (SC_W, d))],
            out_specs=[pl.BlockSpec((FN_GROUPS, TM_MIX, FN_GW), lambda i: (0, i, 0)), tok(d), tok(d), tok(d)],
            out_shape=[jax.ShapeDtypeStruct((FN_GROUPS, nt, FN_GW), F32)] + [jax.ShapeDtypeStruct((nt, d), BF16)] * 3,
            compiler_params=_params(("arbitrary",)),
            name="mix_in",
        )(h, w_mix, conv_w, wb)

        ss_ctx, sfin = _scan(u, g, s0_ctx, 0, b_ctx, seq // SUB, 0, math.gcd(b_ctx, 4), N_HEADS)
        ss_lat, _ = _scan(u, g, s0_lat, l, b_lat, n_sub_lat, nc // SUB, 1, N_HEADS)
        new_states.append(sfin)

        cs_spec = _const_spec((2 * FN_GW, FN_GW))
        sq = math.gcd(b_ctx, 4)
        fr_ctx = pl.pallas_call(
            functools.partial(_dft_direct_kernel, seq=seq),
            grid=(b_ctx // sq,),
            in_specs=[pl.BlockSpec((FN_GROUPS, sq * seq, FN_GW), lambda s: (0, s, 0)),
                      _const_spec((2 * seq, seq)), cs_spec],
            out_specs=pl.BlockSpec((sq * seq, FN_W), lambda s: (s, 0)),
            out_shape=jax.ShapeDtypeStruct((nc, FN_W), BF16),
            compiler_params=_params(("arbitrary",)),
            name="dft_ctx",
        )(fx, m_ctx_tab, cs_tab)
        lat_blk0 = nc // dec_seq
        n_st1, n_st2 = l2 // CT_NB, l1 // CT_KB
        t_rows = max(CT_NB * _pitch(l1), CT_KB * _pitch(l2))
        fr_lat = pl.pallas_call(
            functools.partial(_ct_kernel, l1=l1, l2=l2),
            grid=(b_lat, n_st1 + n_st2),
            in_specs=[pl.BlockSpec((FN_GROUPS, dec_seq, FN_GW), lambda b, j: (0, lat_blk0 + b, 0)),
                      _const_spec((l2, 2 * l1, l1)), _const_spec((2 * l2, 2 * l2)), cs_spec],
            out_specs=pl.BlockSpec((None, l2, CT_KB, FN_W), lambda b, j: (b, 0, jnp.maximum(j - n_st1, 0), 0)),
            out_shape=jax.ShapeDtypeStruct((b_lat, l2, l1, FN_W), BF16),
            scratch_shapes=[pltpu.VMEM((FN_GROUPS, l1 * _pitch(l2), FN_GW), F32),
                            pltpu.VMEM((l1, l2, FN_W), BF16), pltpu.VMEM((l1, l2, FN_W), BF16),
                            pltpu.VMEM((FN_GROUPS, t_rows, FN_GW), F32), pltpu.VMEM((FN_GROUPS, t_rows, FN_GW), F32)],
            compiler_params=_params(("arbitrary", "arbitrary")),
            name="dft_lat",
        )(fx, m1_tab, m2_tab, cs_tab).reshape(nl, FN_W)

        tok = t_out.tok
        split_x = x is None
        ss_spec = lambda blk: pl.BlockSpec((TM_OUT // SUB, N_HEADS, DV, CPS * 2 * DK),
                                           lambda i: (blk(i), 0, 0, 0))
        ya, yc = pl.pallas_call(
            functools.partial(_mix_out_kernel, n_ctx_tiles=t_out.n_ctx),
            grid=(t_out.n,),
            in_specs=[tok(V_W), tok(FB_W), ss_spec(t_out.ctx_blk), ss_spec(t_out.lat_blk), tok(V_W),
                      t_out.tok_ctx(FN_W), t_out.tok_lat(FN_W),
                      _layer_spec(l, (1, DV)), _layer_spec(l, (V_W, d)), _layer_spec(l, (FN_W, d))],
            out_specs=[tok(d), tok(d)],
            out_shape=[jax.ShapeDtypeStruct((nt, d), BF16)] * 2,
            scratch_shapes=[pltpu.VMEM((TM_OUT, V_W), F32)],
            compiler_params=_params(("arbitrary",)),
            name="mix_out",
        )(oin, qin, ss_ctx, ss_lat, og, fr_ctx, fr_lat, gn, wa, wc_out)

        final = l == depth - 1
        tok = t_ffn.tok
        nxt = depth - 1 if final else l + 1
        outs = pl.pallas_call(
            functools.partial(_ffn_kernel, final=final, split_x=split_x, n_ctx_tiles=t_ffn.n_ctx),
            grid=(t_ffn.n,),
            in_specs=([t_ffn.tok_ctx(d), t_ffn.tok_lat(d)] if split_x else [tok(d)]) + [
                tok(d), tok(d), tok(d), tok(d), tok(d),
                _layer_spec(l, (d, d)), _layer_spec(l, (d, 2 * d_ff)), _layer_spec(l, (d_ff, d)),
                t_ffn.mod(l), t_ffn.mod(nxt), _layer_spec(l, (1, d)),
                _const_spec((1, d)) if final else _layer_spec(nxt, (1, d))],
            out_specs=[t_ffn.tok_ctx(d), t_ffn.tok_lat(d)] if final else [tok(d), tok(d)],
            out_shape=([jax.ShapeDtypeStruct((nc, d), F32), jax.ShapeDtypeStruct((nl, d), F32)] if final else
                       [jax.ShapeDtypeStruct((nt, d), F32), jax.ShapeDtypeStruct((nt, d), BF16)]),
            compiler_params=_params(("arbitrary",)),
            name="ffn",
        )(*((xc, xl) if split_x else (x,)), ya, yc, g0, g2, yp, wo, wup, wdn, mods, mods, n2,
          norm_f[None, :] if final else n1)
        if final:
            y_out = outs
        else:
            x, h = outs

    y_prompt = y_out[0].reshape(b_ctx, seq, d)
    y_sample = y_out[1].reshape(b_lat, dec_seq, d)
    sfin_all = jnp.stack(new_states, axis=0).reshape(depth, b_ctx, N_HEADS, DV, 2, DK)
    new_state_gla = sfin_all.transpose(1, 0, 4, 2, 5, 3).astype(x_prompt.dtype)
    return (y_prompt, y_sample, new_state_gla)
```

```python
import functools
import math

import numpy as np
import jax
import jax.numpy as jnp
from jax import lax
from jax.experimental import pallas as pl
from jax.experimental.pallas import tpu as pltpu

F32 = jnp.float32
BF16 = jnp.bfloat16

GRID_W = 64
N_HEADS = 4
DK = 64
DV = 128
V_W = N_HEADS * DV
FB_W = N_HEADS * 2 * DK
LOWRANK = 16
GATE_NORMALIZER = 16.0
CHUNK = 64
SUB = 256
CPS = SUB // CHUNK
SC_W = 512
FN_GROUPS = 4
FN_GW = 128
FN_W = FN_GROUPS * FN_GW
N_MOD = 6
EPS = 1e-6
GK_PAD = 128
SUBLANES = 8

TM_GLA = 1024
TM_MIX = 1024
TM_OUT = 1024
TM_FFN = 512
TM_PRE = 2048
FF_CHUNK = 512
CT_NB = 16
CT_KB = 16
VMEM_LIMIT = 56 * 1024 * 1024


def _dot(a, b):
    return jnp.dot(a, b, preferred_element_type=F32)


def _dot_nt(a, b):
    return lax.dot_general(a, b, (((1,), (1,)), ((), ())), preferred_element_type=F32)


def _dot_tn(a, b):
    return lax.dot_general(a, b, (((0,), (0,)), ((), ())), preferred_element_type=F32)


def _sigmoid(x):
    return 1.0 / (1.0 + jnp.exp(-x))


def _silu(x):
    return x * _sigmoid(x)


def _rms(x):
    return x * lax.rsqrt(jnp.mean(x * x, axis=-1, keepdims=True) + EPS)


def _params(sem):
    return pltpu.CompilerParams(dimension_semantics=sem, vmem_limit_bytes=VMEM_LIMIT)


def _const_spec(shape):
    nd = len(shape)
    return pl.BlockSpec(shape, lambda *_: (0,) * nd, pipeline_mode=pl.Buffered(1))


def _layer_spec(layer, shape):
    nd = len(shape)
    return pl.BlockSpec((None,) + tuple(shape), lambda *_: (layer,) + (0,) * nd, pipeline_mode=pl.Buffered(1))


def _ada_kernel(c_ref, w_ref, b_ref, o_ref):
    s = _silu(c_ref[...]).astype(BF16)
    o_ref[...] = _dot(s, w_ref[...].astype(BF16)) + b_ref[...]


def _ada(cond, w_ada, b_ada):
    depth, d, n = w_ada.shape
    rows = cond.shape[0]
    nb = 1536
    return pl.pallas_call(
        _ada_kernel,
        grid=(depth, n // nb),
        in_specs=[
            pl.BlockSpec((rows, d), lambda l, j: (0, 0)),
            pl.BlockSpec((None, d, nb), lambda l, j: (l, 0, j)),
            pl.BlockSpec((None, 1, nb), lambda l, j: (l, 0, j)),
        ],
        out_specs=pl.BlockSpec((None, rows, nb), lambda l, j: (l, 0, j)),
        out_shape=jax.ShapeDtypeStruct((depth, rows, n), F32),
        compiler_params=_params(("arbitrary", "arbitrary")),
        name="ada",
    )(cond, w_ada, b_ada.reshape(depth, 1, n))


def _prenorm_kernel(xc_ref, xl_ref, mod_ref, n_ref, h_ref, *, n_ctx_tiles):
    x = jnp.where(pl.program_id(0) < n_ctx_tiles, xc_ref[...], xl_ref[...])
    m = mod_ref[...]
    h = _rms(x) * n_ref[...] * (1.0 + m[1:2]) + m[0:1]
    h_ref[...] = h.astype(BF16)


def _gla_in_kernel(h_ref, w_ref, wgk2_ref, bgk_ref,
                   og_ref, qin_ref, oin_ref, u_ref, g_ref,
                   q_s, k_s, v_s, la_s, pre_s):
    qk_w = N_HEADS * DK
    tm = h_ref.shape[0]
    h = h_ref[...]
    gk = _dot(h, w_ref[:, 2 * qk_w + 2 * V_W:]).astype(BF16)
    lp = _dot(gk, wgk2_ref[...]) + bgk_ref[...]
    la = (jnp.minimum(lp, 0.0) - jnp.log(1.0 + jnp.exp(-jnp.abs(lp)))) * (1.0 / GATE_NORMALIZER)
    la_s[...] = la
    z = _dot(h, w_ref[:, :2 * qk_w + 2 * V_W])
    srow = lax.broadcasted_iota(jnp.int32, (1, SUBLANES, 1), 1)

    def chunk_prefix(x):
        x = x.reshape(SUB // SUBLANES, SUBLANES, FB_W)
        for sh in (1, 2, 4):
            x = x + jnp.where(srow >= sh, pltpu.roll(x, sh, axis=1), 0.0)
        groups = [x[i] for i in range(SUB // SUBLANES)]
        per_chunk = CHUNK // SUBLANES
        for i in range(len(groups)):
            if i % per_chunk:
                groups[i] = groups[i] + jnp.broadcast_to(groups[i - 1][SUBLANES - 1:SUBLANES, :], (SUBLANES, FB_W))
        return jnp.concatenate(groups, axis=0)

    lo_half = lax.broadcasted_iota(jnp.int32, (1, 2 * DK), 1) < DK

    def both_directions(x):
        tiles = []
        for p in range(N_HEADS // 2):
            t = x[:, p * 2 * DK:(p + 1) * 2 * DK]
            r = pltpu.roll(t, DK, axis=1)
            tiles += [jnp.where(lo_half, t, r), jnp.where(lo_half, r, t)]
        return jnp.concatenate(tiles, axis=1)

    q_s[...] = both_directions(z[:, :qk_w]) * (DK ** -0.5)
    k_s[...] = both_directions(z[:, qk_w:2 * qk_w])
    v_s[...] = z[:, 2 * qk_w:2 * qk_w + V_W].astype(BF16)
    og_ref[...] = z[:, 2 * qk_w + V_W:].astype(BF16)
    n_sub = tm // SUB
    pre_s[0] = chunk_prefix(la_s[0:SUB, :])

    row = lax.broadcasted_iota(jnp.int32, (SUB, SUB), 0)
    col = lax.broadcasted_iota(jnp.int32, (SUB, SUB), 1)
    same = (row & -CHUNK) == (col & -CHUNK)
    lower = same & (col <= row)
    upper = same & (col >= row)
    is_f = (lax.broadcasted_iota(jnp.int32, (SUB, FB_W), 1) & (2 * DK - 1)) < DK
    rchunk = lax.broadcasted_iota(jnp.int32, (SUB, 2 * DK), 0) & -CHUNK

    def sub_tile(s, carry):
        r0 = pl.multiple_of(s * SUB, SUB)
        rows = pl.ds(r0, SUB)
        la = la_s[rows, :]
        pre = pre_s[s & 1]
        nxt = pl.multiple_of(jnp.minimum(s + 1, n_sub - 1) * SUB, SUB)
        pre_s[(s + 1) & 1] = chunk_prefix(la_s[pl.ds(nxt, SUB), :])
        tot_rows = [pre[c * CHUNK + CHUNK - 1:c * CHUNK + CHUNK, :] for c in range(CPS)]
        tot = jnp.concatenate([jnp.broadcast_to(t, (CHUNK, FB_W)) for t in tot_rows], axis=0)
        b = jnp.where(is_f, pre, tot - pre + la)
        q = q_s[rows, :]
        k = k_s[rows, :]
        qin = (q * jnp.exp(b)).astype(BF16)
        kin = (k * jnp.exp(-b)).astype(BF16)
        kout = (k * jnp.exp(tot - b)).astype(BF16)
        qin_ref[rows, :] = qin
        zero = jnp.zeros_like(qin)
        qf = jnp.where(is_f, qin, zero)
        qb = jnp.where(is_f, zero, qin)
        v = v_s[rows, :]
        for hh in range(N_HEADS):
            fb = slice(hh * 2 * DK, (hh + 1) * 2 * DK)
            vs = slice(hh * DV, (hh + 1) * DV)
            a2 = _dot_nt(jnp.concatenate([qf[:, fb], qb[:, fb]], axis=0), kin[:, fb])
            att = (jnp.where(lower, a2[:SUB], 0.0) + jnp.where(upper, a2[SUB:], 0.0)).astype(BF16)
            oin_ref[rows, vs] = _dot(att, v[:, vs]).astype(BF16)
            ko = kout[:, fb]
            u_ref[s, hh] = jnp.concatenate(
                [_dot_tn(v[c * CHUNK:(c + 1) * CHUNK, vs], ko[c * CHUNK:(c + 1) * CHUNK, :]) for c in range(CPS)],
                axis=1).astype(BF16)
            g_ref[s, hh] = jnp.concatenate([jnp.exp(t[:, fb]) for t in tot_rows], axis=1)
        return carry

    lax.fori_loop(0, h_ref.shape[0] // SUB, sub_tile, 0, unroll=2)


def _mix_in_kernel(h_ref, w_ref, cw_ref, wb_ref,
                   fx_ref, g0_ref, g2_ref, yp_ref, *, n_ctx_tiles, period_ctx, period_lat):
    i = pl.program_id(0)
    d = g0_ref.shape[-1]
    z = _dot(h_ref[...], w_ref[...])
    sb, sc, sx = z[:, :SC_W], z[:, SC_W:2 * SC_W], z[:, 2 * SC_W:3 * SC_W]
    m0 = 3 * SC_W + FN_W
    u = sc * sx
    tm = h_ref.shape[0]
    period = jnp.where(i < n_ctx_tiles, period_ctx, period_lat)
    pos = lax.broadcasted_iota(jnp.int32, (tm, 1), 0) & (period - 1)
    up = jnp.where(pos == 0, 0.0, pltpu.roll(u, 1, axis=0))
    un = jnp.where(pos == period - 1, 0.0, pltpu.roll(u, tm - 1, axis=0))
    cw = cw_ref[...]
    conv = cw[0:1] * up + cw[1:2] * u + cw[2:3] * un
    yb = _dot((sb * conv).astype(BF16), wb_ref[...])

    _put_rows(fx_ref, 0, z[:, 3 * SC_W:m0])
    g0_ref[...] = _sigmoid(z[:, m0:m0 + d]).astype(BF16)
    yp_ref[...] = (_sigmoid(z[:, m0 + d:m0 + 2 * d]) * yb).astype(BF16)
    g2_ref[...] = _sigmoid(z[:, m0 + 2 * d:m0 + 3 * d]).astype(BF16)


def _scan_kernel(u_ref, g_ref, s0_ref, ss_ref, sfin_ref, *, n_sub):
    lane_f = lax.broadcasted_iota(jnp.int32, (DV, 2 * DK), 1) < DK
    for q in range(s0_ref.shape[0]):
        for hh in range(s0_ref.shape[1]):
            s0 = s0_ref[q, hh]

            def fwd(j, st):
                for c in range(CPS):
                    cs = slice(c * 2 * DK, (c + 1) * 2 * DK)
                    ss_ref[j, hh, :, cs] = st.astype(BF16)
                    st = g_ref[j, hh, :, cs] * st + u_ref[j, hh, :, cs].astype(F32)
                return st

            def bwd(j, st):
                for c in range(CPS - 1, -1, -1):
                    cs = slice(c * 2 * DK, (c + 1) * 2 * DK)
                    ss_ref[j, hh, :, cs] = jnp.where(lane_f, ss_ref[j, hh, :, cs], st.astype(BF16))
                    st = g_ref[j, hh, :, cs] * st + u_ref[j, hh, :, cs].astype(F32)
                return st

            lo = q * n_sub
            sf = lax.fori_loop(0, n_sub, lambda i, st: fwd(lo + i, st), s0)
            sb = lax.fori_loop(0, n_sub, lambda i, st: bwd(lo + n_sub - 1 - i, st), s0)
            sfin_ref[q, hh] = jnp.where(lane_f, sf, sb)


def _scan(u, g, s0, layer, n_seq, n_sub, sub0, seq_blk, head_blk):
    rows = seq_blk * n_sub
    assert n_seq % seq_blk == 0 and N_HEADS % head_blk == 0 and sub0 % rows == 0
    wide = CPS * 2 * DK
    return pl.pallas_call(
        functools.partial(_scan_kernel, n_sub=n_sub),
        grid=(n_seq // seq_blk, N_HEADS // head_blk),
        in_specs=[
            pl.BlockSpec((rows, head_blk, DV, wide), lambda b, hh: (sub0 // rows + b, hh, 0, 0)),
            pl.BlockSpec((rows, head_blk, 1, wide), lambda b, hh: (sub0 // rows + b, hh, 0, 0)),
            pl.BlockSpec((None, seq_blk, head_blk, DV, 2 * DK), lambda b, hh: (layer, b, hh, 0, 0)),
        ],
        out_specs=[
            pl.BlockSpec((rows, head_blk, DV, wide), lambda b, hh: (b, hh, 0, 0)),
            pl.BlockSpec((seq_blk, head_blk, DV, 2 * DK), lambda b, hh: (b, hh, 0, 0)),
        ],
        out_shape=[
            jax.ShapeDtypeStruct((n_seq * n_sub, N_HEADS, DV, wide), BF16),
            jax.ShapeDtypeStruct((n_seq, N_HEADS, DV, 2 * DK), F32),
        ],
        compiler_params=_params(("arbitrary", "arbitrary")),
        name="scan",
    )(u, g, s0)


def _gather_rows(ref, start, size, stride):
    return jnp.concatenate([ref[g, pl.ds(start, size, stride=stride), :] for g in range(FN_GROUPS)], axis=1)


def _put_rows(ref, r0, val):
    for g in range(FN_GROUPS):
        ref[g, r0:r0 + val.shape[0], :] = val[:, g * FN_GW:(g + 1) * FN_GW]


def _channel_dft_real(re_g, im_g, cs):
    return _dot(jnp.concatenate([re_g.astype(BF16), im_g.astype(BF16)], axis=1), cs)


def _dft_direct_kernel(x_ref, m_ref, cs_ref, o_ref, *, seq):
    for q in range(x_ref.shape[1] // seq):
        rows = slice(q * seq, (q + 1) * seq)
        x = jnp.concatenate([x_ref[g, rows, :] for g in range(FN_GROUPS)], axis=1)
        res = _dot(m_ref[...], x.astype(BF16))
        for g in range(FN_GROUPS):
            gs = slice(g * FN_GW, (g + 1) * FN_GW)
            o_ref[rows, gs] = _channel_dft_real(res[:seq, gs], res[seq:, gs], cs_ref[...]).astype(BF16)


def _pitch(n):
    return n + 4


def _ct_kernel(x_ref, m1_ref, m2_ref, cs_ref, o_ref, xs_s, yr_s, yi_s, tr_s, ti_s, *, l1, l2):
    j = pl.program_id(1)
    n_stage1 = l2 // CT_NB
    px, pt1, pt2 = _pitch(l2), _pitch(l1), _pitch(l2)

    @pl.when((pl.program_id(0) == 0) & (j == 0))
    def _():
        tr_s[...] = jnp.zeros_like(tr_s)
        ti_s[...] = jnp.zeros_like(ti_s)

    @pl.when(j == 0)
    def _():
        for n1 in range(l1):
            for g in range(FN_GROUPS):
                xs_s[g, n1 * px:n1 * px + l2, :] = x_ref[g, n1 * l2:(n1 + 1) * l2, :]

    @pl.when(j < n_stage1)
    def _():
        col0 = pl.multiple_of(j * CT_NB, CT_NB)
        for jn in range(CT_NB):
            xs = _gather_rows(xs_s, col0 + jn, l1, px)
            res = _dot(m1_ref[col0 + jn], xs.astype(BF16))
            _put_rows(tr_s, jn * pt1, res[:l1])
            _put_rows(ti_s, jn * pt1, res[l1:])
        for k1 in range(l1):
            yr_s[k1, pl.ds(col0, CT_NB), :] = _gather_rows(tr_s, k1, CT_NB, pt1).astype(BF16)
            yi_s[k1, pl.ds(col0, CT_NB), :] = _gather_rows(ti_s, k1, CT_NB, pt1).astype(BF16)

    @pl.when(j >= n_stage1)
    def _():
        m = m2_ref[...]
        k0 = (j - n_stage1) * CT_KB
        for kk in range(CT_KB):
            rhs = jnp.concatenate([yr_s[k0 + kk], yi_s[k0 + kk]], axis=0)
            res = _dot(m, rhs)
            _put_rows(tr_s, kk * pt2, res[:l2])
            _put_rows(ti_s, kk * pt2, res[l2:])
        rows = CT_KB * pt2
        for g in range(FN_GROUPS):
            tr_s[g, :rows, :] = _channel_dft_real(tr_s[g, :rows, :], ti_s[g, :rows, :], cs_ref[...])
        for k2 in range(l2):
            o_ref[k2] = _gather_rows(tr_s, k2, CT_KB, pt2).astype(BF16)


def _cos_sin(n_out, n_in, period, scale):
    idx = (np.arange(n_out)[:, None] * np.arange(n_in)[None, :]) % period
    ang = 2.0 * np.pi * idx / period
    return np.cos(ang) * scale, np.sin(ang) * scale


def _mix_out_kernel(oin_ref, qin_ref, ssc_ref, ssl_ref, og_ref, frc_ref, frl_ref, g0_ref, g2_ref, yp_ref,
                    gn_ref, wa_ref, wc_ref, y_ref, o_s, *, n_ctx_tiles):
    is_ctx = pl.program_id(0) < n_ctx_tiles
    for s in range(oin_ref.shape[0] // SUB):
        for hh in range(N_HEADS):
            st = jnp.where(is_ctx, ssc_ref[s, hh], ssl_ref[s, hh])
            fb = slice(hh * 2 * DK, (hh + 1) * 2 * DK)
            vs = slice(hh * DV, (hh + 1) * DV)
            for c in range(CPS):
                rows = slice(s * SUB + c * CHUNK, s * SUB + (c + 1) * CHUNK)
                o_s[rows, vs] = oin_ref[rows, vs].astype(F32) + _dot_nt(qin_ref[rows, fb], st[:, c * 2 * DK:(c + 1) * 2 * DK])
    gn = gn_ref[...]
    parts = []
    for hh in range(N_HEADS):
        vs = slice(hh * DV, (hh + 1) * DV)
        parts.append((_rms(o_s[:, vs]) * gn * _silu(og_ref[:, vs].astype(F32))).astype(BF16))
    ya = _dot(jnp.concatenate(parts, axis=1), wa_ref[...])
    yc = _dot(jnp.where(is_ctx, frc_ref[...], frl_ref[...]), wc_ref[...])
    y = g0_ref[...].astype(F32) * ya + yp_ref[...].astype(F32) + g2_ref[...].astype(F32) * yc
    y_ref[...] = y.astype(BF16)


def _ffn_kernel(*refs, final, split_x, n_ctx_tiles):
    if split_x:
        xc_ref, xl_ref, *refs = refs
    else:
        x_ref, *refs = refs
    y_ref, wo_ref, wup_ref, wd_ref, mod_ref, modn_ref, n2_ref, nn_ref, *outs = refs
    is_ctx = pl.program_id(0) < n_ctx_tiles
    x = jnp.where(is_ctx, xc_ref[...], xl_ref[...]) if split_x else x_ref[...]
    m = mod_ref[...]
    x1 = x + m[2:3] * _dot(y_ref[...], wo_ref[...])
    h2 = (_rms(x1) * n2_ref[...] * (1.0 + m[4:5]) + m[3:4]).astype(BF16)
    d_ff = wd_ref.shape[0]
    acc = None
    for c0 in range(0, d_ff, FF_CHUNK):
        c1 = min(c0 + FF_CHUNK, d_ff)
        gate = _dot(h2, wup_ref[:, c0:c1])
        up = _dot(h2, wup_ref[:, d_ff + c0:d_ff + c1])
        part = _dot((_silu(gate) * up).astype(BF16), wd_ref[c0:c1, :])
        acc = part if acc is None else acc + part
    x2 = x1 + m[5:6] * acc
    if final:
        yc_ref, yl_ref = outs
        out = _rms(x2) * nn_ref[...]

        @pl.when(is_ctx)
        def _():
            yc_ref[...] = out

        @pl.when(jnp.logical_not(is_ctx))
        def _():
            yl_ref[...] = out
    else:
        x2_ref, hn_ref = outs
        x2_ref[...] = x2
        hn_ref[...] = (_rms(x2) * nn_ref[...] * (1.0 + modn_ref[1:2, :]) + modn_ref[0:1, :]).astype(BF16)


def kernel(x_prompt, x_sample, state_gla, c, c_ctx, w_ada, b_ada, norm1, norm2, w_in, w_gk_f, b_gk_f,
           w_gk_b, b_gk_b, gla_norm, w_a_out, conv_w, w_b_out, w_c_out, w_o, w_up, w_down, norm_f):
    b_ctx, seq, d = x_prompt.shape
    b_lat, dec_seq, _ = x_sample.shape
    depth = w_ada.shape[0]
    d_ff = w_down.shape[1]
    nc, nl = b_ctx * seq, b_lat * dec_seq
    nt = nc + nl
    l1, l2 = dec_seq // GRID_W, GRID_W
    assert seq == SUB and d_ff % 128 == 0
    assert nc % dec_seq == 0 and l2 % CT_NB == 0 and l1 % CT_KB == 0
    n_sub_tot, n_sub_lat = nt // SUB, dec_seq // SUB
    assert (nc // SUB) % n_sub_lat == 0
    n_cond = -(-(1 + b_lat) // 8) * 8

    class Tiling:
        def __init__(self, tm):
            assert nc % tm == 0 and dec_seq % tm == 0
            self.tm, self.n, self.n_ctx, self.per_lat = tm, nt // tm, nc // tm, dec_seq // tm

        def cond(self, i):
            return jnp.where(i < self.n_ctx, 0, 1 + (i - self.n_ctx) // self.per_lat)

        def ctx_blk(self, i):
            return jnp.minimum(i, self.n_ctx - 1)

        def lat_blk(self, i):
            return jnp.maximum(i - self.n_ctx, 0)

        def tok(self, w):
            return pl.BlockSpec((self.tm, w), lambda i, *_: (i, 0))

        def tok_ctx(self, w):
            return pl.BlockSpec((self.tm, w), lambda i, *_: (self.ctx_blk(i), 0))

        def tok_lat(self, w):
            return pl.BlockSpec((self.tm, w), lambda i, *_: (self.lat_blk(i), 0))

        def mod(self, layer):
            return pl.BlockSpec((None, None, N_MOD, d), lambda i, *_: (layer, self.cond(i), 0, 0))

    t_gla, t_mix, t_out, t_ffn = Tiling(TM_GLA), Tiling(TM_MIX), Tiling(TM_OUT), Tiling(TM_FFN)
    t_pre = Tiling(math.gcd(math.gcd(nc, dec_seq), TM_PRE))

    cond = jnp.concatenate([c_ctx[None, :], c, jnp.zeros((n_cond - 1 - b_lat, d), F32)], axis=0)
    mods = _ada(cond, w_ada, b_ada).reshape(depth, n_cond, N_MOD, d)

    cc, sc_ = _cos_sin(FN_GW, FN_GW, FN_GW, FN_GW ** -0.5)
    cs_tab = jnp.asarray(np.concatenate([cc, sc_], axis=0), F32).astype(BF16)
    cl, sl = _cos_sin(seq, seq, seq, seq ** -0.5)
    m_ctx_tab = jnp.asarray(np.concatenate([cl, -sl], axis=0), F32).astype(BF16)
    ang = 2.0 * np.pi * ((np.arange(l1)[None, :, None] * (l2 * np.arange(l1)[None, None, :]
                                                           + np.arange(l2)[:, None, None])) % (l1 * l2)) / (l1 * l2)
    m1_tab = jnp.asarray(np.concatenate([np.cos(ang), -np.sin(ang)], axis=1) * l1 ** -0.5, F32).astype(BF16)
    c2, s2 = _cos_sin(l2, l2, l2, l2 ** -0.5)
    m2_tab = jnp.asarray(np.block([[c2, s2], [-s2, c2]]), F32).astype(BF16)

    xc, xl = x_prompt.reshape(nc, d), x_sample.reshape(nl, d)
    x = None

    gla_cols = 2 * N_HEADS * DK + 2 * V_W
    mix_col0 = gla_cols + 2 * LOWRANK
    w_in16 = w_in.astype(BF16)
    w_gla = w_in16[:, :, :gla_cols + GK_PAD]
    w_mix = w_in16[:, :, mix_col0:]
    zf = jnp.zeros((depth, LOWRANK, N_HEADS, DK), F32)
    top = jnp.stack([w_gk_f.reshape(depth, LOWRANK, N_HEADS, DK), zf], axis=3).reshape(depth, LOWRANK, FB_W)
    bot = jnp.stack([zf, w_gk_b.reshape(depth, LOWRANK, N_HEADS, DK)], axis=3).reshape(depth, LOWRANK, FB_W)
    wgk2 = jnp.concatenate([top, bot, jnp.zeros((depth, GK_PAD - 2 * LOWRANK, FB_W), F32)], axis=1).astype(BF16)
    bgk = jnp.stack([b_gk_f.reshape(depth, N_HEADS, DK), b_gk_b.reshape(depth, N_HEADS, DK)],
                    axis=2).reshape(depth, 1, FB_W)
    wb, wa, wc_out, wo = (t.astype(BF16) for t in (w_b_out, w_a_out, w_c_out, w_o))
    wup, wdn = w_up.astype(BF16), w_down.astype(BF16)
    n1, n2, gn = norm1[:, None, :], norm2[:, None, :], gla_norm[:, None, :]
    s0_ctx = jnp.zeros((1, b_ctx, N_HEADS, DV, 2 * DK), F32)
    s0_lat = state_gla.transpose(1, 0, 3, 5, 2, 4).reshape(depth, b_lat, N_HEADS, DV, 2 * DK)

    h = pl.pallas_call(
        functools.partial(_prenorm_kernel, n_ctx_tiles=t_pre.n_ctx),
        grid=(t_pre.n,),
        in_specs=[t_pre.tok_ctx(d), t_pre.tok_lat(d), t_pre.mod(0), _layer_spec(0, (1, d))],
        out_specs=t_pre.tok(d),
        out_shape=jax.ShapeDtypeStruct((nt, d), BF16),
        compiler_params=_params(("arbitrary",)),
        name="prenorm",
    )(xc, xl, mods, n1)

    new_states = []
    y_out = None
    for l in range(depth):
        tok = t_gla.tok
        og, qin, oin, u, g = pl.pallas_call(
            _gla_in_kernel,
            grid=(t_gla.n,),
            in_specs=[tok(d), _layer_spec(l, (d, gla_cols + GK_PAD)), _layer_spec(l, (GK_PAD, FB_W)),
                      _layer_spec(l, (1, FB_W))],
            out_specs=[tok(V_W), tok(FB_W), tok(V_W),
                       pl.BlockSpec((TM_GLA // SUB, N_HEADS, DV, CPS * 2 * DK), lambda i: (i, 0, 0, 0)),
                       pl.BlockSpec((TM_GLA // SUB, N_HEADS, 1, CPS * 2 * DK), lambda i: (i, 0, 0, 0))],
            out_shape=[jax.ShapeDtypeStruct((nt, V_W), BF16), jax.ShapeDtypeStruct((nt, FB_W), BF16),
                       jax.ShapeDtypeStruct((nt, V_W), BF16),
                       jax.ShapeDtypeStruct((n_sub_tot, N_HEADS, DV, CPS * 2 * DK), BF16),
                       jax.ShapeDtypeStruct((n_sub_tot, N_HEADS, 1, CPS * 2 * DK), F32)],
            scratch_shapes=[pltpu.VMEM((TM_GLA, FB_W), F32), pltpu.VMEM((TM_GLA, FB_W), F32),
                            pltpu.VMEM((TM_GLA, V_W), BF16), pltpu.VMEM((TM_GLA, FB_W), F32),
                            pltpu.VMEM((2, SUB, FB_W), F32)],
            compiler_params=_params(("arbitrary",)),
            name="gla_in",
        )(h, w_gla, wgk2, bgk)

        tok = t_mix.tok
        fx, g0, g2, yp = pl.pallas_call(
            functools.partial(_mix_in_kernel, n_ctx_tiles=t_mix.n_ctx, period_ctx=seq, period_lat=GRID_W),
            grid=(t_mix.n,),
            in_specs=[tok(d), _layer_spec(l, (d, 3 * SC_W + FN_W + 3 * d)),
                      _layer_spec(l, (3, SC_W)), _layer_spec(l, (SC_W, d))],
            out_specs=[pl.BlockSpec((FN_GROUPS, TM_MIX, FN_GW), lambda i: (0, i, 0)), tok(d), tok(d), tok(d)],
            out_shape=[jax.ShapeDtypeStruct((FN_GROUPS, nt, FN_GW), F32)] + [jax.ShapeDtypeStruct((nt, d), BF16)] * 3,
            compiler_params=_params(("arbitrary",)),
            name="mix_in",
        )(h, w_mix, conv_w, wb)

        ss_ctx, sfin = _scan(u, g, s0_ctx, 0, b_ctx, seq // SUB, 0, math.gcd(b_ctx, 4), N_HEADS)
        ss_lat, _ = _scan(u, g, s0_lat, l, b_lat, n_sub_lat, nc // SUB, 1, N_HEADS)
        new_states.append(sfin)

        cs_spec = _const_spec((2 * FN_GW, FN_GW))
        sq = math.gcd(b_ctx, 4)
        fr_ctx = pl.pallas_call(
            functools.partial(_dft_direct_kernel, seq=seq),
            grid=(b_ctx // sq,),
            in_specs=[pl.BlockSpec((FN_GROUPS, sq * seq, FN_GW), lambda s: (0, s, 0)),
                      _const_spec((2 * seq, seq)), cs_spec],
            out_specs=pl.BlockSpec((sq * seq, FN_W), lambda s: (s, 0)),
            out_shape=jax.ShapeDtypeStruct((nc, FN_W), BF16),
            compiler_params=_params(("arbitrary",)),
            name="dft_ctx",
        )(fx, m_ctx_tab, cs_tab)
        lat_blk0 = nc // dec_seq
        n_st1, n_st2 = l2 // CT_NB, l1 // CT_KB
        t_rows = max(CT_NB * _pitch(l1), CT_KB * _pitch(l2))
        fr_lat = pl.pallas_call(
            functools.partial(_ct_kernel, l1=l1, l2=l2),
            grid=(b_lat, n_st1 + n_st2),
            in_specs=[pl.BlockSpec((FN_GROUPS, dec_seq, FN_GW), lambda b, j: (0, lat_blk0 + b, 0)),
                      _const_spec((l2, 2 * l1, l1)), _const_spec((2 * l2, 2 * l2)), cs_spec],
            out_specs=pl.BlockSpec((None, l2, CT_KB, FN_W), lambda b, j: (b, 0, jnp.maximum(j - n_st1, 0), 0)),
            out_shape=jax.ShapeDtypeStruct((b_lat, l2, l1, FN_W), BF16),
            scratch_shapes=[pltpu.VMEM((FN_GROUPS, l1 * _pitch(l2), FN_GW), F32),
                            pltpu.VMEM((l1, l2, FN_W), BF16), pltpu.VMEM((l1, l2, FN_W), BF16),
                            pltpu.VMEM((FN_GROUPS, t_rows, FN_GW), F32), pltpu.VMEM((FN_GROUPS, t_rows, FN_GW), F32)],
            compiler_params=_params(("arbitrary", "arbitrary")),
            name="dft_lat",
        )(fx, m1_tab, m2_tab, cs_tab).reshape(nl, FN_W)

        tok = t_out.tok
        split_x = x is None
        ss_spec = lambda blk: pl.BlockSpec((TM_OUT // SUB, N_HEADS, DV, CPS * 2 * DK),
                                           lambda i: (blk(i), 0, 0, 0))
        y_mix = pl.pallas_call(
            functools.partial(_mix_out_kernel, n_ctx_tiles=t_out.n_ctx),
            grid=(t_out.n,),
            in_specs=[tok(V_W), tok(FB_W), ss_spec(t_out.ctx_blk), ss_spec(t_out.lat_blk), tok(V_W),
                      t_out.tok_ctx(FN_W), t_out.tok_lat(FN_W), tok(d), tok(d), tok(d),
                      _layer_spec(l, (1, DV)), _layer_spec(l, (V_W, d)), _layer_spec(l, (FN_W, d))],
            out_specs=tok(d),
            out_shape=jax.ShapeDtypeStruct((nt, d), BF16),
            scratch_shapes=[pltpu.VMEM((TM_OUT, V_W), F32)],
            compiler_params=_params(("arbitrary",)),
            name="mix_out",
        )(oin, qin, ss_ctx, ss_lat, og, fr_ctx, fr_lat, g0, g2, yp, gn, wa, wc_out)

        final = l == depth - 1
        tok = t_ffn.tok
        nxt = depth - 1 if final else l + 1
        outs = pl.pallas_call(
            functools.partial(_ffn_kernel, final=final, split_x=split_x, n_ctx_tiles=t_ffn.n_ctx),
            grid=(t_ffn.n,),
            in_specs=([t_ffn.tok_ctx(d), t_ffn.tok_lat(d)] if split_x else [tok(d)]) + [
                tok(d), _layer_spec(l, (d, d)), _layer_spec(l, (d, 2 * d_ff)), _layer_spec(l, (d_ff, d)),
                t_ffn.mod(l), t_ffn.mod(nxt), _layer_spec(l, (1, d)),
                _const_spec((1, d)) if final else _layer_spec(nxt, (1, d))],
            out_specs=[t_ffn.tok_ctx(d), t_ffn.tok_lat(d)] if final else [tok(d), tok(d)],
            out_shape=([jax.ShapeDtypeStruct((nc, d), F32), jax.ShapeDtypeStruct((nl, d), F32)] if final else
                       [jax.ShapeDtypeStruct((nt, d), F32), jax.ShapeDtypeStruct((nt, d), BF16)]),
            compiler_params=_params(("arbitrary",)),
            name="ffn",
        )(*((xc, xl) if split_x else (x,)), y_mix, wo, wup, wdn, mods, mods, n2, norm_f[None, :] if final else n1)
        if final:
            y_out = outs
        else:
            x, h = outs

    y_prompt = y_out[0].reshape(b_ctx, seq, d)
    y_sample = y_out[1].reshape(b_lat, dec_seq, d)
    sfin_all = jnp.stack(new_states, axis=0).reshape(depth, b_ctx, N_HEADS, DV, 2, DK)
    new_state_gla = sfin_all.transpose(1, 0, 4, 2, 5, 3).astype(x_prompt.dtype)
    return (y_prompt, y_sample, new_state_gla)
```
